```python
import math
import jax, jax.numpy as jnp
from jax import lax
import numpy as np

D_MODEL = 2048
BATCH = 4
SEQ = 4096
DEPTH = 1

N_MEM = 256
NORM_EPS = 1e-6

SSD_WIDTH = D_MODEL // 2
SSD_HEAD_DIM = 64
SSD_HEADS = SSD_WIDTH // SSD_HEAD_DIM
SSD_GROUPS = 2
SSD_HEADS_PER_GROUP = SSD_HEADS // SSD_GROUPS
SSD_STATE = 128
SSD_CONV = 4
SSD_CHUNK = 128
SSD_CONV_DIM = SSD_WIDTH + 2 * SSD_GROUPS * SSD_STATE
SSD_IN = SSD_WIDTH + SSD_CONV_DIM + SSD_HEADS

RWKV_WIDTH = D_MODEL - SSD_WIDTH
RWKV_HEAD_DIM = 64
RWKV_HEADS = RWKV_WIDTH // RWKV_HEAD_DIM
RWKV_DECAY_RANK = 96
RWKV_AAA_RANK = 96
RWKV_GATE_RANK = 256
RWKV_IN = 3 * RWKV_WIDTH + RWKV_DECAY_RANK + RWKV_AAA_RANK + RWKV_GATE_RANK
RWKV_LN_EPS = 64e-5

MIX_WIDTH = SSD_WIDTH + RWKV_WIDTH
D_IN = SSD_IN + RWKV_IN

XATTN_HEADS = 4
XATTN_HEAD_DIM = D_MODEL // XATTN_HEADS

D_FF = 4 * D_MODEL

kernel_name = "hymba_ssd_rwkv7_memxattn_block"


def rms_norm(x, g, eps=NORM_EPS):
    xf = x.astype(jnp.float32)
    y = xf * lax.rsqrt(jnp.mean(xf * xf, axis=-1, keepdims=True) + eps)
    return (y * g).astype(x.dtype)


def causal_depthwise_conv(u, w, b):
    y = lax.conv_general_dilated(
        u, w, window_strides=(1,), padding=[(w.shape[0] - 1, 0)],
        dimension_numbers=("NWC", "WIO", "NWC"), feature_group_count=u.shape[-1])
    return y + b


def ssd_mixer(u, conv_w, conv_b, dt_bias, a_log, d_skip, norm_g):
    f32 = jnp.float32
    bsz, seq, _ = u.shape
    G, E, P, N, Q = SSD_GROUPS, SSD_HEADS_PER_GROUP, SSD_HEAD_DIM, SSD_STATE, SSD_CHUNK
    nc = seq // Q
    z, xbc, dt = jnp.split(u, [SSD_WIDTH, SSD_WIDTH + SSD_CONV_DIM], axis=-1)
    xbc = jax.nn.silu(causal_depthwise_conv(xbc, conv_w, conv_b))
    xs, bm, cm = jnp.split(xbc, [SSD_WIDTH, SSD_WIDTH + G * N], axis=-1)
    xs = xs.astype(f32).reshape(bsz, nc, Q, G, E, P)
    bm = bm.astype(f32).reshape(bsz, nc, Q, G, N)
    cm = cm.astype(f32).reshape(bsz, nc, Q, G, N)
    dt = jax.nn.softplus(dt.astype(f32) + dt_bias.astype(f32))
    a = -jnp.exp(a_log.astype(f32))
    dt_c = dt.reshape(bsz, nc, Q, G, E)
    xdt = xs * dt_c[..., None]
    da = jnp.transpose(dt_c * a.reshape(G, E), (0, 1, 3, 4, 2))
    cs = jnp.cumsum(da, axis=-1)
    causal = jnp.tril(jnp.ones((Q, Q), dtype=bool))
    seg = cs[..., :, None] - cs[..., None, :]
    lmat = jnp.where(causal, jnp.exp(jnp.where(causal, seg, 0.0)), 0.0)
    cb = jnp.einsum('bclgn,bcsgn->bcgls', cm, bm)
    y_diag = jnp.einsum('bcgls,bcgels,bcsgep->bclgep', cb, lmat, xdt)
    decay_to_end = jnp.exp(cs[..., -1:] - cs)
    chunk_states = jnp.einsum('bcsgn,bcges,bcsgep->bcgepn', bm, decay_to_end, xdt)
    chunk_decay = jnp.exp(cs[..., -1])

    def carry_state(h, inp):
        st, dec = inp
        return h * dec[..., None, None] + st, h

    h0 = jnp.zeros((bsz, G, E, P, N), f32)
    _, start_states = lax.scan(carry_state, h0,
                               (jnp.moveaxis(chunk_states, 1, 0), jnp.moveaxis(chunk_decay, 1, 0)))
    start_states = jnp.moveaxis(start_states, 0, 1)
    y_off = jnp.einsum('bclgn,bcgepn,bcgel->bclgep', cm, start_states, jnp.exp(cs))
    y = y_diag + y_off + xs * d_skip.astype(f32).reshape(G, E, 1)
    y = y.reshape(bsz, seq, SSD_WIDTH) * jax.nn.silu(z.astype(f32))
    yg = y.reshape(bsz, seq, G, SSD_WIDTH // G)
    yg = yg * lax.rsqrt(jnp.mean(yg * yg, axis=-1, keepdims=True) + NORM_EPS)
    y = yg.reshape(bsz, seq, SSD_WIDTH) * norm_g
    return y.astype(u.dtype)


def rwkv7_mixer(u, mu, w0, w2, a0, a2, g2, k_k, k_a, r_k, ln_w, ln_b):
    f32 = jnp.float32
    bsz, seq, _ = u.shape
    H, N, W = RWKV_HEADS, RWKV_HEAD_DIM, RWKV_WIDTH
    uf = u.astype(f32)
    u_prev = jnp.pad(uf, ((0, 0), (1, 0), (0, 0)))[:, :-1]
    uf = uf + (u_prev - uf) * mu
    r, k, v, pw, pa, pg = jnp.split(
        uf, [W, 2 * W, 3 * W, 3 * W + RWKV_DECAY_RANK,
             3 * W + RWKV_DECAY_RANK + RWKV_AAA_RANK], axis=-1)
    w_log = -jax.nn.softplus(-(w0 + jnp.tanh(pw) @ w2)) - 0.5
    decay = jnp.exp(-jnp.exp(w_log))
    iclr = jax.nn.sigmoid(a0 + pa @ a2)
    gate = jax.nn.sigmoid(pg) @ g2
    heads = lambda t: t.reshape(bsz, seq, H, N)
    kk = heads(k * k_k)
    kk = kk / jnp.maximum(jnp.sqrt(jnp.sum(kk * kk, axis=-1, keepdims=True)), 1e-12)
    k = k * (1.0 + (iclr - 1.0) * k_a)
    r, k, v, decay, iclr = heads(r), heads(k), heads(v), heads(decay), heads(iclr)

    def step(state, inp):
        r_t, w_t, k_t, v_t, kk_t, a_t = inp
        sa = jnp.einsum('bhij,bhj->bhi', state, -kk_t)
        state = (state * w_t[:, :, None, :]
                 + sa[..., None] * (kk_t * a_t)[:, :, None, :]
                 + v_t[..., None] * k_t[:, :, None, :])
        return state, jnp.einsum('bhij,bhj->bhi', state, r_t)

    seq_first = lambda t: jnp.moveaxis(t, 1, 0)
    s0 = jnp.zeros((bsz, H, N, N), f32)
    _, y = lax.scan(step, s0, (seq_first(r), seq_first(decay), seq_first(k),
                               seq_first(v), seq_first(kk), seq_first(iclr)))
    y = jnp.moveaxis(y, 0, 1)
    mean = jnp.mean(y, axis=-1, keepdims=True)
    var = jnp.mean(jnp.square(y - mean), axis=-1, keepdims=True)
    y = ((y - mean) * lax.rsqrt(var + RWKV_LN_EPS)).reshape(bsz, seq, W) * ln_w + ln_b
    bonus = jnp.sum(r * k * r_k, axis=-1, keepdims=True) * v
    y = (y + bonus.reshape(bsz, seq, W)) * gate
    return y.astype(u.dtype)


def memory_cross_attention(h, m, wq, wk, wv, wo):
    bsz, seq, _ = h.shape
    q = (h @ wq).reshape(bsz, seq, XATTN_HEADS, XATTN_HEAD_DIM)
    k = (m @ wk).reshape(bsz, m.shape[1], XATTN_HEADS, XATTN_HEAD_DIM)
    v = (m @ wv).reshape(bsz, m.shape[1], XATTN_HEADS, XATTN_HEAD_DIM)
    scores = jnp.einsum('bshd,bmhd->bhsm', q, k).astype(jnp.float32) * (XATTN_HEAD_DIM ** -0.5)
    p = jax.nn.softmax(scores, axis=-1).astype(v.dtype)
    o = jnp.einsum('bhsm,bmhd->bshd', p, v).reshape(bsz, seq, D_MODEL)
    return o @ wo


def setup_inputs(seed: int = 0) -> dict:
    key = jax.random.key(seed)
    ks = iter(jax.random.split(key, 40))
    nrm = lambda shape, scale: jax.random.normal(next(ks), shape, jnp.float32) * scale
    uni = lambda shape, lo, hi: jax.random.uniform(next(ks), shape, jnp.float32, lo, hi)
    L = DEPTH
    dt0 = jnp.exp(uni((L, SSD_HEADS), math.log(1e-3), math.log(1e-1)))
    return {
        "x": nrm((BATCH, SEQ, D_MODEL), 1.0),
        "mem": nrm((BATCH, N_MEM, D_MODEL), 1.0),
        "norm_mix_g": 1.0 + nrm((L, D_MODEL), 0.02),
        "w_in": nrm((L, D_MODEL, D_IN), D_MODEL ** -0.5),
        "ssd_conv_w": nrm((L, SSD_CONV, 1, SSD_CONV_DIM), SSD_CONV ** -0.5),
        "ssd_conv_b": nrm((L, SSD_CONV_DIM), 0.01),
        "ssd_dt_bias": dt0 + jnp.log(-jnp.expm1(-dt0)),
        "ssd_a_log": jnp.log(uni((L, SSD_HEADS), 1.0, 16.0)),
        "ssd_d": 1.0 + nrm((L, SSD_HEADS), 0.1),
        "ssd_norm_g": 1.0 + nrm((L, SSD_WIDTH), 0.02),
        "rwkv_mu": uni((L, RWKV_IN), 0.0, 1.0),
        "rwkv_w0": uni((L, RWKV_WIDTH), -6.0, -1.0),
        "rwkv_w2": nrm((L, RWKV_DECAY_RANK, RWKV_WIDTH), 0.5 * RWKV_DECAY_RANK ** -0.5),
        "rwkv_a0": nrm((L, RWKV_WIDTH), 0.1),
        "rwkv_a2": nrm((L, RWKV_AAA_RANK, RWKV_WIDTH), RWKV_AAA_RANK ** -0.5),
        "rwkv_g2": nrm((L, RWKV_GATE_RANK, RWKV_WIDTH), RWKV_GATE_RANK ** -0.5),
        "rwkv_k_k": 0.85 + nrm((L, RWKV_WIDTH), 0.05),
        "rwkv_k_a": 1.0 + nrm((L, RWKV_WIDTH), 0.05),
        "rwkv_r_k": nrm((L, RWKV_HEADS, RWKV_HEAD_DIM), 0.1),
        "rwkv_ln_w": 1.0 + nrm((L, RWKV_WIDTH), 0.02),
        "rwkv_ln_b": nrm((L, RWKV_WIDTH), 0.01),
        "w_out": nrm((L, MIX_WIDTH, D_MODEL), MIX_WIDTH ** -0.5),
        "norm_x_g": 1.0 + nrm((L, D_MODEL), 0.02),
        "norm_mem_g": 1.0 + nrm((L, D_MODEL), 0.02),
        "xattn_wq": nrm((L, D_MODEL, D_MODEL), D_MODEL ** -0.5),
        "xattn_wk": nrm((L, D_MODEL, D_MODEL), D_MODEL ** -0.5),
        "xattn_wv": nrm((L, D_MODEL, D_MODEL), D_MODEL ** -0.5),
        "xattn_wo": nrm((L, D_MODEL, D_MODEL), D_MODEL ** -0.5),
        "norm_ffn_g": 1.0 + nrm((L, D_MODEL), 0.02),
        "ffn_w1": nrm((L, D_MODEL, D_FF), D_MODEL ** -0.5),
        "ffn_w2": nrm((L, D_FF, D_MODEL), D_FF ** -0.5),
        "final_norm_g": 1.0 + nrm((D_MODEL,), 0.02),
    }


def reference(x, mem, norm_mix_g, w_in, ssd_conv_w, ssd_conv_b, ssd_dt_bias, ssd_a_log,
              ssd_d, ssd_norm_g, rwkv_mu, rwkv_w0, rwkv_w2, rwkv_a0, rwkv_a2, rwkv_g2,
              rwkv_k_k, rwkv_k_a, rwkv_r_k, rwkv_ln_w, rwkv_ln_b, w_out, norm_x_g,
              norm_mem_g, xattn_wq, xattn_wk, xattn_wv, xattn_wo, norm_ffn_g, ffn_w1,
              ffn_w2, final_norm_g):
    for l in range(DEPTH):
        h = rms_norm(x, norm_mix_g[l])
        u = h @ w_in[l]
        y_ssd = ssd_mixer(u[..., :SSD_IN], ssd_conv_w[l], ssd_conv_b[l], ssd_dt_bias[l],
                          ssd_a_log[l], ssd_d[l], ssd_norm_g[l])
        y_rwkv = rwkv7_mixer(u[..., SSD_IN:], rwkv_mu[l], rwkv_w0[l], rwkv_w2[l], rwkv_a0[l],
                             rwkv_a2[l], rwkv_g2[l], rwkv_k_k[l], rwkv_k_a[l], rwkv_r_k[l],
                             rwkv_ln_w[l], rwkv_ln_b[l])
        x = x + jnp.concatenate([y_ssd, y_rwkv], axis=-1) @ w_out[l]
        h = rms_norm(x, norm_x_g[l])
        m = rms_norm(mem, norm_mem_g[l])
        x = x + memory_cross_attention(h, m, xattn_wq[l], xattn_wk[l], xattn_wv[l], xattn_wo[l])
        h = rms_norm(x, norm_ffn_g[l])
        x = x + jnp.square(jax.nn.relu(h @ ffn_w1[l])) @ ffn_w2[l]
    return rms_norm(x, final_norm_g)
```

```python
import functools

import numpy as np
import jax
import jax.numpy as jnp
from jax import lax
from jax.experimental import pallas as pl
from jax.experimental.pallas import tpu as pltpu

F32 = jnp.float32
MXU_DTYPE = jnp.bfloat16

NORM_EPS = 1e-6
RWKV_LN_EPS = 64e-5

HEAD = 64
SSD_STATE = 128
SSD_GROUPS = 2
SSD_CHUNK = 128
SSD_CONV = 4
RWKV_CHUNK = 64
RWKV_DECAY_RANK = 96
RWKV_AAA_RANK = 96
RWKV_GATE_RANK = 256
XATTN_HEADS = 4

LANES = 128
SUBLANES = 8
VMEM_LIMIT = 56 * 1024 * 1024

W_MIX = 1024
COL_Z, COL_XS, COL_R, COL_K, COL_V = 0, 1024, 2048, 3072, 4096
COL_BC, COL_PG, COL_MISC = 5120, 5632, 5888
N_U = 6144
MISC_PW, MISC_DT, MISC_PA = 0, 96, 128


def _mx(a):
    return a.astype(MXU_DTYPE)


def _dot(a, b):
    return jnp.dot(a, b, preferred_element_type=F32)


def _dot_nt(a, b):
    return lax.dot_general(a, b, (((1,), (1,)), ((), ())), preferred_element_type=F32)


def _split(v, parts):
    out = []
    rem = v
    for _ in range(parts):
        p = rem.astype(MXU_DTYPE)
        out.append(p)
        rem = rem - p.astype(F32)
    return out


def _dot_split_rhs(a01, v, parts):
    acc = None
    for p in _split(v, parts):
        t = _dot(a01, p)
        acc = t if acc is None else acc + t
    return acc


def _dot_split_lhs(v, b01, parts):
    acc = None
    for p in _split(v, parts):
        t = _dot(p, b01)
        acc = t if acc is None else acc + t
    return acc


def _sigmoid(x):
    return 1.0 / (1.0 + jnp.exp(-x))


def _softplus(x):
    return jnp.maximum(x, 0.0) + jnp.log(1.0 + jnp.exp(-jnp.abs(x)))


def _cparams(sem):
    return pltpu.CompilerParams(dimension_semantics=sem, vmem_limit_bytes=VMEM_LIMIT)


def _rmsnorm_body(x_ref, g_ref, o_ref):
    x = x_ref[...]
    ms = jnp.mean(x * x, axis=-1, keepdims=True)
    o_ref[...] = (x * lax.rsqrt(ms + NORM_EPS) * g_ref[...]).astype(o_ref.dtype)


def _rmsnorm(x, g, out_dtype, tm=512):
    m, d = x.shape
    tm = min(tm, m)
    return pl.pallas_call(
        _rmsnorm_body,
        grid=(m // tm,),
        in_specs=[pl.BlockSpec((tm, d), lambda i: (i, 0)),
                  pl.BlockSpec((1, d), lambda i: (0, 0))],
        out_specs=pl.BlockSpec((tm, d), lambda i: (i, 0)),
        out_shape=jax.ShapeDtypeStruct((m, d), out_dtype),
        compiler_params=_cparams(("arbitrary",)),
        name="rmsnorm",
    )(x, g.reshape(1, d))


def _mm_body(*refs, n_pairs, nk, has_res, act):
    ab = refs[:2 * n_pairs]
    pos = 2 * n_pairs
    res_ref = refs[pos] if has_res else None
    pos += int(has_res)
    o_ref = refs[pos]
    acc_ref = refs[pos + 1] if nk > 1 else None

    part = None
    for p in range(n_pairs):
        t = _dot(ab[2 * p][...], ab[2 * p + 1][...])
        part = t if part is None else part + t

    def finish(r):
        if act == "relu2":
            r = jnp.square(jnp.maximum(r, 0.0))
        if has_res:
            r = res_ref[...] + r
        o_ref[...] = r.astype(o_ref.dtype)

    if nk == 1:
        finish(part)
        return

    k = pl.program_id(2)

    @pl.when(k == 0)
    def _():
        acc_ref[...] = part

    @pl.when(k > 0)
    def _():
        acc_ref[...] += part

    @pl.when(k == nk - 1)
    def _():
        finish(acc_ref[...])


def _matmul(pairs, out_dtype, res=None, act=None, tm=1024, tn=1024, tk=2048, name="matmul"):
    m, kdim = pairs[0][0].shape
    n = pairs[0][1].shape[1]
    tm, tn, tk = min(tm, m), min(tn, n), min(tk, kdim)
    nk = kdim // tk
    in_specs, args = [], []
    for a, b in pairs:
        in_specs += [pl.BlockSpec((tm, tk), lambda i, j, k: (i, k)),
                     pl.BlockSpec((tk, tn), lambda i, j, k: (k, j))]
        args += [a, b]
    if res is not None:
        in_specs.append(pl.BlockSpec((tm, tn), lambda i, j, k: (i, j)))
        args.append(res)
    scratch = [pltpu.VMEM((tm, tn), F32)] if nk > 1 else []
    return pl.pallas_call(
        functools.partial(_mm_body, n_pairs=len(pairs), nk=nk, has_res=res is not None, act=act),
        grid=(m // tm, n // tn, nk),
        in_specs=in_specs,
        out_specs=pl.BlockSpec((tm, tn), lambda i, j, k: (i, j)),
        out_shape=jax.ShapeDtypeStruct((m, n), out_dtype),
        scratch_shapes=scratch,
        compiler_params=_cparams(("arbitrary", "arbitrary", "arbitrary")),
        name=name,
    )(*args)


def _ssd_body(z_ref, xs_ref, bc_ref, m_ref, cwx_ref, cbx_ref, cwbc_ref, cbbc_ref, dtb_ref,
              alog_ref, dvec_ref, ng_ref, esel_ref, o_ref, xbuf, bcbuf, st_ref):
    q = SSD_CHUNK
    n = SSD_STATE
    gw = st_ref.shape[2]
    c = pl.program_id(1)

    @pl.when(c == 0)
    def _():
        xbuf[0:SUBLANES, :] = jnp.zeros((SUBLANES, xbuf.shape[1]), F32)
        bcbuf[0:SUBLANES, :] = jnp.zeros((SUBLANES, bcbuf.shape[1]), F32)
        st_ref[...] = jnp.zeros(st_ref.shape, F32)

    @pl.when(c > 0)
    def _():
        xbuf[0:SUBLANES, :] = xbuf[q:q + SUBLANES, :]
        bcbuf[0:SUBLANES, :] = bcbuf[q:q + SUBLANES, :]

    xbuf[SUBLANES:SUBLANES + q, :] = xs_ref[...]
    bcbuf[SUBLANES:SUBLANES + q, :] = bc_ref[...]

    def conv_silu(buf, w_ref, b_ref):
        acc = None
        for k in range(SSD_CONV):
            off = SUBLANES - (SSD_CONV - 1) + k
            t = buf[off:off + q, :] * w_ref[k:k + 1, :]
            acc = t if acc is None else acc + t
        acc = acc + b_ref[...]
        return acc * _sigmoid(acc)

    xc = conv_silu(xbuf, cwx_ref, cbx_ref)
    bcc = conv_silu(bcbuf, cwbc_ref, cbbc_ref)
    g_n = SSD_GROUPS * n

    lane = lax.broadcasted_iota(jnp.int32, (1, LANES), 1)
    n_heads = SSD_GROUPS * gw // HEAD
    dmask = (lane >= MISC_DT) & (lane < MISC_DT + n_heads)
    dt = jnp.where(dmask, _softplus(m_ref[...] + dtb_ref[...]), 0.0)
    a = jnp.where(dmask, -jnp.exp(alog_ref[...]), 0.0)
    da = dt * a
    row = lax.broadcasted_iota(jnp.int32, (q, q), 0)
    col = lax.broadcasted_iota(jnp.int32, (q, q), 1)
    tri = row >= col
    tril = jnp.where(tri, 1.0, 0.0).astype(MXU_DTYPE)
    cs = _dot_split_rhs(tril, da, 3)
    ecs = jnp.exp(cs)
    dte = jnp.exp(cs[q - 1:q, :] - cs)
    esel = esel_ref[...]
    dt_e = _dot_split_lhs(dt, esel, 2)
    ecs_e = _dot_split_lhs(ecs, esel, 2)
    dte_e = _dot_split_lhs(dte, esel, 2)
    cs_t = cs.T

    xdt = xc * dt_e
    xdt_m = _mx(xdt)
    xd_m = _mx(xdt * dte_e)
    lane_lo = lane < HEAD

    y_cols = []
    for g in range(SSD_GROUPS):
        bg = bcc[:, g * n:(g + 1) * n]
        cg = _mx(bcc[:, g_n + g * n:g_n + (g + 1) * n])
        cb = _dot_nt(cg, _mx(bg))
        gs = slice(g * gw, (g + 1) * gw)
        st = st_ref[g]
        y_off = _dot(cg, _mx(st)) * ecs_e[:, gs]
        st_ref[g] = st * ecs_e[q - 1:q, gs] + _dot(_mx(bg.T), xd_m[:, gs])
        for pr in range(gw // LANES):
            h0 = (g * gw + pr * LANES) // HEAD
            ps = slice(g * gw + pr * LANES, g * gw + (pr + 1) * LANES)
            res = []
            for hh in (h0, h0 + 1):
                li = MISC_DT + hh
                seg = cs[:, li:li + 1] - cs_t[li:li + 1, :]
                lm = jnp.where(tri, jnp.exp(jnp.where(tri, seg, 0.0)), 0.0)
                res.append(_dot(_mx(cb * lm), xdt_m[:, ps]))
            y_diag = jnp.where(lane_lo, res[0], res[1])
            y_cols.append(y_diag + y_off[:, pr * LANES:(pr + 1) * LANES])
    y = jnp.concatenate(y_cols, axis=1) + xc * dvec_ref[...]
    zz = z_ref[...]
    y = y * (zz * _sigmoid(zz))
    outs = []
    for g in range(SSD_GROUPS):
        yg = y[:, g * gw:(g + 1) * gw]
        ms = jnp.mean(yg * yg, axis=-1, keepdims=True)
        outs.append(yg * lax.rsqrt(ms + NORM_EPS))
    y = jnp.concatenate(outs, axis=1) * ng_ref[...]
    o_ref[...] = y.astype(o_ref.dtype)


def _ssd(u, bsz, seq, conv_w, conv_b, dt_bias, a_log, d_skip, norm_g):
    q = SSD_CHUNK
    nc = seq // q
    w = W_MIX
    gw = w // SSD_GROUPS
    heads = w // HEAD
    bcw = 2 * SSD_GROUPS * SSD_STATE
    cw = conv_w[:, 0, :]
    pad = lambda v: jnp.zeros((1, LANES), F32).at[0, MISC_DT:MISC_DT + heads].set(v)
    esel = np.zeros((LANES, w), np.float32)
    for h in range(heads):
        esel[MISC_DT + h, h * HEAD:(h + 1) * HEAD] = 1.0
    rowblk = lambda cb: (lambda b, c: (b * nc + c, cb))
    const = lambda b, c: (0, 0)
    return pl.pallas_call(
        _ssd_body,
        grid=(bsz, nc),
        in_specs=[pl.BlockSpec((q, w), rowblk(COL_Z // w)),
                  pl.BlockSpec((q, w), rowblk(COL_XS // w)),
                  pl.BlockSpec((q, bcw), rowblk(COL_BC // bcw)),
                  pl.BlockSpec((q, LANES), rowblk(COL_MISC // LANES)),
                  pl.BlockSpec((SSD_CONV, w), const), pl.BlockSpec((1, w), const),
                  pl.BlockSpec((SSD_CONV, bcw), const), pl.BlockSpec((1, bcw), const),
                  pl.BlockSpec((1, LANES), const), pl.BlockSpec((1, LANES), const),
                  pl.BlockSpec((1, w), const), pl.BlockSpec((1, w), const),
                  pl.BlockSpec((LANES, w), const)],
        out_specs=pl.BlockSpec((q, w), lambda b, c: (b * nc + c, 0)),
        out_shape=jax.ShapeDtypeStruct((bsz * seq, w), MXU_DTYPE),
        scratch_shapes=[pltpu.VMEM((q + SUBLANES, w), F32),
                        pltpu.VMEM((q + SUBLANES, bcw), F32),
                        pltpu.VMEM((SSD_GROUPS, SSD_STATE, gw), F32)],
        compiler_params=_cparams(("arbitrary", "arbitrary")),
        name="ssd_mixer",
    )(u, u, u, u, cw[:, :w], conv_b[:w].reshape(1, w), cw[:, w:], conv_b[w:].reshape(1, bcw),
      pad(dt_bias), pad(a_log), jnp.repeat(d_skip, HEAD).reshape(1, w), norm_g.reshape(1, w),
      jnp.asarray(esel, MXU_DTYPE))


def _unit_lower_inverse(l_mat, blk_mask, eye_f):
    d = jnp.where(blk_mask, l_mat, 0.0)
    o = l_mat - d
    mm = lambda a, b: _dot(_mx(a), _mx(b))
    d2 = mm(d, d)
    d4 = mm(d2, d2)
    d8 = mm(d4, d4)
    t = eye_f + d
    t = t + mm(d2, t)
    t = t + mm(d4, t)
    t_d = t + mm(d8, t)
    m1 = mm(t_d, o)
    m2 = mm(m1, m1)
    w = m1 + m2 + mm(m1, m2)
    return t_d + mm(w, t_d)


def _rwkv_body(r_ref, k_ref, v_ref, pg_ref, m_ref, mur_ref, muk_ref, muv_ref, mupg_ref, mum_ref,
               w0_ref, w2_ref, a0_ref, a2_ref, g2_ref, kk_ref, ka_ref, rk_ref, lnw_ref, lnb_ref,
               ones_ref, o_ref, rbuf, kbuf, vbuf, pgbuf, mbuf, st_ref, ybuf):
    cl = RWKV_CHUNK
    c = pl.program_id(1)
    bufs = (rbuf, kbuf, vbuf, pgbuf, mbuf)
    srcs = (r_ref, k_ref, v_ref, pg_ref, m_ref)

    @pl.when(c == 0)
    def _():
        for b in bufs:
            b[0:SUBLANES, :] = jnp.zeros((SUBLANES, b.shape[1]), F32)
        st_ref[...] = jnp.zeros(st_ref.shape, F32)

    @pl.when(c > 0)
    def _():
        for b in bufs:
            b[0:SUBLANES, :] = b[cl:cl + SUBLANES, :]

    def shifted(buf, src, mu_ref):
        cur = src[...]
        buf[SUBLANES:SUBLANES + cl, :] = cur
        prev = buf[SUBLANES - 1:SUBLANES - 1 + cl, :]
        return cur + (prev - cur) * mu_ref[...]

    r = shifted(rbuf, r_ref, mur_ref)
    k = shifted(kbuf, k_ref, muk_ref)
    v = shifted(vbuf, v_ref, muv_ref)
    pg = shifted(pgbuf, pg_ref, mupg_ref)
    mm = shifted(mbuf, m_ref, mum_ref)

    ones_blk = ones_ref[...]
    n_tiles = r.shape[1] // LANES

    def head_sum(x):
        cols = [_dot_split_lhs(x[:, t * LANES:(t + 1) * LANES], ones_blk, 2) for t in range(n_tiles)]
        return jnp.concatenate(cols, axis=1)

    w_log = -_softplus(-(w0_ref[...] + _dot(_mx(jnp.tanh(mm)), w2_ref[...]))) - 0.5
    lw = -jnp.exp(w_log)
    iclr = _sigmoid(a0_ref[...] + _dot(_mx(mm), a2_ref[...]))
    gate = _dot(_mx(_sigmoid(pg)), g2_ref[...])
    kk = k * kk_ref[...]
    kk = kk / jnp.maximum(jnp.sqrt(head_sum(kk * kk)), 1e-12)
    k = k * (1.0 + (iclr - 1.0) * ka_ref[...])

    row_c = lax.broadcasted_iota(jnp.int32, (cl, cl), 0)
    col_c = lax.broadcasted_iota(jnp.int32, (cl, cl), 1)
    tril_c = jnp.where(row_c >= col_c, 1.0, 0.0).astype(MXU_DTYPE)
    cum = _dot_split_rhs(tril_c, lw, 3)
    p_in = jnp.exp(cum)
    p_inv = jnp.exp(-cum)
    a_t = -(kk * jnp.exp(cum - lw))
    b_t = kk * iclr * p_inv
    k_t = k * p_inv
    r_t = r * p_in
    p_end = p_in[cl - 1:cl, :]
    bp_t = b_t * p_end
    kp_t = k_t * p_end

    lane = lax.broadcasted_iota(jnp.int32, (1, LANES), 1)
    lane_lo = lane < HEAD
    row = lax.broadcasted_iota(jnp.int32, (LANES, LANES), 0)
    col = lax.broadcasted_iota(jnp.int32, (LANES, LANES), 1)
    same = (row >= HEAD) == (col >= HEAD)
    rs = row & (HEAD - 1)
    cs = col & (HEAD - 1)
    m_strict = same & (rs > cs)
    m_incl = same & (rs >= cs)
    m_blk = (row >> 4) == (col >> 4)
    eye = row == col
    eye_f = jnp.where(eye, 1.0, 0.0)

    def stack2(x):
        return jnp.concatenate([jnp.where(lane_lo, x, 0.0), jnp.where(lane_lo, 0.0, x)], axis=0)

    for p in range(n_tiles):
        sl = slice(p * LANES, (p + 1) * LANES)
        a2, r2, b2, k2 = stack2(a_t[:, sl]), stack2(r_t[:, sl]), stack2(b_t[:, sl]), stack2(k_t[:, sl])
        v2, bp2, kp2 = stack2(v[:, sl]), stack2(bp_t[:, sl]), stack2(kp_t[:, sl])
        gram = _dot_nt(_mx(jnp.concatenate([a2, r2], axis=0)), _mx(jnp.concatenate([b2, k2], axis=0)))
        a_ab = jnp.where(m_strict, gram[0:LANES, 0:LANES], 0.0)
        a_ak = jnp.where(m_strict, gram[0:LANES, LANES:], 0.0)
        a_rb = jnp.where(m_incl, gram[LANES:, 0:LANES], 0.0)
        a_rk = jnp.where(m_incl, gram[LANES:, LANES:], 0.0)
        t_inv = _unit_lower_inverse(a_ab, m_blk, eye_f)
        st = st_ref[p]
        y0 = _dot(_mx(jnp.concatenate([a2, a_ak], axis=1)), _mx(jnp.concatenate([st, v2], axis=0)))
        sa = _dot(_mx(t_inv), _mx(y0))
        rhs = _mx(jnp.concatenate([st, sa, v2], axis=0))
        o2 = _dot(_mx(jnp.concatenate([r2, a_rb, a_rk], axis=1)), rhs)
        ybuf[:, sl] = o2[0:cl, :] + o2[cl:, :]
        p_col = jnp.sum(jnp.where(eye, p_end[:, sl], 0.0), axis=1, keepdims=True)
        upd = _dot(_mx(jnp.concatenate([bp2, kp2], axis=0).T), _mx(jnp.concatenate([sa, v2], axis=0)))
        st_ref[p] = st * p_col + upd

    y = ybuf[...]
    inv_n = 1.0 / HEAD
    mean = head_sum(y) * inv_n
    d = y - mean
    var = head_sum(d * d) * inv_n
    y = d * lax.rsqrt(var + RWKV_LN_EPS) * lnw_ref[...] + lnb_ref[...]
    bonus = head_sum(r * k * rk_ref[...]) * v
    o_ref[...] = ((y + bonus) * gate).astype(o_ref.dtype)


def _rwkv(u, bsz, seq, mu, w0, w2, a0, a2, g2, k_k, k_a, r_k, ln_w, ln_b):
    cl = RWKV_CHUNK
    nc = seq // cl
    w = W_MIX
    mw = 2 * LANES
    row1 = lambda vec: vec.reshape(1, -1)
    o_pw = 3 * w
    o_pa = o_pw + RWKV_DECAY_RANK
    o_pg = o_pa + RWKV_AAA_RANK
    mu_m = jnp.zeros((mw,), F32)
    mu_m = mu_m.at[MISC_PW:MISC_PW + RWKV_DECAY_RANK].set(mu[o_pw:o_pa])
    mu_m = mu_m.at[MISC_PA:MISC_PA + RWKV_AAA_RANK].set(mu[o_pa:o_pg])
    w2e = jnp.zeros((mw, w), F32).at[MISC_PW:MISC_PW + RWKV_DECAY_RANK].set(w2)
    a2e = jnp.zeros((mw, w), F32).at[MISC_PA:MISC_PA + RWKV_AAA_RANK].set(a2)
    ones_blk = np.kron(np.eye(LANES // HEAD, dtype=np.float32), np.ones((HEAD, HEAD), np.float32))
    rowblk = lambda width, cb: pl.BlockSpec((cl, width), lambda b, c: (b * nc + c, cb))
    cvec = lambda width: pl.BlockSpec((1, width), lambda b, c: (0, 0))
    cmat = lambda rows, width: pl.BlockSpec((rows, width), lambda b, c: (0, 0))
    return pl.pallas_call(
        _rwkv_body,
        grid=(bsz, nc),
        in_specs=[rowblk(w, COL_R // w), rowblk(w, COL_K // w), rowblk(w, COL_V // w),
                  rowblk(mw, COL_PG // mw), rowblk(mw, COL_MISC // mw),
                  cvec(w), cvec(w), cvec(w), cvec(mw), cvec(mw),
                  cvec(w), cmat(mw, w), cvec(w), cmat(mw, w), cmat(RWKV_GATE_RANK, w),
                  cvec(w), cvec(w), cvec(w), cvec(w), cvec(w),
                  cmat(LANES, LANES)],
        out_specs=pl.BlockSpec((cl, w), lambda b, c: (b * nc + c, 0)),
        out_shape=jax.ShapeDtypeStruct((bsz * seq, w), MXU_DTYPE),
        scratch_shapes=[pltpu.VMEM((cl + SUBLANES, w), F32), pltpu.VMEM((cl + SUBLANES, w), F32),
                        pltpu.VMEM((cl + SUBLANES, w), F32), pltpu.VMEM((cl + SUBLANES, mw), F32),
                        pltpu.VMEM((cl + SUBLANES, mw), F32),
                        pltpu.VMEM((w // LANES, LANES, LANES), F32),
                        pltpu.VMEM((cl, w), F32)],
        compiler_params=_cparams(("arbitrary", "arbitrary")),
        name="rwkv7_mixer",
    )(u, u, u, u, u,
      row1(mu[0:w]), row1(mu[w:2 * w]), row1(mu[2 * w:3 * w]), row1(mu[o_pg:o_pg + RWKV_GATE_RANK]),
      row1(mu_m), row1(w0), _mx(w2e), row1(a0), _mx(a2e), _mx(g2), row1(k_k), row1(k_a),
      row1(r_k), row1(ln_w), row1(ln_b), jnp.asarray(ones_blk, MXU_DTYPE))


def _attn_body(q_ref, k_ref, v_ref, o_ref):
    d = q_ref.shape[1]
    hd = d // XATTN_HEADS
    scale = hd ** -0.5
    for h in range(XATTN_HEADS):
        hs = slice(h * hd, (h + 1) * hd)
        s = _dot_nt(q_ref[:, hs], k_ref[0, :, hs]) * scale
        s = s - jnp.max(s, axis=-1, keepdims=True)
        e = jnp.exp(s)
        p = e / jnp.sum(e, axis=-1, keepdims=True)
        o_ref[:, hs] = _dot(_mx(p), v_ref[0, :, hs]).astype(o_ref.dtype)


def _attention(q, k, v, bsz, seq, ts=512):
    d = q.shape[1]
    n_mem = k.shape[1]
    ts = min(ts, seq)
    ns = seq // ts
    return pl.pallas_call(
        _attn_body,
        grid=(bsz, ns),
        in_specs=[pl.BlockSpec((ts, d), lambda b, s: (b * ns + s, 0)),
                  pl.BlockSpec((1, n_mem, d), lambda b, s: (b, 0, 0)),
                  pl.BlockSpec((1, n_mem, d), lambda b, s: (b, 0, 0))],
        out_specs=pl.BlockSpec((ts, d), lambda b, s: (b * ns + s, 0)),
        out_shape=jax.ShapeDtypeStruct((bsz * seq, d), MXU_DTYPE),
        compiler_params=_cparams(("arbitrary", "arbitrary")),
        name="mem_attention",
    )(q, k, v)


def _in_proj_columns(d_in):
    w = W_MIX
    bcw = 2 * SSD_GROUPS * SSD_STATE
    heads = w // HEAD
    ssd_in = 2 * w + bcw + heads
    idx = np.full((N_U,), -1, np.int64)
    idx[COL_Z:COL_Z + w] = np.arange(0, w)
    idx[COL_XS:COL_XS + w] = np.arange(w, 2 * w)
    idx[COL_BC:COL_BC + bcw] = np.arange(2 * w, 2 * w + bcw)
    idx[COL_MISC + MISC_DT:COL_MISC + MISC_DT + heads] = np.arange(2 * w + bcw, ssd_in)
    rw = ssd_in
    idx[COL_R:COL_R + w] = rw + np.arange(0, w)
    idx[COL_K:COL_K + w] = rw + np.arange(w, 2 * w)
    idx[COL_V:COL_V + w] = rw + np.arange(2 * w, 3 * w)
    o = 3 * w
    idx[COL_MISC + MISC_PW:COL_MISC + MISC_PW + RWKV_DECAY_RANK] = rw + o + np.arange(RWKV_DECAY_RANK)
    o += RWKV_DECAY_RANK
    idx[COL_MISC + MISC_PA:COL_MISC + MISC_PA + RWKV_AAA_RANK] = rw + o + np.arange(RWKV_AAA_RANK)
    o += RWKV_AAA_RANK
    idx[COL_PG:COL_PG + RWKV_GATE_RANK] = rw + o + np.arange(RWKV_GATE_RANK)
    assert rw + o + RWKV_GATE_RANK == d_in
    return idx


def _permute_in_proj(w_in):
    idx = _in_proj_columns(w_in.shape[1])
    pieces, start = [], 0
    while start < N_U:
        stop = start + 1
        if idx[start] < 0:
            while stop < N_U and idx[stop] < 0:
                stop += 1
            pieces.append(jnp.zeros((w_in.shape[0], stop - start), w_in.dtype))
        else:
            while stop < N_U and idx[stop] == idx[stop - 1] + 1:
                stop += 1
            pieces.append(w_in[:, idx[start]:idx[stop - 1] + 1])
        start = stop
    return jnp.concatenate(pieces, axis=1)


def kernel(x, mem, norm_mix_g, w_in, ssd_conv_w, ssd_conv_b, ssd_dt_bias, ssd_a_log, ssd_d, ssd_norm_g, rwkv_mu, rwkv_w0, rwkv_w2, rwkv_a0, rwkv_a2, rwkv_g2, rwkv_k_k, rwkv_k_a, rwkv_r_k, rwkv_ln_w, rwkv_ln_b, w_out, norm_x_g, norm_mem_g, xattn_wq, xattn_wk, xattn_wv, xattn_wo, norm_ffn_g, ffn_w1, ffn_w2, final_norm_g):
    bsz, seq, d = x.shape
    n_mem = mem.shape[1]
    xr = x.reshape(bsz * seq, d)
    memr = mem.reshape(bsz * n_mem, d)
    for l in range(w_in.shape[0]):
        h = _rmsnorm(xr, norm_mix_g[l], MXU_DTYPE)
        u = _matmul([(h, _mx(_permute_in_proj(w_in[l])))], F32, name="in_proj")
        y_ssd = _ssd(u, bsz, seq, ssd_conv_w[l], ssd_conv_b[l], ssd_dt_bias[l], ssd_a_log[l],
                     ssd_d[l], ssd_norm_g[l])
        y_rwkv = _rwkv(u, bsz, seq, rwkv_mu[l], rwkv_w0[l], rwkv_w2[l], rwkv_a0[l], rwkv_a2[l],
                       rwkv_g2[l], rwkv_k_k[l], rwkv_k_a[l], rwkv_r_k[l].reshape(-1),
                       rwkv_ln_w[l], rwkv_ln_b[l])
        wo = _mx(w_out[l])
        xr = _matmul([(y_ssd, wo[:W_MIX]), (y_rwkv, wo[W_MIX:])], F32, res=xr, name="out_proj")

        h = _rmsnorm(xr, norm_x_g[l], MXU_DTYPE)
        m = _rmsnorm(memr, norm_mem_g[l], MXU_DTYPE)
        q = _matmul([(h, _mx(xattn_wq[l]))], MXU_DTYPE, name="q_proj")
        kx = _matmul([(m, _mx(xattn_wk[l]))], MXU_DTYPE, name="k_proj")
        vx = _matmul([(m, _mx(xattn_wv[l]))], MXU_DTYPE, name="v_proj")
        o = _attention(q, kx.reshape(bsz, n_mem, d), vx.reshape(bsz, n_mem, d), bsz, seq)
        xr = _matmul([(o, _mx(xattn_wo[l]))], F32, res=xr, name="o_proj")

        h = _rmsnorm(xr, norm_ffn_g[l], MXU_DTYPE)
        hid = _matmul([(h, _mx(ffn_w1[l]))], MXU_DTYPE, act="relu2", name="ffn_up")
        xr = _matmul([(hid, _mx(ffn_w2[l]))], F32, res=xr, name="ffn_down")
    return _rmsnorm(xr, final_norm_g, x.dtype).reshape(bsz, seq, d)
```

```python
import functools

import numpy as np
import jax
import jax.numpy as jnp
from jax import lax
from jax.experimental import pallas as pl
from jax.experimental.pallas import tpu as pltpu

F32 = jnp.float32
MXU_DTYPE = jnp.bfloat16

NORM_EPS = 1e-6
RWKV_LN_EPS = 64e-5

HEAD = 64
SSD_STATE = 128
SSD_GROUPS = 2
SSD_CHUNK = 128
SSD_CONV = 4
RWKV_CHUNK = 64
RWKV_DECAY_RANK = 96
RWKV_AAA_RANK = 96
RWKV_GATE_RANK = 256
XATTN_HEADS = 4

LANES = 128
SUBLANES = 8
VMEM_LIMIT = 56 * 1024 * 1024

W_MIX = 1024
COL_Z, COL_XS, COL_R, COL_K, COL_V = 0, 1024, 2048, 3072, 4096
COL_BC, COL_PG, COL_MISC = 5120, 5632, 5888
N_U = 6144
MISC_PW, MISC_DT, MISC_PA = 0, 96, 128


def _mx(a):
    return a.astype(MXU_DTYPE)


def _dot(a, b):
    return jnp.dot(a, b, preferred_element_type=F32)


def _dot_nt(a, b):
    return lax.dot_general(a, b, (((1,), (1,)), ((), ())), preferred_element_type=F32)


def _split(v, parts):
    out = []
    rem = v
    for _ in range(parts):
        p = rem.astype(MXU_DTYPE)
        out.append(p)
        rem = rem - p.astype(F32)
    return out


def _dot_split_rhs(a01, v, parts):
    acc = None
    for p in _split(v, parts):
        t = _dot(a01, p)
        acc = t if acc is None else acc + t
    return acc


def _dot_split_lhs(v, b01, parts):
    acc = None
    for p in _split(v, parts):
        t = _dot(p, b01)
        acc = t if acc is None else acc + t
    return acc


def _sigmoid(x):
    return 1.0 / (1.0 + jnp.exp(-x))


def _softplus(x):
    return jnp.maximum(x, 0.0) + jnp.log(1.0 + jnp.exp(-jnp.abs(x)))


def _cparams(sem):
    return pltpu.CompilerParams(dimension_semantics=sem, vmem_limit_bytes=VMEM_LIMIT)


def _rmsnorm_body(x_ref, g_ref, o_ref):
    x = x_ref[...]
    ms = jnp.mean(x * x, axis=-1, keepdims=True)
    o_ref[...] = (x * lax.rsqrt(ms + NORM_EPS) * g_ref[...]).astype(o_ref.dtype)


def _rmsnorm(x, g, out_dtype, tm=512):
    m, d = x.shape
    tm = min(tm, m)
    return pl.pallas_call(
        _rmsnorm_body,
        grid=(m // tm,),
        in_specs=[pl.BlockSpec((tm, d), lambda i: (i, 0)),
                  pl.BlockSpec((1, d), lambda i: (0, 0))],
        out_specs=pl.BlockSpec((tm, d), lambda i: (i, 0)),
        out_shape=jax.ShapeDtypeStruct((m, d), out_dtype),
        compiler_params=_cparams(("arbitrary",)),
        name="rmsnorm",
    )(x, g.reshape(1, d))


def _mm_body(*refs, n_pairs, nk, has_res, act):
    ab = refs[:2 * n_pairs]
    pos = 2 * n_pairs
    res_ref = refs[pos] if has_res else None
    pos += int(has_res)
    o_ref = refs[pos]
    acc_ref = refs[pos + 1] if nk > 1 else None

    part = None
    for p in range(n_pairs):
        t = _dot(ab[2 * p][...], ab[2 * p + 1][...])
        part = t if part is None else part + t

    def finish(r):
        if act == "relu2":
            r = jnp.square(jnp.maximum(r, 0.0))
        if has_res:
            r = res_ref[...] + r
        o_ref[...] = r.astype(o_ref.dtype)

    if nk == 1:
        finish(part)
        return

    k = pl.program_id(2)

    @pl.when(k == 0)
    def _():
        acc_ref[...] = part

    @pl.when(k > 0)
    def _():
        acc_ref[...] += part

    @pl.when(k == nk - 1)
    def _():
        finish(acc_ref[...])


def _matmul(pairs, out_dtype, res=None, act=None, tm=1024, tn=1024, tk=2048, name="matmul"):
    m, kdim = pairs[0][0].shape
    n = pairs[0][1].shape[1]
    tm, tn, tk = min(tm, m), min(tn, n), min(tk, kdim)
    nk = kdim // tk
    in_specs, args = [], []
    for a, b in pairs:
        in_specs += [pl.BlockSpec((tm, tk), lambda i, j, k: (i, k)),
                     pl.BlockSpec((tk, tn), lambda i, j, k: (k, j))]
        args += [a, b]
    if res is not None:
        in_specs.append(pl.BlockSpec((tm, tn), lambda i, j, k: (i, j)))
        args.append(res)
    scratch = [pltpu.VMEM((tm, tn), F32)] if nk > 1 else []
    return pl.pallas_call(
        functools.partial(_mm_body, n_pairs=len(pairs), nk=nk, has_res=res is not None, act=act),
        grid=(m // tm, n // tn, nk),
        in_specs=in_specs,
        out_specs=pl.BlockSpec((tm, tn), lambda i, j, k: (i, j)),
        out_shape=jax.ShapeDtypeStruct((m, n), out_dtype),
        scratch_shapes=scratch,
        compiler_params=_cparams(("arbitrary", "arbitrary", "arbitrary")),
        name=name,
    )(*args)


def _ssd_body(z_ref, xs_ref, bc_ref, m_ref, cwx_ref, cbx_ref, cwbc_ref, cbbc_ref, dtb_ref,
              alog_ref, dvec_ref, ng_ref, esel_ref, o_ref, xbuf, bcbuf, st_ref):
    q = SSD_CHUNK
    n = SSD_STATE
    gw = st_ref.shape[2]
    c = pl.program_id(1)

    @pl.when(c == 0)
    def _():
        xbuf[0:SUBLANES, :] = jnp.zeros((SUBLANES, xbuf.shape[1]), F32)
        bcbuf[0:SUBLANES, :] = jnp.zeros((SUBLANES, bcbuf.shape[1]), F32)
        st_ref[...] = jnp.zeros(st_ref.shape, F32)

    @pl.when(c > 0)
    def _():
        xbuf[0:SUBLANES, :] = xbuf[q:q + SUBLANES, :]
        bcbuf[0:SUBLANES, :] = bcbuf[q:q + SUBLANES, :]

    xbuf[SUBLANES:SUBLANES + q, :] = xs_ref[...]
    bcbuf[SUBLANES:SUBLANES + q, :] = bc_ref[...]

    def conv_silu(buf, w_ref, b_ref):
        acc = None
        for k in range(SSD_CONV):
            off = SUBLANES - (SSD_CONV - 1) + k
            t = buf[off:off + q, :] * w_ref[k:k + 1, :]
            acc = t if acc is None else acc + t
        acc = acc + b_ref[...]
        return acc * _sigmoid(acc)

    xc = conv_silu(xbuf, cwx_ref, cbx_ref)
    bcc = conv_silu(bcbuf, cwbc_ref, cbbc_ref)
    g_n = SSD_GROUPS * n

    lane = lax.broadcasted_iota(jnp.int32, (1, LANES), 1)
    n_heads = SSD_GROUPS * gw // HEAD
    dmask = (lane >= MISC_DT) & (lane < MISC_DT + n_heads)
    dt = jnp.where(dmask, _softplus(m_ref[...] + dtb_ref[...]), 0.0)
    a = jnp.where(dmask, -jnp.exp(alog_ref[...]), 0.0)
    da = dt * a
    row = lax.broadcasted_iota(jnp.int32, (q, q), 0)
    col = lax.broadcasted_iota(jnp.int32, (q, q), 1)
    tri = row >= col
    tril = jnp.where(tri, 1.0, 0.0).astype(MXU_DTYPE)
    cs = _dot_split_rhs(tril, da, 3)
    ecs = jnp.exp(cs)
    dte = jnp.exp(cs[q - 1:q, :] - cs)
    esel = esel_ref[...]
    dt_e = _dot_split_lhs(dt, esel, 2)
    ecs_e = _dot_split_lhs(ecs, esel, 2)
    dte_e = _dot_split_lhs(dte, esel, 2)
    cs_t = cs.T

    xdt = xc * dt_e
    xdt_m = _mx(xdt)
    xd_m = _mx(xdt * dte_e)
    lane_lo = lane < HEAD

    y_cols = []
    for g in range(SSD_GROUPS):
        bg = bcc[:, g * n:(g + 1) * n]
        cg = _mx(bcc[:, g_n + g * n:g_n + (g + 1) * n])
        cb = _dot_nt(cg, _mx(bg))
        gs = slice(g * gw, (g + 1) * gw)
        st = st_ref[g]
        y_off = _dot(cg, _mx(st)) * ecs_e[:, gs]
        st_ref[g] = st * ecs_e[q - 1:q, gs] + _dot(_mx(bg.T), xd_m[:, gs])
        for pr in range(gw // LANES):
            h0 = (g * gw + pr * LANES) // HEAD
            ps = slice(g * gw + pr * LANES, g * gw + (pr + 1) * LANES)
            res = []
            for hh in (h0, h0 + 1):
                li = MISC_DT + hh
                seg = cs[:, li:li + 1] - cs_t[li:li + 1, :]
                lm = jnp.where(tri, jnp.exp(jnp.where(tri, seg, 0.0)), 0.0)
                res.append(_dot(_mx(cb * lm), xdt_m[:, ps]))
            y_diag = jnp.where(lane_lo, res[0], res[1])
            y_cols.append(y_diag + y_off[:, pr * LANES:(pr + 1) * LANES])
    y = jnp.concatenate(y_cols, axis=1) + xc * dvec_ref[...]
    zz = z_ref[...]
    y = y * (zz * _sigmoid(zz))
    outs = []
    for g in range(SSD_GROUPS):
        yg = y[:, g * gw:(g + 1) * gw]
        ms = jnp.mean(yg * yg, axis=-1, keepdims=True)
        outs.append(yg * lax.rsqrt(ms + NORM_EPS))
    y = jnp.concatenate(outs, axis=1) * ng_ref[...]
    o_ref[...] = y.astype(o_ref.dtype)


def _ssd(u, bsz, seq, conv_w, conv_b, dt_bias, a_log, d_skip, norm_g):
    q = SSD_CHUNK
    nc = seq // q
    w = W_MIX
    gw = w // SSD_GROUPS
    heads = w // HEAD
    bcw = 2 * SSD_GROUPS * SSD_STATE
    cw = conv_w[:, 0, :]
    pad = lambda v: jnp.zeros((1, LANES), F32).at[0, MISC_DT:MISC_DT + heads].set(v)
    esel = np.zeros((LANES, w), np.float32)
    for h in range(heads):
        esel[MISC_DT + h, h * HEAD:(h + 1) * HEAD] = 1.0
    rowblk = lambda cb: (lambda b, c: (b * nc + c, cb))
    const = lambda b, c: (0, 0)
    return pl.pallas_call(
        _ssd_body,
        grid=(bsz, nc),
        in_specs=[pl.BlockSpec((q, w), rowblk(COL_Z // w)),
                  pl.BlockSpec((q, w), rowblk(COL_XS // w)),
                  pl.BlockSpec((q, bcw), rowblk(COL_BC // bcw)),
                  pl.BlockSpec((q, LANES), rowblk(COL_MISC // LANES)),
                  pl.BlockSpec((SSD_CONV, w), const), pl.BlockSpec((1, w), const),
                  pl.BlockSpec((SSD_CONV, bcw), const), pl.BlockSpec((1, bcw), const),
                  pl.BlockSpec((1, LANES), const), pl.BlockSpec((1, LANES), const),
                  pl.BlockSpec((1, w), const), pl.BlockSpec((1, w), const),
                  pl.BlockSpec((LANES, w), const)],
        out_specs=pl.BlockSpec((q, w), lambda b, c: (b * nc + c, 0)),
        out_shape=jax.ShapeDtypeStruct((bsz * seq, w), MXU_DTYPE),
        scratch_shapes=[pltpu.VMEM((q + SUBLANES, w), F32),
                        pltpu.VMEM((q + SUBLANES, bcw), F32),
                        pltpu.VMEM((SSD_GROUPS, SSD_STATE, gw), F32)],
        compiler_params=_cparams(("arbitrary", "arbitrary")),
        name="ssd_mixer",
    )(u, u, u, u, cw[:, :w], conv_b[:w].reshape(1, w), cw[:, w:], conv_b[w:].reshape(1, bcw),
      pad(dt_bias), pad(a_log), jnp.repeat(d_skip, HEAD).reshape(1, w), norm_g.reshape(1, w),
      jnp.asarray(esel, MXU_DTYPE))


def _unit_lower_inverse(l_mats, blk_mask, eye_f):
    mm = lambda xs, ys: [_dot(x, y) for x, y in zip(xs, ys)]
    mx = lambda xs: [_mx(x) for x in xs]
    add = lambda xs, ys: [x + y for x, y in zip(xs, ys)]
    d = [jnp.where(blk_mask, l, 0.0) for l in l_mats]
    o = mx([l - x for l, x in zip(l_mats, d)])
    d1 = mx(d)
    d2 = mx(mm(d1, d1))
    d4 = mx(mm(d2, d2))
    d8 = mx(mm(d4, d4))
    t = [eye_f + x for x in d]
    t = add(t, mm(d2, mx(t)))
    t = add(t, mm(d4, mx(t)))
    t_d = add(t, mm(d8, mx(t)))
    t_dm = mx(t_d)
    m1 = mm(t_dm, o)
    m1m = mx(m1)
    m2 = mm(m1m, m1m)
    m3 = mm(m1m, mx(m2))
    w = mx([x + y + z for x, y, z in zip(m1, m2, m3)])
    return add(t_d, mm(w, t_dm))


def _rwkv_body(r_ref, k_ref, v_ref, pg_ref, m_ref, mur_ref, muk_ref, muv_ref, mupg_ref, mum_ref,
               w0_ref, w2_ref, a0_ref, a2_ref, g2_ref, kk_ref, ka_ref, rk_ref, lnw_ref, lnb_ref,
               ones_ref, o_ref, rbuf, kbuf, vbuf, pgbuf, mbuf, st_ref, ybuf):
    cl = RWKV_CHUNK
    c = pl.program_id(1)
    bufs = (rbuf, kbuf, vbuf, pgbuf, mbuf)
    srcs = (r_ref, k_ref, v_ref, pg_ref, m_ref)

    @pl.when(c == 0)
    def _():
        for b in bufs:
            b[0:SUBLANES, :] = jnp.zeros((SUBLANES, b.shape[1]), F32)
        st_ref[...] = jnp.zeros(st_ref.shape, F32)

    @pl.when(c > 0)
    def _():
        for b in bufs:
            b[0:SUBLANES, :] = b[cl:cl + SUBLANES, :]

    def shifted(buf, src, mu_ref):
        cur = src[...]
        buf[SUBLANES:SUBLANES + cl, :] = cur
        prev = buf[SUBLANES - 1:SUBLANES - 1 + cl, :]
        return cur + (prev - cur) * mu_ref[...]

    r = shifted(rbuf, r_ref, mur_ref)
    k = shifted(kbuf, k_ref, muk_ref)
    v = shifted(vbuf, v_ref, muv_ref)
    pg = shifted(pgbuf, pg_ref, mupg_ref)
    misc = shifted(mbuf, m_ref, mum_ref)

    ones_blk = ones_ref[...]
    n_tiles = r.shape[1] // LANES

    def head_sum(x):
        cols = [_dot_split_lhs(x[:, t * LANES:(t + 1) * LANES], ones_blk, 2) for t in range(n_tiles)]
        return jnp.concatenate(cols, axis=1)

    w_log = -_softplus(-(w0_ref[...] + _dot(_mx(jnp.tanh(misc)), w2_ref[...]))) - 0.5
    lw = -jnp.exp(w_log)
    iclr = _sigmoid(a0_ref[...] + _dot(_mx(misc), a2_ref[...]))
    gate = _dot(_mx(_sigmoid(pg)), g2_ref[...])
    kk = k * kk_ref[...]
    kk = kk / jnp.maximum(jnp.sqrt(head_sum(kk * kk)), 1e-12)
    k = k * (1.0 + (iclr - 1.0) * ka_ref[...])

    row_c = lax.broadcasted_iota(jnp.int32, (cl, cl), 0)
    col_c = lax.broadcasted_iota(jnp.int32, (cl, cl), 1)
    tril_c = jnp.where(row_c >= col_c, 1.0, 0.0).astype(MXU_DTYPE)
    cum = _dot_split_rhs(tril_c, lw, 3)
    p_in = jnp.exp(cum)
    p_inv = jnp.exp(-cum)
    a_t = -(kk * jnp.exp(cum - lw))
    b_t = kk * iclr * p_inv
    k_t = k * p_inv
    r_t = r * p_in
    p_end = p_in[cl - 1:cl, :]
    bp_t = b_t * p_end
    kp_t = k_t * p_end

    lane = lax.broadcasted_iota(jnp.int32, (1, LANES), 1)
    lane_lo = lane < HEAD
    row = lax.broadcasted_iota(jnp.int32, (LANES, LANES), 0)
    col = lax.broadcasted_iota(jnp.int32, (LANES, LANES), 1)
    same = (row >= HEAD) == (col >= HEAD)
    rs = row & (HEAD - 1)
    cs = col & (HEAD - 1)
    m_strict = same & (rs > cs)
    m_incl = same & (rs >= cs)
    m_blk = (row >> 4) == (col >> 4)
    eye = row == col
    eye_f = jnp.where(eye, 1.0, 0.0)

    def stack2(x):
        return jnp.concatenate([jnp.where(lane_lo, x, 0.0), jnp.where(lane_lo, 0.0, x)], axis=0)

    tiles = range(n_tiles)
    tile = lambda x, p: x[:, p * LANES:(p + 1) * LANES]
    stacked = lambda x: [_mx(stack2(tile(x, p))) for p in tiles]
    cat0 = lambda *xs: jnp.concatenate(xs, axis=0)
    cat1 = lambda *xs: jnp.concatenate(xs, axis=1)
    a2, r2, b2, k2, v2 = stacked(a_t), stacked(r_t), stacked(b_t), stacked(k_t), stacked(v)
    bkp_t = [_mx(cat0(stack2(tile(bp_t, p)), stack2(tile(kp_t, p))).T) for p in tiles]
    gram = [_dot_nt(cat0(a2[p], r2[p]), cat0(b2[p], k2[p])) for p in tiles]
    a_ab = [jnp.where(m_strict, g[0:LANES, 0:LANES], 0.0) for g in gram]
    a_ak = [_mx(jnp.where(m_strict, g[0:LANES, LANES:], 0.0)) for g in gram]
    a_rb = [_mx(jnp.where(m_incl, g[LANES:, 0:LANES], 0.0)) for g in gram]
    a_rk = [_mx(jnp.where(m_incl, g[LANES:, LANES:], 0.0)) for g in gram]
    t_inv = _unit_lower_inverse(a_ab, m_blk, eye_f)
    st = [st_ref[p] for p in tiles]
    st_m = [_mx(s) for s in st]
    y0 = [_dot(cat1(a2[p], a_ak[p]), cat0(st_m[p], v2[p])) for p in tiles]
    sa = [_mx(_dot(_mx(t_inv[p]), _mx(y0[p]))) for p in tiles]
    o2 = [_dot(cat1(r2[p], a_rb[p], a_rk[p]), cat0(st_m[p], sa[p], v2[p])) for p in tiles]
    upd = [_dot(bkp_t[p], cat0(sa[p], v2[p])) for p in tiles]
    for p in tiles:
        ybuf[:, p * LANES:(p + 1) * LANES] = o2[p][0:cl, :] + o2[p][cl:, :]
        p_col = jnp.sum(jnp.where(eye, tile(p_end, p), 0.0), axis=1, keepdims=True)
        st_ref[p] = st[p] * p_col + upd[p]

    y = ybuf[...]
    inv_n = 1.0 / HEAD
    mean = head_sum(y) * inv_n
    d = y - mean
    var = head_sum(d * d) * inv_n
    y = d * lax.rsqrt(var + RWKV_LN_EPS) * lnw_ref[...] + lnb_ref[...]
    bonus = head_sum(r * k * rk_ref[...]) * v
    o_ref[...] = ((y + bonus) * gate).astype(o_ref.dtype)


def _rwkv(u, bsz, seq, mu, w0, w2, a0, a2, g2, k_k, k_a, r_k, ln_w, ln_b):
    cl = RWKV_CHUNK
    nc = seq // cl
    w = W_MIX
    mw = 2 * LANES
    row1 = lambda vec: vec.reshape(1, -1)
    o_pw = 3 * w
    o_pa = o_pw + RWKV_DECAY_RANK
    o_pg = o_pa + RWKV_AAA_RANK
    mu_m = jnp.zeros((mw,), F32)
    mu_m = mu_m.at[MISC_PW:MISC_PW + RWKV_DECAY_RANK].set(mu[o_pw:o_pa])
    mu_m = mu_m.at[MISC_PA:MISC_PA + RWKV_AAA_RANK].set(mu[o_pa:o_pg])
    w2e = jnp.zeros((mw, w), F32).at[MISC_PW:MISC_PW + RWKV_DECAY_RANK].set(w2)
    a2e = jnp.zeros((mw, w), F32).at[MISC_PA:MISC_PA + RWKV_AAA_RANK].set(a2)
    ones_blk = np.kron(np.eye(LANES // HEAD, dtype=np.float32), np.ones((HEAD, HEAD), np.float32))
    rowblk = lambda width, cb: pl.BlockSpec((cl, width), lambda b, c: (b * nc + c, cb))
    cvec = lambda width: pl.BlockSpec((1, width), lambda b, c: (0, 0))
    cmat = lambda rows, width: pl.BlockSpec((rows, width), lambda b, c: (0, 0))
    return pl.pallas_call(
        _rwkv_body,
        grid=(bsz, nc),
        in_specs=[rowblk(w, COL_R // w), rowblk(w, COL_K // w), rowblk(w, COL_V // w),
                  rowblk(mw, COL_PG // mw), rowblk(mw, COL_MISC // mw),
                  cvec(w), cvec(w), cvec(w), cvec(mw), cvec(mw),
                  cvec(w), cmat(mw, w), cvec(w), cmat(mw, w), cmat(RWKV_GATE_RANK, w),
                  cvec(w), cvec(w), cvec(w), cvec(w), cvec(w),
                  cmat(LANES, LANES)],
        out_specs=pl.BlockSpec((cl, w), lambda b, c: (b * nc + c, 0)),
        out_shape=jax.ShapeDtypeStruct((bsz * seq, w), MXU_DTYPE),
        scratch_shapes=[pltpu.VMEM((cl + SUBLANES, w), F32), pltpu.VMEM((cl + SUBLANES, w), F32),
                        pltpu.VMEM((cl + SUBLANES, w), F32), pltpu.VMEM((cl + SUBLANES, mw), F32),
                        pltpu.VMEM((cl + SUBLANES, mw), F32),
                        pltpu.VMEM((w // LANES, LANES, LANES), F32),
                        pltpu.VMEM((cl, w), F32)],
        compiler_params=_cparams(("arbitrary", "arbitrary")),
        name="rwkv7_mixer",
    )(u, u, u, u, u,
      row1(mu[0:w]), row1(mu[w:2 * w]), row1(mu[2 * w:3 * w]), row1(mu[o_pg:o_pg + RWKV_GATE_RANK]),
      row1(mu_m), row1(w0), _mx(w2e), row1(a0), _mx(a2e), _mx(g2), row1(k_k), row1(k_a),
      row1(r_k), row1(ln_w), row1(ln_b), jnp.asarray(ones_blk, MXU_DTYPE))


def _attn_body(q_ref, k_ref, v_ref, o_ref):
    d = q_ref.shape[1]
    hd = d // XATTN_HEADS
    scale = hd ** -0.5
    for h in range(XATTN_HEADS):
        hs = slice(h * hd, (h + 1) * hd)
        s = _dot_nt(q_ref[:, hs], k_ref[0, :, hs]) * scale
        s = s - jnp.max(s, axis=-1, keepdims=True)
        e = jnp.exp(s)
        p = e / jnp.sum(e, axis=-1, keepdims=True)
        o_ref[:, hs] = _dot(_mx(p), v_ref[0, :, hs]).astype(o_ref.dtype)


def _attention(q, k, v, bsz, seq, ts=512):
    d = q.shape[1]
    n_mem = k.shape[1]
    ts = min(ts, seq)
    ns = seq // ts
    return pl.pallas_call(
        _attn_body,
        grid=(bsz, ns),
        in_specs=[pl.BlockSpec((ts, d), lambda b, s: (b * ns + s, 0)),
                  pl.BlockSpec((1, n_mem, d), lambda b, s: (b, 0, 0)),
                  pl.BlockSpec((1, n_mem, d), lambda b, s: (b, 0, 0))],
        out_specs=pl.BlockSpec((ts, d), lambda b, s: (b * ns + s, 0)),
        out_shape=jax.ShapeDtypeStruct((bsz * seq, d), MXU_DTYPE),
        compiler_params=_cparams(("arbitrary", "arbitrary")),
        name="mem_attention",
    )(q, k, v)


def _in_proj_columns(d_in):
    w = W_MIX
    bcw = 2 * SSD_GROUPS * SSD_STATE
    heads = w // HEAD
    ssd_in = 2 * w + bcw + heads
    idx = np.full((N_U,), -1, np.int64)
    idx[COL_Z:COL_Z + w] = np.arange(0, w)
    idx[COL_XS:COL_XS + w] = np.arange(w, 2 * w)
    idx[COL_BC:COL_BC + bcw] = np.arange(2 * w, 2 * w + bcw)
    idx[COL_MISC + MISC_DT:COL_MISC + MISC_DT + heads] = np.arange(2 * w + bcw, ssd_in)
    rw = ssd_in
    idx[COL_R:COL_R + w] = rw + np.arange(0, w)
    idx[COL_K:COL_K + w] = rw + np.arange(w, 2 * w)
    idx[COL_V:COL_V + w] = rw + np.arange(2 * w, 3 * w)
    o = 3 * w
    idx[COL_MISC + MISC_PW:COL_MISC + MISC_PW + RWKV_DECAY_RANK] = rw + o + np.arange(RWKV_DECAY_RANK)
    o += RWKV_DECAY_RANK
    idx[COL_MISC + MISC_PA:COL_MISC + MISC_PA + RWKV_AAA_RANK] = rw + o + np.arange(RWKV_AAA_RANK)
    o += RWKV_AAA_RANK
    idx[COL_PG:COL_PG + RWKV_GATE_RANK] = rw + o + np.arange(RWKV_GATE_RANK)
    assert rw + o + RWKV_GATE_RANK == d_in
    return idx


def _permute_in_proj(w_in):
    idx = _in_proj_columns(w_in.shape[1])
    pieces, start = [], 0
    while start < N_U:
        stop = start + 1
        if idx[start] < 0:
            while stop < N_U and idx[stop] < 0:
                stop += 1
            pieces.append(jnp.zeros((w_in.shape[0], stop - start), w_in.dtype))
        else:
            while stop < N_U and idx[stop] == idx[stop - 1] + 1:
                stop += 1
            pieces.append(w_in[:, idx[start]:idx[stop - 1] + 1])
        start = stop
    return jnp.concatenate(pieces, axis=1)


def kernel(x, mem, norm_mix_g, w_in, ssd_conv_w, ssd_conv_b, ssd_dt_bias, ssd_a_log, ssd_d, ssd_norm_g, rwkv_mu, rwkv_w0, rwkv_w2, rwkv_a0, rwkv_a2, rwkv_g2, rwkv_k_k, rwkv_k_a, rwkv_r_k, rwkv_ln_w, rwkv_ln_b, w_out, norm_x_g, norm_mem_g, xattn_wq, xattn_wk, xattn_wv, xattn_wo, norm_ffn_g, ffn_w1, ffn_w2, final_norm_g):
    bsz, seq, d = x.shape
    n_mem = mem.shape[1]
    xr = x.reshape(bsz * seq, d)
    memr = mem.reshape(bsz * n_mem, d)
    for l in range(w_in.shape[0]):
        h = _rmsnorm(xr, norm_mix_g[l], MXU_DTYPE)
        u = _matmul([(h, _mx(_permute_in_proj(w_in[l])))], F32, name="in_proj")
        y_ssd = _ssd(u, bsz, seq, ssd_conv_w[l], ssd_conv_b[l], ssd_dt_bias[l], ssd_a_log[l],
                     ssd_d[l], ssd_norm_g[l])
        y_rwkv = _rwkv(u, bsz, seq, rwkv_mu[l], rwkv_w0[l], rwkv_w2[l], rwkv_a0[l], rwkv_a2[l],
                       rwkv_g2[l], rwkv_k_k[l], rwkv_k_a[l], rwkv_r_k[l].reshape(-1),
                       rwkv_ln_w[l], rwkv_ln_b[l])
        wo = _mx(w_out[l])
        xr = _matmul([(y_ssd, wo[:W_MIX]), (y_rwkv, wo[W_MIX:])], F32, res=xr, name="out_proj")

        h = _rmsnorm(xr, norm_x_g[l], MXU_DTYPE)
        m = _rmsnorm(memr, norm_mem_g[l], MXU_DTYPE)
        q = _matmul([(h, _mx(xattn_wq[l]))], MXU_DTYPE, name="q_proj")
        kx = _matmul([(m, _mx(xattn_wk[l]))], MXU_DTYPE, name="k_proj")
        vx = _matmul([(m, _mx(xattn_wv[l]))], MXU_DTYPE, name="v_proj")
        o = _attention(q, kx.reshape(bsz, n_mem, d), vx.reshape(bsz, n_mem, d), bsz, seq)
        xr = _matmul([(o, _mx(xattn_wo[l]))], F32, res=xr, name="o_proj")

        h = _rmsnorm(xr, norm_ffn_g[l], MXU_DTYPE)
        hid = _matmul([(h, _mx(ffn_w1[l]))], MXU_DTYPE, act="relu2", name="ffn_up")
        xr = _matmul([(hid, _mx(ffn_w2[l]))], F32, res=xr, name="ffn_down")
    return _rmsnorm(xr, final_norm_g, x.dtype).reshape(bsz, seq, d)
```

```python
import functools

import numpy as np
import jax
import jax.numpy as jnp
from jax import lax
from jax.experimental import pallas as pl
from jax.experimental.pallas import tpu as pltpu

F32 = jnp.float32
MXU_DTYPE = jnp.bfloat16

NORM_EPS = 1e-6
RWKV_LN_EPS = 64e-5

HEAD = 64
SSD_STATE = 128
SSD_GROUPS = 2
SSD_CHUNK = 128
SSD_CONV = 4
RWKV_CHUNK = 64
RWKV_DECAY_RANK = 96
RWKV_AAA_RANK = 96
RWKV_GATE_RANK = 256
XATTN_HEADS = 4

LANES = 128
SUBLANES = 8
VMEM_LIMIT = 56 * 1024 * 1024

W_MIX = 1024
COL_Z, COL_XS, COL_R, COL_K, COL_V = 0, 1024, 2048, 3072, 4096
COL_BC, COL_PG, COL_MISC = 5120, 5632, 5888
N_U = 6144
MISC_PW, MISC_DT, MISC_PA = 0, 96, 128


def _mx(a):
    return a.astype(MXU_DTYPE)


def _dot(a, b):
    return jnp.dot(a, b, preferred_element_type=F32)


def _dot_nt(a, b):
    return lax.dot_general(a, b, (((1,), (1,)), ((), ())), preferred_element_type=F32)


def _split(v, parts):
    out = []
    rem = v
    for _ in range(parts):
        p = rem.astype(MXU_DTYPE)
        out.append(p)
        rem = rem - p.astype(F32)
    return out


def _dot_split_rhs(a01, v, parts):
    acc = None
    for p in _split(v, parts):
        t = _dot(a01, p)
        acc = t if acc is None else acc + t
    return acc


def _dot_split_lhs(v, b01, parts):
    acc = None
    for p in _split(v, parts):
        t = _dot(p, b01)
        acc = t if acc is None else acc + t
    return acc


def _sigmoid(x):
    return 1.0 / (1.0 + jnp.exp(-x))


def _softplus(x):
    return jnp.maximum(x, 0.0) + jnp.log(1.0 + jnp.exp(-jnp.abs(x)))


def _cparams(sem):
    return pltpu.CompilerParams(dimension_semantics=sem, vmem_limit_bytes=VMEM_LIMIT)


def _rmsnorm_body(x_ref, g_ref, o_ref):
    x = x_ref[...]
    ms = jnp.mean(x * x, axis=-1, keepdims=True)
    o_ref[...] = (x * lax.rsqrt(ms + NORM_EPS) * g_ref[...]).astype(o_ref.dtype)


def _rmsnorm(x, g, out_dtype, tm=512):
    m, d = x.shape
    tm = min(tm, m)
    return pl.pallas_call(
        _rmsnorm_body,
        grid=(m // tm,),
        in_specs=[pl.BlockSpec((tm, d), lambda i: (i, 0)),
                  pl.BlockSpec((1, d), lambda i: (0, 0))],
        out_specs=pl.BlockSpec((tm, d), lambda i: (i, 0)),
        out_shape=jax.ShapeDtypeStruct((m, d), out_dtype),
        compiler_params=_cparams(("arbitrary",)),
        name="rmsnorm",
    )(x, g.reshape(1, d))


def _mm_body(*refs, n_pairs, nk, has_res, act):
    ab = refs[:2 * n_pairs]
    pos = 2 * n_pairs
    res_ref = refs[pos] if has_res else None
    pos += int(has_res)
    o_ref = refs[pos]
    acc_ref = refs[pos + 1] if nk > 1 else None

    part = None
    for p in range(n_pairs):
        t = _dot(ab[2 * p][...], ab[2 * p + 1][...])
        part = t if part is None else part + t

    def finish(r):
        if act == "relu2":
            r = jnp.square(jnp.maximum(r, 0.0))
        if has_res:
            r = res_ref[...] + r
        o_ref[...] = r.astype(o_ref.dtype)

    if nk == 1:
        finish(part)
        return

    k = pl.program_id(2)

    @pl.when(k == 0)
    def _():
        acc_ref[...] = part

    @pl.when(k > 0)
    def _():
        acc_ref[...] += part

    @pl.when(k == nk - 1)
    def _():
        finish(acc_ref[...])


def _matmul(pairs, out_dtype, res=None, act=None, tm=1024, tn=1024, tk=2048, name="matmul"):
    m, kdim = pairs[0][0].shape
    n = pairs[0][1].shape[1]
    tm, tn, tk = min(tm, m), min(tn, n), min(tk, kdim)
    nk = kdim // tk
    in_specs, args = [], []
    for a, b in pairs:
        in_specs += [pl.BlockSpec((tm, tk), lambda i, j, k: (i, k)),
                     pl.BlockSpec((tk, tn), lambda i, j, k: (k, j))]
        args += [a, b]
    if res is not None:
        in_specs.append(pl.BlockSpec((tm, tn), lambda i, j, k: (i, j)))
        args.append(res)
    scratch = [pltpu.VMEM((tm, tn), F32)] if nk > 1 else []
    return pl.pallas_call(
        functools.partial(_mm_body, n_pairs=len(pairs), nk=nk, has_res=res is not None, act=act),
        grid=(m // tm, n // tn, nk),
        in_specs=in_specs,
        out_specs=pl.BlockSpec((tm, tn), lambda i, j, k: (i, j)),
        out_shape=jax.ShapeDtypeStruct((m, n), out_dtype),
        scratch_shapes=scratch,
        compiler_params=_cparams(("arbitrary", "arbitrary", "arbitrary")),
        name=name,
    )(*args)


def _mm_resident_body(a_ref, b_ref, o_ref, bm_ref):
    @pl.when(pl.program_id(1) == 0)
    def _():
        bm_ref[...] = b_ref[...].astype(MXU_DTYPE)

    o_ref[...] = _dot(a_ref[...], bm_ref[...]).astype(o_ref.dtype)


def _matmul_resident(a, b, out_dtype, tm=1024, tn=1024, name="matmul_resident"):
    m, kdim = a.shape
    n = b.shape[1]
    tm, tn = min(tm, m), min(tn, n)
    return pl.pallas_call(
        _mm_resident_body,
        grid=(n // tn, m // tm),
        in_specs=[pl.BlockSpec((tm, kdim), lambda j, i: (i, 0)),
                  pl.BlockSpec((kdim, tn), lambda j, i: (0, j))],
        out_specs=pl.BlockSpec((tm, tn), lambda j, i: (i, j)),
        out_shape=jax.ShapeDtypeStruct((m, n), out_dtype),
        scratch_shapes=[pltpu.VMEM((kdim, tn), MXU_DTYPE)],
        compiler_params=_cparams(("arbitrary", "arbitrary")),
        name=name,
    )(a, b)


def _mm_rows_body(*refs, n_pairs):
    ab = refs[:2 * n_pairs]
    res_ref, g_ref, x_ref, h_ref = refs[2 * n_pairs:]
    acc = None
    for p in range(n_pairs):
        t = _dot(ab[2 * p][...], ab[2 * p + 1][...])
        acc = t if acc is None else acc + t
    x = res_ref[...] + acc
    x_ref[...] = x
    ms = jnp.mean(x * x, axis=-1, keepdims=True)
    h_ref[...] = (x * lax.rsqrt(ms + NORM_EPS) * g_ref[...]).astype(h_ref.dtype)


def _matmul_rows(pairs, res, g, tm=512, name="matmul_rows"):
    m, n = res.shape
    tm = min(tm, m)
    in_specs, args = [], []
    for a, b in pairs:
        in_specs += [pl.BlockSpec((tm, a.shape[1]), lambda i: (i, 0)),
                     pl.BlockSpec(b.shape, lambda i: (0, 0))]
        args += [a, b]
    in_specs += [pl.BlockSpec((tm, n), lambda i: (i, 0)), pl.BlockSpec((1, n), lambda i: (0, 0))]
    return pl.pallas_call(
        functools.partial(_mm_rows_body, n_pairs=len(pairs)),
        grid=(m // tm,),
        in_specs=in_specs,
        out_specs=[pl.BlockSpec((tm, n), lambda i: (i, 0)), pl.BlockSpec((tm, n), lambda i: (i, 0))],
        out_shape=[jax.ShapeDtypeStruct((m, n), F32), jax.ShapeDtypeStruct((m, n), MXU_DTYPE)],
        compiler_params=_cparams(("arbitrary",)),
        name=name,
    )(*args, res, g.reshape(1, n))


def _ssd_body(z_ref, xs_ref, bc_ref, m_ref, cwx_ref, cbx_ref, cwbc_ref, cbbc_ref, dtb_ref,
              alog_ref, dvec_ref, ng_ref, esel_ref, o_ref, xbuf, bcbuf, st_ref):
    q = SSD_CHUNK
    n = SSD_STATE
    gw = st_ref.shape[2]
    c = pl.program_id(1)

    @pl.when(c == 0)
    def _():
        xbuf[0:SUBLANES, :] = jnp.zeros((SUBLANES, xbuf.shape[1]), F32)
        bcbuf[0:SUBLANES, :] = jnp.zeros((SUBLANES, bcbuf.shape[1]), F32)
        st_ref[...] = jnp.zeros(st_ref.shape, F32)

    @pl.when(c > 0)
    def _():
        xbuf[0:SUBLANES, :] = xbuf[q:q + SUBLANES, :]
        bcbuf[0:SUBLANES, :] = bcbuf[q:q + SUBLANES, :]

    xbuf[SUBLANES:SUBLANES + q, :] = xs_ref[...]
    bcbuf[SUBLANES:SUBLANES + q, :] = bc_ref[...]

    def conv_silu(buf, w_ref, b_ref):
        acc = None
        for k in range(SSD_CONV):
            off = SUBLANES - (SSD_CONV - 1) + k
            t = buf[off:off + q, :] * w_ref[k:k + 1, :]
            acc = t if acc is None else acc + t
        acc = acc + b_ref[...]
        return acc * _sigmoid(acc)

    xc = conv_silu(xbuf, cwx_ref, cbx_ref)
    bcc = conv_silu(bcbuf, cwbc_ref, cbbc_ref)
    g_n = SSD_GROUPS * n

    lane = lax.broadcasted_iota(jnp.int32, (1, LANES), 1)
    n_heads = SSD_GROUPS * gw // HEAD
    dmask = (lane >= MISC_DT) & (lane < MISC_DT + n_heads)
    dt = jnp.where(dmask, _softplus(m_ref[...] + dtb_ref[...]), 0.0)
    a = jnp.where(dmask, -jnp.exp(alog_ref[...]), 0.0)
    da = dt * a
    row = lax.broadcasted_iota(jnp.int32, (q, q), 0)
    col = lax.broadcasted_iota(jnp.int32, (q, q), 1)
    tri = row >= col
    tril = jnp.where(tri, 1.0, 0.0).astype(MXU_DTYPE)
    cs = _dot_split_rhs(tril, da, 3)
    ecs = jnp.exp(cs)
    dte = jnp.exp(cs[q - 1:q, :] - cs)
    esel = esel_ref[...]
    dt_e = _dot_split_lhs(dt, esel, 2)
    ecs_e = _dot_split_lhs(ecs, esel, 2)
    dte_e = _dot_split_lhs(dte, esel, 2)
    cs_t = cs.T

    xdt = xc * dt_e
    xdt_m = _mx(xdt)
    xd_m = _mx(xdt * dte_e)
    lane_lo = lane < HEAD

    y_cols = []
    for g in range(SSD_GROUPS):
        bg = bcc[:, g * n:(g + 1) * n]
        cg = _mx(bcc[:, g_n + g * n:g_n + (g + 1) * n])
        cb = _dot_nt(cg, _mx(bg))
        gs = slice(g * gw, (g + 1) * gw)
        st = st_ref[g]
        y_off = _dot(cg, _mx(st)) * ecs_e[:, gs]
        st_ref[g] = st * ecs_e[q - 1:q, gs] + _dot(_mx(bg.T), xd_m[:, gs])
        for pr in range(gw // LANES):
            h0 = (g * gw + pr * LANES) // HEAD
            ps = slice(g * gw + pr * LANES, g * gw + (pr + 1) * LANES)
            res = []
            for hh in (h0, h0 + 1):
                li = MISC_DT + hh
                seg = cs[:, li:li + 1] - cs_t[li:li + 1, :]
                lm = jnp.where(tri, jnp.exp(jnp.where(tri, seg, 0.0)), 0.0)
                res.append(_dot(_mx(cb * lm), xdt_m[:, ps]))
            y_diag = jnp.where(lane_lo, res[0], res[1])
            y_cols.append(y_diag + y_off[:, pr * LANES:(pr + 1) * LANES])
    y = jnp.concatenate(y_cols, axis=1) + xc * dvec_ref[...]
    zz = z_ref[...]
    y = y * (zz * _sigmoid(zz))
    outs = []
    for g in range(SSD_GROUPS):
        yg = y[:, g * gw:(g + 1) * gw]
        ms = jnp.mean(yg * yg, axis=-1, keepdims=True)
        outs.append(yg * lax.rsqrt(ms + NORM_EPS))
    y = jnp.concatenate(outs, axis=1) * ng_ref[...]
    o_ref[...] = y.astype(o_ref.dtype)


def _ssd(u, bsz, seq, conv_w, conv_b, dt_bias, a_log, d_skip, norm_g):
    q = SSD_CHUNK
    nc = seq // q
    w = W_MIX
    gw = w // SSD_GROUPS
    heads = w // HEAD
    bcw = 2 * SSD_GROUPS * SSD_STATE
    cw = conv_w[:, 0, :]
    pad = lambda v: jnp.zeros((1, LANES), F32).at[0, MISC_DT:MISC_DT + heads].set(v)
    esel = np.zeros((LANES, w), np.float32)
    for h in range(heads):
        esel[MISC_DT + h, h * HEAD:(h + 1) * HEAD] = 1.0
    rowblk = lambda cb: (lambda b, c: (b * nc + c, cb))
    const = lambda b, c: (0, 0)
    return pl.pallas_call(
        _ssd_body,
        grid=(bsz, nc),
        in_specs=[pl.BlockSpec((q, w), rowblk(COL_Z // w)),
                  pl.BlockSpec((q, w), rowblk(COL_XS // w)),
                  pl.BlockSpec((q, bcw), rowblk(COL_BC // bcw)),
                  pl.BlockSpec((q, LANES), rowblk(COL_MISC // LANES)),
                  pl.BlockSpec((SSD_CONV, w), const), pl.BlockSpec((1, w), const),
                  pl.BlockSpec((SSD_CONV, bcw), const), pl.BlockSpec((1, bcw), const),
                  pl.BlockSpec((1, LANES), const), pl.BlockSpec((1, LANES), const),
                  pl.BlockSpec((1, w), const), pl.BlockSpec((1, w), const),
                  pl.BlockSpec((LANES, w), const)],
        out_specs=pl.BlockSpec((q, w), lambda b, c: (b * nc + c, 0)),
        out_shape=jax.ShapeDtypeStruct((bsz * seq, w), MXU_DTYPE),
        scratch_shapes=[pltpu.VMEM((q + SUBLANES, w), F32),
                        pltpu.VMEM((q + SUBLANES, bcw), F32),
                        pltpu.VMEM((SSD_GROUPS, SSD_STATE, gw), F32)],
        compiler_params=_cparams(("arbitrary", "arbitrary")),
        name="ssd_mixer",
    )(u, u, u, u, cw[:, :w], conv_b[:w].reshape(1, w), cw[:, w:], conv_b[w:].reshape(1, bcw),
      pad(dt_bias), pad(a_log), jnp.repeat(d_skip, HEAD).reshape(1, w), norm_g.reshape(1, w),
      jnp.asarray(esel, MXU_DTYPE))


def _unit_lower_inverse(l_mats, blk_mask, eye_f):
    mm = lambda xs, ys: [_dot(x, y) for x, y in zip(xs, ys)]
    mx = lambda xs: [_mx(x) for x in xs]
    add = lambda xs, ys: [x + y for x, y in zip(xs, ys)]
    d = [jnp.where(blk_mask, l, 0.0) for l in l_mats]
    o = mx([l - x for l, x in zip(l_mats, d)])
    d1 = mx(d)
    d2 = mx(mm(d1, d1))
    d4 = mx(mm(d2, d2))
    d8 = mx(mm(d4, d4))
    t = [eye_f + x for x in d]
    t = add(t, mm(d2, mx(t)))
    t = add(t, mm(d4, mx(t)))
    t_d = add(t, mm(d8, mx(t)))
    t_dm = mx(t_d)
    m1 = mm(t_dm, o)
    m1m = mx(m1)
    m2 = mm(m1m, m1m)
    m3 = mm(m1m, mx(m2))
    w = mx([x + y + z for x, y, z in zip(m1, m2, m3)])
    return add(t_d, mm(w, t_dm))


def _rwkv_masks():
    lane = lax.broadcasted_iota(jnp.int32, (1, LANES), 1)
    row = lax.broadcasted_iota(jnp.int32, (LANES, LANES), 0)
    col = lax.broadcasted_iota(jnp.int32, (LANES, LANES), 1)
    same = (row >= HEAD) == (col >= HEAD)
    rs = row & (HEAD - 1)
    cs = col & (HEAD - 1)
    eye = row == col
    return dict(lane_lo=lane < HEAD, strict=same & (rs > cs), incl=same & (rs >= cs),
                blk=(row >> 4) == (col >> 4), eye=eye, eye_f=jnp.where(eye, 1.0, 0.0))


def _rwkv_head_sum(x, ones_blk):
    cols = [_dot_split_lhs(x[:, t * LANES:(t + 1) * LANES], ones_blk, 2)
            for t in range(x.shape[1] // LANES)]
    return jnp.concatenate(cols, axis=1)


def _rwkv_prepare(srcs, bufs, mus, prm, masks, first):
    cl = RWKV_CHUNK
    w0_ref, w2_ref, a0_ref, a2_ref, g2_ref, kk_ref, ka_ref, rk_ref, ones_ref = prm
    vals = []
    for src, buf, mu_ref in zip(srcs, bufs, mus):
        if first:
            buf[0:SUBLANES, :] = jnp.zeros((SUBLANES, buf.shape[1]), F32)
        else:
            buf[0:SUBLANES, :] = buf[cl:cl + SUBLANES, :]
        cur = src[...]
        buf[SUBLANES:SUBLANES + cl, :] = cur
        prev = buf[SUBLANES - 1:SUBLANES - 1 + cl, :]
        vals.append(cur + (prev - cur) * mu_ref[...])
    r, k, v, pg, misc = vals
    ones_blk = ones_ref[...]

    w_log = -_softplus(-(w0_ref[...] + _dot(_mx(jnp.tanh(misc)), w2_ref[...]))) - 0.5
    lw = -jnp.exp(w_log)
    iclr = _sigmoid(a0_ref[...] + _dot(_mx(misc), a2_ref[...]))
    gate = _dot(_mx(_sigmoid(pg)), g2_ref[...])
    kk = k * kk_ref[...]
    kk = kk / jnp.maximum(jnp.sqrt(_rwkv_head_sum(kk * kk, ones_blk)), 1e-12)
    k = k * (1.0 + (iclr - 1.0) * ka_ref[...])
    bonus = _rwkv_head_sum(r * k * rk_ref[...], ones_blk) * v

    row_c = lax.broadcasted_iota(jnp.int32, (cl, cl), 0)
    col_c = lax.broadcasted_iota(jnp.int32, (cl, cl), 1)
    tril_c = jnp.where(row_c >= col_c, 1.0, 0.0).astype(MXU_DTYPE)
    cum = _dot_split_rhs(tril_c, lw, 3)
    p_in = jnp.exp(cum)
    p_inv = jnp.exp(-cum)
    a_t = -(kk * jnp.exp(cum - lw))
    b_t = kk * iclr * p_inv
    k_t = k * p_inv
    r_t = r * p_in
    p_end = p_in[cl - 1:cl, :]
    bp_t = b_t * p_end
    kp_t = k_t * p_end

    lane_lo = masks["lane_lo"]
    tiles = range(r.shape[1] // LANES)
    tile = lambda x, p: x[:, p * LANES:(p + 1) * LANES]
    stack2 = lambda x: jnp.concatenate([jnp.where(lane_lo, x, 0.0), jnp.where(lane_lo, 0.0, x)], axis=0)
    stacked = lambda x: [_mx(stack2(tile(x, p))) for p in tiles]
    bkp_t = [_mx(jnp.concatenate([stack2(tile(bp_t, p)), stack2(tile(kp_t, p))], axis=0).T)
             for p in tiles]
    return dict(a2=stacked(a_t), r2=stacked(r_t), b2=stacked(b_t), k2=stacked(k_t), v2=stacked(v),
                bkp=bkp_t, p_end=p_end, bonus=bonus, gate=gate)


def _rwkv_scan(ops, st_ref, ybuf, lnw_ref, lnb_ref, ones_ref, masks):
    cl = RWKV_CHUNK
    a2, r2, b2, k2, v2, bkp_t = ops["a2"], ops["r2"], ops["b2"], ops["k2"], ops["v2"], ops["bkp"]
    p_end = ops["p_end"]
    tiles = range(len(a2))
    cat0 = lambda *xs: jnp.concatenate(xs, axis=0)
    cat1 = lambda *xs: jnp.concatenate(xs, axis=1)
    gram = [_dot_nt(cat0(a2[p], r2[p]), cat0(b2[p], k2[p])) for p in tiles]
    a_ab = [jnp.where(masks["strict"], g[0:LANES, 0:LANES], 0.0) for g in gram]
    a_ak = [_mx(jnp.where(masks["strict"], g[0:LANES, LANES:], 0.0)) for g in gram]
    a_rb = [_mx(jnp.where(masks["incl"], g[LANES:, 0:LANES], 0.0)) for g in gram]
    a_rk = [_mx(jnp.where(masks["incl"], g[LANES:, LANES:], 0.0)) for g in gram]
    t_inv = _unit_lower_inverse(a_ab, masks["blk"], masks["eye_f"])
    st = [st_ref[p] for p in tiles]
    st_m = [_mx(s) for s in st]
    y0 = [_dot(cat1(a2[p], a_ak[p]), cat0(st_m[p], v2[p])) for p in tiles]
    sa = [_mx(_dot(_mx(t_inv[p]), _mx(y0[p]))) for p in tiles]
    o2 = [_dot(cat1(r2[p], a_rb[p], a_rk[p]), cat0(st_m[p], sa[p], v2[p])) for p in tiles]
    upd = [_dot(bkp_t[p], cat0(sa[p], v2[p])) for p in tiles]
    for p in tiles:
        ps = slice(p * LANES, (p + 1) * LANES)
        ybuf[:, ps] = o2[p][0:cl, :] + o2[p][cl:, :]
        p_col = jnp.sum(jnp.where(masks["eye"], p_end[:, ps], 0.0), axis=1, keepdims=True)
        st_ref[p] = st[p] * p_col + upd[p]

    y = ybuf[...]
    ones_blk = ones_ref[...]
    inv_n = 1.0 / HEAD
    mean = _rwkv_head_sum(y, ones_blk) * inv_n
    d = y - mean
    var = _rwkv_head_sum(d * d, ones_blk) * inv_n
    y = d * lax.rsqrt(var + RWKV_LN_EPS) * lnw_ref[...] + lnb_ref[...]
    return (y + ops["bonus"]) * ops["gate"]


_RWKV_STAGED = ("a2", "r2", "b2", "k2", "v2", "bkp")


def _rwkv_body(*refs):
    cl = RWKV_CHUNK
    first_srcs, odd_srcs, next_srcs = refs[0:5], refs[5:10], refs[10:15]
    mus = refs[15:20]
    prm = refs[20:28] + (refs[30],)
    lnw_ref, lnb_ref, ones_ref = refs[28], refs[29], refs[30]
    o_ref = refs[31]
    bufs = refs[32:37]
    st_ref, ybuf = refs[37], refs[38]
    staged = dict(zip(_RWKV_STAGED, refs[39:45]))
    s_pend, s_bonus, s_gate = refs[45], refs[46], refs[47]
    masks = _rwkv_masks()
    n_tiles = st_ref.shape[0]

    def stash(ops):
        for name in _RWKV_STAGED:
            for p in range(n_tiles):
                staged[name][p] = ops[name][p]
        s_pend[...] = ops["p_end"]
        s_bonus[...] = ops["bonus"]
        s_gate[...] = ops["gate"]

    @pl.when(pl.program_id(1) == 0)
    def _():
        st_ref[...] = jnp.zeros(st_ref.shape, F32)
        stash(_rwkv_prepare(first_srcs, bufs, mus, prm, masks, first=True))

    even = {name: [staged[name][p] for p in range(n_tiles)] for name in _RWKV_STAGED}
    even.update(p_end=s_pend[...], bonus=s_bonus[...], gate=s_gate[...])
    o_ref[0:cl, :] = _rwkv_scan(even, st_ref, ybuf, lnw_ref, lnb_ref, ones_ref, masks).astype(o_ref.dtype)
    odd = _rwkv_prepare(odd_srcs, bufs, mus, prm, masks, first=False)
    o_ref[cl:2 * cl, :] = _rwkv_scan(odd, st_ref, ybuf, lnw_ref, lnb_ref, ones_ref, masks).astype(o_ref.dtype)
    stash(_rwkv_prepare(next_srcs, bufs, mus, prm, masks, first=False))


def _rwkv(u, bsz, seq, mu, w0, w2, a0, a2, g2, k_k, k_a, r_k, ln_w, ln_b):
    cl = RWKV_CHUNK
    nc = seq // cl
    assert nc % 2 == 0
    w = W_MIX
    mw = 2 * LANES
    n_tiles = w // LANES
    row1 = lambda vec: vec.reshape(1, -1)
    o_pw = 3 * w
    o_pa = o_pw + RWKV_DECAY_RANK
    o_pg = o_pa + RWKV_AAA_RANK
    mu_m = jnp.zeros((mw,), F32)
    mu_m = mu_m.at[MISC_PW:MISC_PW + RWKV_DECAY_RANK].set(mu[o_pw:o_pa])
    mu_m = mu_m.at[MISC_PA:MISC_PA + RWKV_AAA_RANK].set(mu[o_pa:o_pg])
    w2e = jnp.zeros((mw, w), F32).at[MISC_PW:MISC_PW + RWKV_DECAY_RANK].set(w2)
    a2e = jnp.zeros((mw, w), F32).at[MISC_PA:MISC_PA + RWKV_AAA_RANK].set(a2)
    ones_blk = np.kron(np.eye(LANES // HEAD, dtype=np.float32), np.ones((HEAD, HEAD), np.float32))
    widths = (w, w, w, mw, mw)
    colblk = (COL_R // w, COL_K // w, COL_V // w, COL_PG // mw, COL_MISC // mw)

    def chunk_specs(chunk_of):
        return [pl.BlockSpec((cl, wd), (lambda b, j, cb=cb: (b * nc + chunk_of(j), cb)))
                for wd, cb in zip(widths, colblk)]

    cvec = lambda width: pl.BlockSpec((1, width), lambda b, j: (0, 0))
    cmat = lambda rows, width: pl.BlockSpec((rows, width), lambda b, j: (0, 0))
    stage = lambda cols: pltpu.VMEM((n_tiles, LANES, cols), MXU_DTYPE)
    return pl.pallas_call(
        _rwkv_body,
        grid=(bsz, nc // 2),
        in_specs=(chunk_specs(lambda j: 0) + chunk_specs(lambda j: 2 * j + 1)
                  + chunk_specs(lambda j: jnp.minimum(2 * j + 2, nc - 1))
                  + [cvec(w), cvec(w), cvec(w), cvec(mw), cvec(mw),
                     cvec(w), cmat(mw, w), cvec(w), cmat(mw, w), cmat(RWKV_GATE_RANK, w),
                     cvec(w), cvec(w), cvec(w), cvec(w), cvec(w), cmat(LANES, LANES)]),
        out_specs=pl.BlockSpec((2 * cl, w), lambda b, j: (b * (nc // 2) + j, 0)),
        out_shape=jax.ShapeDtypeStruct((bsz * seq, w), MXU_DTYPE),
        scratch_shapes=[pltpu.VMEM((cl + SUBLANES, w), F32), pltpu.VMEM((cl + SUBLANES, w), F32),
                        pltpu.VMEM((cl + SUBLANES, w), F32), pltpu.VMEM((cl + SUBLANES, mw), F32),
                        pltpu.VMEM((cl + SUBLANES, mw), F32),
                        pltpu.VMEM((n_tiles, LANES, LANES), F32),
                        pltpu.VMEM((cl, w), F32),
                        stage(LANES), stage(LANES), stage(LANES), stage(LANES), stage(LANES),
                        stage(2 * LANES),
                        pltpu.VMEM((1, w), F32), pltpu.VMEM((cl, w), F32), pltpu.VMEM((cl, w), F32)],
        compiler_params=_cparams(("arbitrary", "arbitrary")),
        name="rwkv7_mixer",
    )(*([u] * 15),
      row1(mu[0:w]), row1(mu[w:2 * w]), row1(mu[2 * w:3 * w]), row1(mu[o_pg:o_pg + RWKV_GATE_RANK]),
      row1(mu_m), row1(w0), _mx(w2e), row1(a0), _mx(a2e), _mx(g2), row1(k_k), row1(k_a),
      row1(r_k), row1(ln_w), row1(ln_b), jnp.asarray(ones_blk, MXU_DTYPE))


def _attn_body(q_ref, k_ref, v_ref, o_ref):
    d = q_ref.shape[1]
    hd = d // XATTN_HEADS
    scale = hd ** -0.5
    for h in range(XATTN_HEADS):
        hs = slice(h * hd, (h + 1) * hd)
        s = _dot_nt(q_ref[:, hs], k_ref[0, :, hs]) * scale
        s = s - jnp.max(s, axis=-1, keepdims=True)
        e = jnp.exp(s)
        p = e / jnp.sum(e, axis=-1, keepdims=True)
        o_ref[:, hs] = _dot(_mx(p), v_ref[0, :, hs]).astype(o_ref.dtype)


def _attention(q, k, v, bsz, seq, ts=512):
    d = q.shape[1]
    n_mem = k.shape[1]
    ts = min(ts, seq)
    ns = seq // ts
    return pl.pallas_call(
        _attn_body,
        grid=(bsz, ns),
        in_specs=[pl.BlockSpec((ts, d), lambda b, s: (b * ns + s, 0)),
                  pl.BlockSpec((1, n_mem, d), lambda b, s: (b, 0, 0)),
                  pl.BlockSpec((1, n_mem, d), lambda b, s: (b, 0, 0))],
        out_specs=pl.BlockSpec((ts, d), lambda b, s: (b * ns + s, 0)),
        out_shape=jax.ShapeDtypeStruct((bsz * seq, d), MXU_DTYPE),
        compiler_params=_cparams(("arbitrary", "arbitrary")),
        name="mem_attention",
    )(q, k, v)


def _in_proj_columns(d_in):
    w = W_MIX
    bcw = 2 * SSD_GROUPS * SSD_STATE
    heads = w // HEAD
    ssd_in = 2 * w + bcw + heads
    idx = np.full((N_U,), -1, np.int64)
    idx[COL_Z:COL_Z + w] = np.arange(0, w)
    idx[COL_XS:COL_XS + w] = np.arange(w, 2 * w)
    idx[COL_BC:COL_BC + bcw] = np.arange(2 * w, 2 * w + bcw)
    idx[COL_MISC + MISC_DT:COL_MISC + MISC_DT + heads] = np.arange(2 * w + bcw, ssd_in)
    rw = ssd_in
    idx[COL_R:COL_R + w] = rw + np.arange(0, w)
    idx[COL_K:COL_K + w] = rw + np.arange(w, 2 * w)
    idx[COL_V:COL_V + w] = rw + np.arange(2 * w, 3 * w)
    o = 3 * w
    idx[COL_MISC + MISC_PW:COL_MISC + MISC_PW + RWKV_DECAY_RANK] = rw + o + np.arange(RWKV_DECAY_RANK)
    o += RWKV_DECAY_RANK
    idx[COL_MISC + MISC_PA:COL_MISC + MISC_PA + RWKV_AAA_RANK] = rw + o + np.arange(RWKV_AAA_RANK)
    o += RWKV_AAA_RANK
    idx[COL_PG:COL_PG + RWKV_GATE_RANK] = rw + o + np.arange(RWKV_GATE_RANK)
    assert rw + o + RWKV_GATE_RANK == d_in
    return idx


def _permute_in_proj(w_in):
    idx = _in_proj_columns(w_in.shape[1])
    pieces, start = [], 0
    while start < N_U:
        stop = start + 1
        if idx[start] < 0:
            while stop < N_U and idx[stop] < 0:
                stop += 1
            pieces.append(jnp.zeros((w_in.shape[0], stop - start), w_in.dtype))
        else:
            while stop < N_U and idx[stop] == idx[stop - 1] + 1:
                stop += 1
            pieces.append(w_in[:, idx[start]:idx[stop - 1] + 1])
        start = stop
    return jnp.concatenate(pieces, axis=1)


def kernel(x, mem, norm_mix_g, w_in, ssd_conv_w, ssd_conv_b, ssd_dt_bias, ssd_a_log, ssd_d, ssd_norm_g, rwkv_mu, rwkv_w0, rwkv_w2, rwkv_a0, rwkv_a2, rwkv_g2, rwkv_k_k, rwkv_k_a, rwkv_r_k, rwkv_ln_w, rwkv_ln_b, w_out, norm_x_g, norm_mem_g, xattn_wq, xattn_wk, xattn_wv, xattn_wo, norm_ffn_g, ffn_w1, ffn_w2, final_norm_g):
    bsz, seq, d = x.shape
    n_mem = mem.shape[1]
    xr = x.reshape(bsz * seq, d)
    memr = mem.reshape(bsz * n_mem, d)
    for l in range(w_in.shape[0]):
        h = _rmsnorm(xr, norm_mix_g[l], MXU_DTYPE)
        u = _matmul_resident(h, _permute_in_proj(w_in[l]), F32, name="in_proj")
        y_ssd = _ssd(u, bsz, seq, ssd_conv_w[l], ssd_conv_b[l], ssd_dt_bias[l], ssd_a_log[l],
                     ssd_d[l], ssd_norm_g[l])
        y_rwkv = _rwkv(u, bsz, seq, rwkv_mu[l], rwkv_w0[l], rwkv_w2[l], rwkv_a0[l], rwkv_a2[l],
                       rwkv_g2[l], rwkv_k_k[l], rwkv_k_a[l], rwkv_r_k[l].reshape(-1),
                       rwkv_ln_w[l], rwkv_ln_b[l])
        wo = _mx(w_out[l])
        xr, h = _matmul_rows([(y_ssd, wo[:W_MIX]), (y_rwkv, wo[W_MIX:])], xr, norm_x_g[l],
                             name="out_proj")

        m = _rmsnorm(memr, norm_mem_g[l], MXU_DTYPE)
        q = _matmul_resident(h, xattn_wq[l], MXU_DTYPE, name="q_proj")
        kx = _matmul_resident(m, xattn_wk[l], MXU_DTYPE, name="k_proj")
        vx = _matmul_resident(m, xattn_wv[l], MXU_DTYPE, name="v_proj")
        o = _attention(q, kx.reshape(bsz, n_mem, d), vx.reshape(bsz, n_mem, d), bsz, seq)
        xr, h = _matmul_rows([(o, _mx(xattn_wo[l]))], xr, norm_ffn_g[l], name="o_proj")

        hid = _matmul([(h, _mx(ffn_w1[l]))], MXU_DTYPE, act="relu2", name="ffn_up")
        xr = _matmul([(hid, _mx(ffn_w2[l]))], F32, res=xr, name="ffn_down")
    return _rmsnorm(xr, final_norm_g, x.dtype).reshape(bsz, seq, d)
```

```python
import functools

import numpy as np
import jax
import jax.numpy as jnp
from jax import lax
from jax.experimental import pallas as pl
from jax.experimental.pallas import tpu as pltpu

F32 = jnp.float32
MXU_DTYPE = jnp.bfloat16

NORM_EPS = 1e-6
RWKV_LN_EPS = 64e-5

HEAD = 64
SSD_STATE = 128
SSD_GROUPS = 2
SSD_CHUNK = 128
SSD_CONV = 4
RWKV_CHUNK = 64
RWKV_DECAY_RANK = 96
RWKV_AAA_RANK = 96
RWKV_GATE_RANK = 256
XATTN_HEADS = 4

LANES = 128
SUBLANES = 8
VMEM_LIMIT = 56 * 1024 * 1024

W_MIX = 1024
COL_Z, COL_XS, COL_R, COL_K, COL_V = 0, 1024, 2048, 3072, 4096
COL_BC, COL_PG, COL_MISC = 5120, 5632, 5888
N_U = 6144
MISC_PW, MISC_DT, MISC_PA = 0, 96, 128


def _mx(a):
    return a.astype(MXU_DTYPE)


def _dot(a, b):
    return jnp.dot(a, b, preferred_element_type=F32)


def _dot_nt(a, b):
    return lax.dot_general(a, b, (((1,), (1,)), ((), ())), preferred_element_type=F32)


def _split(v, parts):
    out = []
    rem = v
    for _ in range(parts):
        p = rem.astype(MXU_DTYPE)
        out.append(p)
        rem = rem - p.astype(F32)
    return out


def _dot_split_rhs(a01, v, parts):
    acc = None
    for p in _split(v, parts):
        t = _dot(a01, p)
        acc = t if acc is None else acc + t
    return acc


def _dot_split_lhs(v, b01, parts):
    acc = None
    for p in _split(v, parts):
        t = _dot(p, b01)
        acc = t if acc is None else acc + t
    return acc


def _sigmoid(x):
    return 1.0 / (1.0 + jnp.exp(-x))


def _softplus(x):
    return jnp.maximum(x, 0.0) + jnp.log(1.0 + jnp.exp(-jnp.abs(x)))


def _cparams(sem):
    return pltpu.CompilerParams(dimension_semantics=sem, vmem_limit_bytes=VMEM_LIMIT)


def _rmsnorm_body(x_ref, g_ref, o_ref):
    x = x_ref[...]
    ms = jnp.mean(x * x, axis=-1, keepdims=True)
    o_ref[...] = (x * lax.rsqrt(ms + NORM_EPS) * g_ref[...]).astype(o_ref.dtype)


def _rmsnorm(x, g, out_dtype, tm=512):
    m, d = x.shape
    tm = min(tm, m)
    return pl.pallas_call(
        _rmsnorm_body,
        grid=(m // tm,),
        in_specs=[pl.BlockSpec((tm, d), lambda i: (i, 0)),
                  pl.BlockSpec((1, d), lambda i: (0, 0))],
        out_specs=pl.BlockSpec((tm, d), lambda i: (i, 0)),
        out_shape=jax.ShapeDtypeStruct((m, d), out_dtype),
        compiler_params=_cparams(("arbitrary",)),
        name="rmsnorm",
    )(x, g.reshape(1, d))


def _mm_resident_body(a_ref, b_ref, o_ref, bm_ref, *, act):
    @pl.when(pl.program_id(1) == 0)
    def _():
        bm_ref[...] = b_ref[...].astype(MXU_DTYPE)

    r = _dot(a_ref[...], bm_ref[...])
    if act == "relu2":
        r = jnp.square(jnp.maximum(r, 0.0))
    o_ref[...] = r.astype(o_ref.dtype)


def _matmul_resident(a, b, out_dtype, act=None, tm=1024, tn=1024, name="matmul_resident"):
    m, kdim = a.shape
    n = b.shape[1]
    tm, tn = min(tm, m), min(tn, n)
    return pl.pallas_call(
        functools.partial(_mm_resident_body, act=act),
        grid=(n // tn, m // tm),
        in_specs=[pl.BlockSpec((tm, kdim), lambda j, i: (i, 0)),
                  pl.BlockSpec((kdim, tn), lambda j, i: (0, j))],
        out_specs=pl.BlockSpec((tm, tn), lambda j, i: (i, j)),
        out_shape=jax.ShapeDtypeStruct((m, n), out_dtype),
        scratch_shapes=[pltpu.VMEM((kdim, tn), MXU_DTYPE)],
        compiler_params=_cparams(("arbitrary", "arbitrary")),
        name=name,
    )(a, b)


def _mm_rows_body(*refs, n_pairs, nk, emit_x):
    ab = refs[:2 * n_pairs]
    res_ref, g_ref = refs[2 * n_pairs:2 * n_pairs + 2]
    outs = refs[2 * n_pairs + 2:]
    x_ref = outs[0] if emit_x else None
    h_ref = outs[int(emit_x)]
    acc_ref = outs[int(emit_x) + 1] if nk > 1 else None
    part = None
    for p in range(n_pairs):
        t = _dot(ab[2 * p][...], ab[2 * p + 1][...])
        part = t if part is None else part + t

    def finish(acc):
        x = res_ref[...] + acc
        if emit_x:
            x_ref[...] = x
        ms = jnp.mean(x * x, axis=-1, keepdims=True)
        h_ref[...] = (x * lax.rsqrt(ms + NORM_EPS) * g_ref[...]).astype(h_ref.dtype)

    if nk == 1:
        finish(part)
        return

    k = pl.program_id(1)

    @pl.when(k == 0)
    def _():
        acc_ref[...] = part

    @pl.when(k > 0)
    def _():
        acc_ref[...] += part

    @pl.when(k == nk - 1)
    def _():
        finish(acc_ref[...])


def _matmul_rows(pairs, res, g, h_dtype, emit_x=True, tm=512, tk=2048, name="matmul_rows"):
    m, n = res.shape
    tm = min(tm, m)
    kdims = [a.shape[1] for a, _ in pairs]
    tk = min(tk, min(kdims))
    nk = kdims[0] // tk
    assert all(kd == nk * tk for kd in kdims)
    in_specs, args = [], []
    for a, b in pairs:
        in_specs += [pl.BlockSpec((tm, tk), lambda i, k: (i, k)),
                     pl.BlockSpec((tk, n), lambda i, k: (k, 0))]
        args += [a, b]
    row_spec = pl.BlockSpec((tm, n), lambda i, k: (i, 0))
    in_specs += [row_spec, pl.BlockSpec((1, n), lambda i, k: (0, 0))]
    out_specs, out_shape = [row_spec], [jax.ShapeDtypeStruct((m, n), h_dtype)]
    if emit_x:
        out_specs, out_shape = [row_spec] + out_specs, [jax.ShapeDtypeStruct((m, n), F32)] + out_shape
    outs = pl.pallas_call(
        functools.partial(_mm_rows_body, n_pairs=len(pairs), nk=nk, emit_x=emit_x),
        grid=(m // tm, nk),
        in_specs=in_specs,
        out_specs=out_specs,
        out_shape=out_shape,
        scratch_shapes=[pltpu.VMEM((tm, n), F32)] if nk > 1 else [],
        compiler_params=_cparams(("arbitrary", "arbitrary")),
        name=name,
    )(*args, res, g.reshape(1, n))
    return tuple(outs) if emit_x else outs[0]


def _ssd_body(z_ref, xs_ref, bc_ref, m_ref, cwx_ref, cbx_ref, cwbc_ref, cbbc_ref, dtb_ref,
              alog_ref, dvec_ref, ng_ref, esel_ref, o_ref, xbuf, bcbuf, st_ref):
    q = SSD_CHUNK
    n = SSD_STATE
    gw = st_ref.shape[2]
    c = pl.program_id(1)

    @pl.when(c == 0)
    def _():
        xbuf[0:SUBLANES, :] = jnp.zeros((SUBLANES, xbuf.shape[1]), F32)
        bcbuf[0:SUBLANES, :] = jnp.zeros((SUBLANES, bcbuf.shape[1]), F32)
        st_ref[...] = jnp.zeros(st_ref.shape, F32)

    @pl.when(c > 0)
    def _():
        xbuf[0:SUBLANES, :] = xbuf[q:q + SUBLANES, :]
        bcbuf[0:SUBLANES, :] = bcbuf[q:q + SUBLANES, :]

    xbuf[SUBLANES:SUBLANES + q, :] = xs_ref[...]
    bcbuf[SUBLANES:SUBLANES + q, :] = bc_ref[...]

    def conv_silu(buf, w_ref, b_ref):
        acc = None
        for k in range(SSD_CONV):
            off = SUBLANES - (SSD_CONV - 1) + k
            t = buf[off:off + q, :] * w_ref[k:k + 1, :]
            acc = t if acc is None else acc + t
        acc = acc + b_ref[...]
        return acc * _sigmoid(acc)

    xc = conv_silu(xbuf, cwx_ref, cbx_ref)
    bcc = conv_silu(bcbuf, cwbc_ref, cbbc_ref)
    g_n = SSD_GROUPS * n

    lane = lax.broadcasted_iota(jnp.int32, (1, LANES), 1)
    n_heads = SSD_GROUPS * gw // HEAD
    dmask = (lane >= MISC_DT) & (lane < MISC_DT + n_heads)
    dt = jnp.where(dmask, _softplus(m_ref[...] + dtb_ref[...]), 0.0)
    a = jnp.where(dmask, -jnp.exp(alog_ref[...]), 0.0)
    da = dt * a
    row = lax.broadcasted_iota(jnp.int32, (q, q), 0)
    col = lax.broadcasted_iota(jnp.int32, (q, q), 1)
    tri = row >= col
    tril = jnp.where(tri, 1.0, 0.0).astype(MXU_DTYPE)
    cs = _dot_split_rhs(tril, da, 3)
    ecs = jnp.exp(cs)
    dte = jnp.exp(cs[q - 1:q, :] - cs)
    esel = esel_ref[...]
    dt_e = _dot_split_lhs(dt, esel, 2)
    ecs_e = _dot_split_lhs(ecs, esel, 2)
    dte_e = _dot_split_lhs(dte, esel, 2)
    cs_t = cs.T

    xdt = xc * dt_e
    xdt_m = _mx(xdt)
    xd_m = _mx(xdt * dte_e)
    lane_lo = lane < HEAD

    y_cols = []
    for g in range(SSD_GROUPS):
        bg = bcc[:, g * n:(g + 1) * n]
        cg = _mx(bcc[:, g_n + g * n:g_n + (g + 1) * n])
        cb = _dot_nt(cg, _mx(bg))
        gs = slice(g * gw, (g + 1) * gw)
        st = st_ref[g]
        y_off = _dot(cg, _mx(st)) * ecs_e[:, gs]
        st_ref[g] = st * ecs_e[q - 1:q, gs] + _dot(_mx(bg.T), xd_m[:, gs])
        for pr in range(gw // LANES):
            h0 = (g * gw + pr * LANES) // HEAD
            ps = slice(g * gw + pr * LANES, g * gw + (pr + 1) * LANES)
            res = []
            for hh in (h0, h0 + 1):
                li = MISC_DT + hh
                seg = cs[:, li:li + 1] - cs_t[li:li + 1, :]
                lm = jnp.where(tri, jnp.exp(jnp.where(tri, seg, 0.0)), 0.0)
                res.append(_dot(_mx(cb * lm), xdt_m[:, ps]))
            y_diag = jnp.where(lane_lo, res[0], res[1])
            y_cols.append(y_diag + y_off[:, pr * LANES:(pr + 1) * LANES])
    y = jnp.concatenate(y_cols, axis=1) + xc * dvec_ref[...]
    zz = z_ref[...]
    y = y * (zz * _sigmoid(zz))
    outs = []
    for g in range(SSD_GROUPS):
        yg = y[:, g * gw:(g + 1) * gw]
        ms = jnp.mean(yg * yg, axis=-1, keepdims=True)
        outs.append(yg * lax.rsqrt(ms + NORM_EPS))
    y = jnp.concatenate(outs, axis=1) * ng_ref[...]
    o_ref[...] = y.astype(o_ref.dtype)


def _ssd(u, bsz, seq, conv_w, conv_b, dt_bias, a_log, d_skip, norm_g):
    q = SSD_CHUNK
    nc = seq // q
    w = W_MIX
    gw = w // SSD_GROUPS
    heads = w // HEAD
    bcw = 2 * SSD_GROUPS * SSD_STATE
    cw = conv_w[:, 0, :]
    pad = lambda v: jnp.zeros((1, LANES), F32).at[0, MISC_DT:MISC_DT + heads].set(v)
    esel = np.zeros((LANES, w), np.float32)
    for h in range(heads):
        esel[MISC_DT + h, h * HEAD:(h + 1) * HEAD] = 1.0
    rowblk = lambda cb: (lambda b, c: (b * nc + c, cb))
    const = lambda b, c: (0, 0)
    return pl.pallas_call(
        _ssd_body,
        grid=(bsz, nc),
        in_specs=[pl.BlockSpec((q, w), rowblk(COL_Z // w)),
                  pl.BlockSpec((q, w), rowblk(COL_XS // w)),
                  pl.BlockSpec((q, bcw), rowblk(COL_BC // bcw)),
                  pl.BlockSpec((q, LANES), rowblk(COL_MISC // LANES)),
                  pl.BlockSpec((SSD_CONV, w), const), pl.BlockSpec((1, w), const),
                  pl.BlockSpec((SSD_CONV, bcw), const), pl.BlockSpec((1, bcw), const),
                  pl.BlockSpec((1, LANES), const), pl.BlockSpec((1, LANES), const),
                  pl.BlockSpec((1, w), const), pl.BlockSpec((1, w), const),
                  pl.BlockSpec((LANES, w), const)],
        out_specs=pl.BlockSpec((q, w), lambda b, c: (b * nc + c, 0)),
        out_shape=jax.ShapeDtypeStruct((bsz * seq, w), MXU_DTYPE),
        scratch_shapes=[pltpu.VMEM((q + SUBLANES, w), F32),
                        pltpu.VMEM((q + SUBLANES, bcw), F32),
                        pltpu.VMEM((SSD_GROUPS, SSD_STATE, gw), F32)],
        compiler_params=_cparams(("arbitrary", "arbitrary")),
        name="ssd_mixer",
    )(u, u, u, u, cw[:, :w], conv_b[:w].reshape(1, w), cw[:, w:], conv_b[w:].reshape(1, bcw),
      pad(dt_bias), pad(a_log), jnp.repeat(d_skip, HEAD).reshape(1, w), norm_g.reshape(1, w),
      jnp.asarray(esel, MXU_DTYPE))


def _unit_lower_inverse(l_mats, blk_mask, eye_f):
    mm = lambda xs, ys: [_dot(x, y) for x, y in zip(xs, ys)]
    mx = lambda xs: [_mx(x) for x in xs]
    add = lambda xs, ys: [x + y for x, y in zip(xs, ys)]
    d = [jnp.where(blk_mask, l, 0.0) for l in l_mats]
    o = mx([l - x for l, x in zip(l_mats, d)])
    d1 = mx(d)
    d2 = mx(mm(d1, d1))
    t = [eye_f + x for x in d]
    yield
    d4 = mx(mm(d2, d2))
    t = add(t, mm(d2, mx(t)))
    yield
    d8 = mx(mm(d4, d4))
    t = add(t, mm(d4, mx(t)))
    yield
    t_d = add(t, mm(d8, mx(t)))
    t_dm = mx(t_d)
    yield
    m1 = mm(t_dm, o)
    m1m = mx(m1)
    yield
    m2 = mm(m1m, m1m)
    yield
    m3 = mm(m1m, mx(m2))
    w = mx([x + y + z for x, y, z in zip(m1, m2, m3)])
    yield
    return add(t_d, mm(w, t_dm))


def _rwkv_masks():
    lane = lax.broadcasted_iota(jnp.int32, (1, LANES), 1)
    row = lax.broadcasted_iota(jnp.int32, (LANES, LANES), 0)
    col = lax.broadcasted_iota(jnp.int32, (LANES, LANES), 1)
    same = (row >= HEAD) == (col >= HEAD)
    rs = row & (HEAD - 1)
    cs = col & (HEAD - 1)
    eye = row == col
    return dict(lane_lo=lane < HEAD, strict=same & (rs > cs), incl=same & (rs >= cs),
                blk=(row >> 4) == (col >> 4), eye=eye, eye_f=jnp.where(eye, 1.0, 0.0))


def _rwkv_head_sum(x, ones_blk):
    cols = [_dot_split_lhs(x[:, t * LANES:(t + 1) * LANES], ones_blk, 2)
            for t in range(x.shape[1] // LANES)]
    return jnp.concatenate(cols, axis=1)


def _rwkv_prepare(srcs, bufs, mus, prm, masks, first):
    cl = RWKV_CHUNK
    w0_ref, w2_ref, a0_ref, a2_ref, g2_ref, kk_ref, ka_ref, rk_ref, ones_ref = prm
    vals = []
    for src, buf, mu_ref in zip(srcs, bufs, mus):
        if first:
            buf[0:SUBLANES, :] = jnp.zeros((SUBLANES, buf.shape[1]), F32)
        else:
            buf[0:SUBLANES, :] = buf[cl:cl + SUBLANES, :]
        cur = src[...]
        buf[SUBLANES:SUBLANES + cl, :] = cur
        prev = buf[SUBLANES - 1:SUBLANES - 1 + cl, :]
        vals.append(cur + (prev - cur) * mu_ref[...])
    r, k, v, pg, misc = vals
    ones_blk = ones_ref[...]
    yield
    w_log = -_softplus(-(w0_ref[...] + _dot(_mx(jnp.tanh(misc)), w2_ref[...]))) - 0.5
    lw = -jnp.exp(w_log)
    iclr = _sigmoid(a0_ref[...] + _dot(_mx(misc), a2_ref[...]))
    gate = _dot(_mx(_sigmoid(pg)), g2_ref[...])
    yield

    row_c = lax.broadcasted_iota(jnp.int32, (cl, cl), 0)
    col_c = lax.broadcasted_iota(jnp.int32, (cl, cl), 1)
    tril_c = jnp.where(row_c >= col_c, 1.0, 0.0).astype(MXU_DTYPE)
    lane_lo = masks["lane_lo"]
    stack2 = lambda x: jnp.concatenate([jnp.where(lane_lo, x, 0.0), jnp.where(lane_lo, 0.0, x)], axis=0)
    out = dict(a2=[], r2=[], b2=[], k2=[], v2=[], bkp=[])
    p_ends, bonuses = [], []
    for p in range(r.shape[1] // LANES):
        ps = slice(p * LANES, (p + 1) * LANES)
        r_p, k_p, v_p, lw_p, iclr_p = r[:, ps], k[:, ps], v[:, ps], lw[:, ps], iclr[:, ps]
        kk = k_p * kk_ref[:, ps]
        kk = kk / jnp.maximum(jnp.sqrt(_dot_split_lhs(kk * kk, ones_blk, 2)), 1e-12)
        k_p = k_p * (1.0 + (iclr_p - 1.0) * ka_ref[:, ps])
        bonuses.append(_dot_split_lhs(r_p * k_p * rk_ref[:, ps], ones_blk, 2) * v_p)
        cum = _dot_split_rhs(tril_c, lw_p, 3)
        p_in = jnp.exp(cum)
        p_inv = jnp.exp(-cum)
        a_t = -(kk * jnp.exp(cum - lw_p))
        b_t = kk * iclr_p * p_inv
        k_t = k_p * p_inv
        p_end = p_in[cl - 1:cl, :]
        out["a2"].append(_mx(stack2(a_t)))
        out["r2"].append(_mx(stack2(r_p * p_in)))
        out["b2"].append(_mx(stack2(b_t)))
        out["k2"].append(_mx(stack2(k_t)))
        out["v2"].append(_mx(stack2(v_p)))
        out["bkp"].append(_mx(jnp.concatenate([stack2(b_t * p_end), stack2(k_t * p_end)], axis=0).T))
        p_ends.append(p_end)
        yield
    out.update(p_end=jnp.concatenate(p_ends, axis=1), bonus=jnp.concatenate(bonuses, axis=1), gate=gate)
    return out


def _rwkv_scan(ops, st_ref, ybuf, lnw_ref, lnb_ref, ones_ref, masks):
    cl = RWKV_CHUNK
    a2, r2, b2, k2, v2, bkp_t = ops["a2"], ops["r2"], ops["b2"], ops["k2"], ops["v2"], ops["bkp"]
    p_end = ops["p_end"]
    tiles = range(len(a2))
    cat0 = lambda *xs: jnp.concatenate(xs, axis=0)
    cat1 = lambda *xs: jnp.concatenate(xs, axis=1)
    gram = [_dot_nt(cat0(a2[p], r2[p]), cat0(b2[p], k2[p])) for p in tiles]
    yield
    a_ab = [jnp.where(masks["strict"], g[0:LANES, 0:LANES], 0.0) for g in gram]
    a_ak = [_mx(jnp.where(masks["strict"], g[0:LANES, LANES:], 0.0)) for g in gram]
    a_rb = [_mx(jnp.where(masks["incl"], g[LANES:, 0:LANES], 0.0)) for g in gram]
    a_rk = [_mx(jnp.where(masks["incl"], g[LANES:, LANES:], 0.0)) for g in gram]
    t_inv = yield from _unit_lower_inverse(a_ab, masks["blk"], masks["eye_f"])
    st = [st_ref[p] for p in tiles]
    st_m = [_mx(s) for s in st]
    y0 = [_dot(cat1(a2[p], a_ak[p]), cat0(st_m[p], v2[p])) for p in tiles]
    yield
    sa = [_mx(_dot(_mx(t_inv[p]), _mx(y0[p]))) for p in tiles]
    yield
    o2 = [_dot(cat1(r2[p], a_rb[p], a_rk[p]), cat0(st_m[p], sa[p], v2[p])) for p in tiles]
    yield
    upd = [_dot(bkp_t[p], cat0(sa[p], v2[p])) for p in tiles]
    for p in tiles:
        ps = slice(p * LANES, (p + 1) * LANES)
        ybuf[:, ps] = o2[p][0:cl, :] + o2[p][cl:, :]
        p_col = jnp.sum(jnp.where(masks["eye"], p_end[:, ps], 0.0), axis=1, keepdims=True)
        st_ref[p] = st[p] * p_col + upd[p]
    yield
    y = ybuf[...]
    ones_blk = ones_ref[...]
    inv_n = 1.0 / HEAD
    mean = _rwkv_head_sum(y, ones_blk) * inv_n
    d = y - mean
    var = _rwkv_head_sum(d * d, ones_blk) * inv_n
    y = d * lax.rsqrt(var + RWKV_LN_EPS) * lnw_ref[...] + lnb_ref[...]
    return (y + ops["bonus"]) * ops["gate"]


def _run(gen):
    while True:
        try:
            next(gen)
        except StopIteration as stop:
            return stop.value


def _interleave(scan_gen, prep_gen, scan_per_prep=1, lead=0):
    results, done = [None, None], [False, False]
    gens = (scan_gen, prep_gen)

    def step(i):
        if not done[i]:
            try:
                next(gens[i])
            except StopIteration as stop:
                results[i], done[i] = stop.value, True

    for _ in range(lead):
        step(0)
    while not all(done):
        step(1)
        for _ in range(scan_per_prep):
            step(0)
    return results


_RWKV_STAGED = ("a2", "r2", "b2", "k2", "v2", "bkp")


def _rwkv_body(*refs):
    cl = RWKV_CHUNK
    first_srcs, odd_srcs, next_srcs = refs[0:5], refs[5:10], refs[10:15]
    mus = refs[15:20]
    prm = refs[20:28] + (refs[30],)
    lnw_ref, lnb_ref, ones_ref = refs[28], refs[29], refs[30]
    o_ref = refs[31]
    bufs = refs[32:37]
    st_ref, ybuf = refs[37], refs[38]
    staged = dict(zip(_RWKV_STAGED, refs[39:45]))
    s_pend, s_bonus, s_gate = refs[45], refs[46], refs[47]
    masks = _rwkv_masks()
    n_tiles = st_ref.shape[0]

    def stash(ops):
        for name in _RWKV_STAGED:
            for p in range(n_tiles):
                staged[name][p] = ops[name][p]
        s_pend[...] = ops["p_end"]
        s_bonus[...] = ops["bonus"]
        s_gate[...] = ops["gate"]

    @pl.when(pl.program_id(1) == 0)
    def _():
        st_ref[...] = jnp.zeros(st_ref.shape, F32)
        stash(_run(_rwkv_prepare(first_srcs, bufs, mus, prm, masks, first=True)))

    scan = lambda ops: _rwkv_scan(ops, st_ref, ybuf, lnw_ref, lnb_ref, ones_ref, masks)
    even = {name: [staged[name][p] for p in range(n_tiles)] for name in _RWKV_STAGED}
    even.update(p_end=s_pend[...], bonus=s_bonus[...], gate=s_gate[...])
    y_even, odd = _interleave(scan(even), _rwkv_prepare(odd_srcs, bufs, mus, prm, masks, first=False))
    o_ref[0:cl, :] = y_even.astype(o_ref.dtype)
    y_odd, nxt = _interleave(scan(odd), _rwkv_prepare(next_srcs, bufs, mus, prm, masks, first=False))
    o_ref[cl:2 * cl, :] = y_odd.astype(o_ref.dtype)
    stash(nxt)


def _rwkv(u, bsz, seq, mu, w0, w2, a0, a2, g2, k_k, k_a, r_k, ln_w, ln_b):
    cl = RWKV_CHUNK
    nc = seq // cl
    assert nc % 2 == 0
    w = W_MIX
    mw = 2 * LANES
    n_tiles = w // LANES
    row1 = lambda vec: vec.reshape(1, -1)
    o_pw = 3 * w
    o_pa = o_pw + RWKV_DECAY_RANK
    o_pg = o_pa + RWKV_AAA_RANK
    mu_m = jnp.zeros((mw,), F32)
    mu_m = mu_m.at[MISC_PW:MISC_PW + RWKV_DECAY_RANK].set(mu[o_pw:o_pa])
    mu_m = mu_m.at[MISC_PA:MISC_PA + RWKV_AAA_RANK].set(mu[o_pa:o_pg])
    w2e = jnp.zeros((mw, w), F32).at[MISC_PW:MISC_PW + RWKV_DECAY_RANK].set(w2)
    a2e = jnp.zeros((mw, w), F32).at[MISC_PA:MISC_PA + RWKV_AAA_RANK].set(a2)
    ones_blk = np.kron(np.eye(LANES // HEAD, dtype=np.float32), np.ones((HEAD, HEAD), np.float32))
    widths = (w, w, w, mw, mw)
    colblk = (COL_R // w, COL_K // w, COL_V // w, COL_PG // mw, COL_MISC // mw)

    def chunk_specs(chunk_of):
        return [pl.BlockSpec((cl, wd), (lambda b, j, cb=cb: (b * nc + chunk_of(j), cb)))
                for wd, cb in zip(widths, colblk)]

    cvec = lambda width: pl.BlockSpec((1, width), lambda b, j: (0, 0))
    cmat = lambda rows, width: pl.BlockSpec((rows, width), lambda b, j: (0, 0))
    stage = lambda cols: pltpu.VMEM((n_tiles, LANES, cols), MXU_DTYPE)
    return pl.pallas_call(
        _rwkv_body,
        grid=(bsz, nc // 2),
        in_specs=(chunk_specs(lambda j: 0) + chunk_specs(lambda j: 2 * j + 1)
                  + chunk_specs(lambda j: jnp.minimum(2 * j + 2, nc - 1))
                  + [cvec(w), cvec(w), cvec(w), cvec(mw), cvec(mw),
                     cvec(w), cmat(mw, w), cvec(w), cmat(mw, w), cmat(RWKV_GATE_RANK, w),
                     cvec(w), cvec(w), cvec(w), cvec(w), cvec(w), cmat(LANES, LANES)]),
        out_specs=pl.BlockSpec((2 * cl, w), lambda b, j: (b * (nc // 2) + j, 0)),
        out_shape=jax.ShapeDtypeStruct((bsz * seq, w), MXU_DTYPE),
        scratch_shapes=[pltpu.VMEM((cl + SUBLANES, w), F32), pltpu.VMEM((cl + SUBLANES, w), F32),
                        pltpu.VMEM((cl + SUBLANES, w), F32), pltpu.VMEM((cl + SUBLANES, mw), F32),
                        pltpu.VMEM((cl + SUBLANES, mw), F32),
                        pltpu.VMEM((n_tiles, LANES, LANES), F32),
                        pltpu.VMEM((cl, w), F32),
                        stage(LANES), stage(LANES), stage(LANES), stage(LANES), stage(LANES),
                        stage(2 * LANES),
                        pltpu.VMEM((1, w), F32), pltpu.VMEM((cl, w), F32), pltpu.VMEM((cl, w), F32)],
        compiler_params=_cparams(("arbitrary", "arbitrary")),
        name="rwkv7_mixer",
    )(*([u] * 15),
      row1(mu[0:w]), row1(mu[w:2 * w]), row1(mu[2 * w:3 * w]), row1(mu[o_pg:o_pg + RWKV_GATE_RANK]),
      row1(mu_m), row1(w0), _mx(w2e), row1(a0), _mx(a2e), _mx(g2), row1(k_k), row1(k_a),
      row1(r_k), row1(ln_w), row1(ln_b), jnp.asarray(ones_blk, MXU_DTYPE))


def _attn_body(q_ref, k_ref, v_ref, o_ref):
    d = q_ref.shape[1]
    hd = d // XATTN_HEADS
    scale = hd ** -0.5
    for h in range(XATTN_HEADS):
        hs = slice(h * hd, (h + 1) * hd)
        s = _dot_nt(q_ref[:, hs], k_ref[0, :, hs]) * scale
        s = s - jnp.max(s, axis=-1, keepdims=True)
        e = jnp.exp(s)
        p = e / jnp.sum(e, axis=-1, keepdims=True)
        o_ref[:, hs] = _dot(_mx(p), v_ref[0, :, hs]).astype(o_ref.dtype)


def _attention(q, k, v, bsz, seq, ts=512):
    d = q.shape[1]
    n_mem = k.shape[1]
    ts = min(ts, seq)
    ns = seq // ts
    return pl.pallas_call(
        _attn_body,
        grid=(bsz, ns),
        in_specs=[pl.BlockSpec((ts, d), lambda b, s: (b * ns + s, 0)),
                  pl.BlockSpec((1, n_mem, d), lambda b, s: (b, 0, 0)),
                  pl.BlockSpec((1, n_mem, d), lambda b, s: (b, 0, 0))],
        out_specs=pl.BlockSpec((ts, d), lambda b, s: (b * ns + s, 0)),
        out_shape=jax.ShapeDtypeStruct((bsz * seq, d), MXU_DTYPE),
        compiler_params=_cparams(("arbitrary", "arbitrary")),
        name="mem_attention",
    )(q, k, v)


def _in_proj_columns(d_in):
    w = W_MIX
    bcw = 2 * SSD_GROUPS * SSD_STATE
    heads = w // HEAD
    ssd_in = 2 * w + bcw + heads
    idx = np.full((N_U,), -1, np.int64)
    idx[COL_Z:COL_Z + w] = np.arange(0, w)
    idx[COL_XS:COL_XS + w] = np.arange(w, 2 * w)
    idx[COL_BC:COL_BC + bcw] = np.arange(2 * w, 2 * w + bcw)
    idx[COL_MISC + MISC_DT:COL_MISC + MISC_DT + heads] = np.arange(2 * w + bcw, ssd_in)
    rw = ssd_in
    idx[COL_R:COL_R + w] = rw + np.arange(0, w)
    idx[COL_K:COL_K + w] = rw + np.arange(w, 2 * w)
    idx[COL_V:COL_V + w] = rw + np.arange(2 * w, 3 * w)
    o = 3 * w
    idx[COL_MISC + MISC_PW:COL_MISC + MISC_PW + RWKV_DECAY_RANK] = rw + o + np.arange(RWKV_DECAY_RANK)
    o += RWKV_DECAY_RANK
    idx[COL_MISC + MISC_PA:COL_MISC + MISC_PA + RWKV_AAA_RANK] = rw + o + np.arange(RWKV_AAA_RANK)
    o += RWKV_AAA_RANK
    idx[COL_PG:COL_PG + RWKV_GATE_RANK] = rw + o + np.arange(RWKV_GATE_RANK)
    assert rw + o + RWKV_GATE_RANK == d_in
    return idx


def _permute_in_proj(w_in):
    idx = _in_proj_columns(w_in.shape[1])
    pieces, start = [], 0
    while start < N_U:
        stop = start + 1
        if idx[start] < 0:
            while stop < N_U and idx[stop] < 0:
                stop += 1
            pieces.append(jnp.zeros((w_in.shape[0], stop - start), w_in.dtype))
        else:
            while stop < N_U and idx[stop] == idx[stop - 1] + 1:
                stop += 1
            pieces.append(w_in[:, idx[start]:idx[stop - 1] + 1])
        start = stop
    return jnp.concatenate(pieces, axis=1)


def kernel(x, mem, norm_mix_g, w_in, ssd_conv_w, ssd_conv_b, ssd_dt_bias, ssd_a_log, ssd_d, ssd_norm_g, rwkv_mu, rwkv_w0, rwkv_w2, rwkv_a0, rwkv_a2, rwkv_g2, rwkv_k_k, rwkv_k_a, rwkv_r_k, rwkv_ln_w, rwkv_ln_b, w_out, norm_x_g, norm_mem_g, xattn_wq, xattn_wk, xattn_wv, xattn_wo, norm_ffn_g, ffn_w1, ffn_w2, final_norm_g):
    bsz, seq, d = x.shape
    n_mem = mem.shape[1]
    xr = x.reshape(bsz * seq, d)
    memr = mem.reshape(bsz * n_mem, d)
    n_layers = w_in.shape[0]
    h = _rmsnorm(xr, norm_mix_g[0], MXU_DTYPE)
    for l in range(n_layers):
        u = _matmul_resident(h, _permute_in_proj(w_in[l]), F32, name="in_proj")
        y_ssd = _ssd(u, bsz, seq, ssd_conv_w[l], ssd_conv_b[l], ssd_dt_bias[l], ssd_a_log[l],
                     ssd_d[l], ssd_norm_g[l])
        y_rwkv = _rwkv(u, bsz, seq, rwkv_mu[l], rwkv_w0[l], rwkv_w2[l], rwkv_a0[l], rwkv_a2[l],
                       rwkv_g2[l], rwkv_k_k[l], rwkv_k_a[l], rwkv_r_k[l].reshape(-1),
                       rwkv_ln_w[l], rwkv_ln_b[l])
        wo = _mx(w_out[l])
        xr, h = _matmul_rows([(y_ssd, wo[:W_MIX]), (y_rwkv, wo[W_MIX:])], xr, norm_x_g[l],
                             MXU_DTYPE, name="out_proj")

        m = _rmsnorm(memr, norm_mem_g[l], MXU_DTYPE)
        q = _matmul_resident(h, xattn_wq[l], MXU_DTYPE, name="q_proj")
        kx = _matmul_resident(m, xattn_wk[l], MXU_DTYPE, name="k_proj")
        vx = _matmul_resident(m, xattn_wv[l], MXU_DTYPE, name="v_proj")
        o = _attention(q, kx.reshape(bsz, n_mem, d), vx.reshape(bsz, n_mem, d), bsz, seq)
        xr, h = _matmul_rows([(o, _mx(xattn_wo[l]))], xr, norm_ffn_g[l], MXU_DTYPE, name="o_proj")

        hid = _matmul_resident(h, ffn_w1[l], MXU_DTYPE, act="relu2", name="ffn_up")
        down = [(hid, _mx(ffn_w2[l]))]
        if l + 1 < n_layers:
            xr, h = _matmul_rows(down, xr, norm_mix_g[l + 1], MXU_DTYPE, name="ffn_down")
        else:
            out = _matmul_rows(down, xr, final_norm_g, x.dtype, emit_x=False, name="ffn_down")
    return out.reshape(bsz, seq, d)
```

```python
import functools

import numpy as np
import jax
import jax.numpy as jnp
from jax import lax
from jax.experimental import pallas as pl
from jax.experimental.pallas import tpu as pltpu

F32 = jnp.float32
MXU_DTYPE = jnp.bfloat16

NORM_EPS = 1e-6
RWKV_LN_EPS = 64e-5

HEAD = 64
SSD_STATE = 128
SSD_GROUPS = 2
SSD_CHUNK = 128
SSD_CONV = 4
RWKV_CHUNK = 64
RWKV_DECAY_RANK = 96
RWKV_AAA_RANK = 96
RWKV_GATE_RANK = 256
XATTN_HEADS = 4

LANES = 128
SUBLANES = 8
VMEM_LIMIT = 56 * 1024 * 1024

W_MIX = 1024
SSD_BC = 2 * SSD_GROUPS * SSD_STATE
U1_COLS = 2 * W_MIX + SSD_BC
U2_R = W_MIX // HEAD
U2_MISC = 3 * W_MIX
RWKV_MISC = 512
U2_COLS = U2_MISC + RWKV_MISC
MISC_PW = 0
MISC_PA = MISC_PW + RWKV_DECAY_RANK
MISC_PG = MISC_PA + RWKV_AAA_RANK
MISC_END = MISC_PG + RWKV_GATE_RANK
SSD_DT_LANE = 0


def _mx(a):
    return a.astype(MXU_DTYPE)


def _dot(a, b):
    return jnp.dot(a, b, preferred_element_type=F32)


def _dot_nt(a, b):
    return lax.dot_general(a, b, (((1,), (1,)), ((), ())), preferred_element_type=F32)


def _split(v, parts):
    out = []
    rem = v
    for _ in range(parts):
        p = rem.astype(MXU_DTYPE)
        out.append(p)
        rem = rem - p.astype(F32)
    return out


def _dot_split_rhs(a01, v, parts):
    acc = None
    for p in _split(v, parts):
        t = _dot(a01, p)
        acc = t if acc is None else acc + t
    return acc


def _dot_split_lhs(v, b01, parts):
    acc = None
    for p in _split(v, parts):
        t = _dot(p, b01)
        acc = t if acc is None else acc + t
    return acc


def _sigmoid(x):
    return 1.0 / (1.0 + jnp.exp(-x))


def _softplus(x):
    return jnp.maximum(x, 0.0) + jnp.log(1.0 + jnp.exp(-jnp.abs(x)))


def _cparams(sem):
    return pltpu.CompilerParams(dimension_semantics=sem, vmem_limit_bytes=VMEM_LIMIT)


def _rmsnorm_body(x_ref, g_ref, o_ref):
    x = x_ref[...]
    ms = jnp.mean(x * x, axis=-1, keepdims=True)
    o_ref[...] = (x * lax.rsqrt(ms + NORM_EPS) * g_ref[...]).astype(o_ref.dtype)


def _rmsnorm(x, g, out_dtype, tm=512):
    m, d = x.shape
    tm = min(tm, m)
    return pl.pallas_call(
        _rmsnorm_body,
        grid=(m // tm,),
        in_specs=[pl.BlockSpec((tm, d), lambda i: (i, 0)),
                  pl.BlockSpec((1, d), lambda i: (0, 0))],
        out_specs=pl.BlockSpec((tm, d), lambda i: (i, 0)),
        out_shape=jax.ShapeDtypeStruct((m, d), out_dtype),
        compiler_params=_cparams(("arbitrary",)),
        name="rmsnorm",
    )(x, g.reshape(1, d))


def _mm_resident_body(a_ref, b_ref, o_ref, bm_ref, *, act):
    @pl.when(pl.program_id(1) == 0)
    def _():
        bm_ref[...] = b_ref[...].astype(MXU_DTYPE)

    r = _dot(a_ref[...], bm_ref[...])
    if act == "relu2":
        r = jnp.square(jnp.maximum(r, 0.0))
    o_ref[...] = r.astype(o_ref.dtype)


def _matmul_resident(a, b, out_dtype, act=None, n=None, tm=1024, tn=1024, name="matmul_resident"):
    m, kdim = a.shape
    n = b.shape[1] if n is None else n
    tm, tn = min(tm, m), min(tn, n)
    return pl.pallas_call(
        functools.partial(_mm_resident_body, act=act),
        grid=(n // tn, m // tm),
        in_specs=[pl.BlockSpec((tm, kdim), lambda j, i: (i, 0)),
                  pl.BlockSpec((kdim, tn), lambda j, i: (0, j))],
        out_specs=pl.BlockSpec((tm, tn), lambda j, i: (i, j)),
        out_shape=jax.ShapeDtypeStruct((m, n), out_dtype),
        scratch_shapes=[pltpu.VMEM((kdim, tn), MXU_DTYPE)],
        compiler_params=_cparams(("arbitrary", "arbitrary")),
        name=name,
    )(a, b)


def _mm_rows_body(*refs, n_pairs, nk, emit_x):
    ab = refs[:2 * n_pairs]
    res_ref, g_ref = refs[2 * n_pairs:2 * n_pairs + 2]
    outs = refs[2 * n_pairs + 2:]
    x_ref = outs[0] if emit_x else None
    h_ref = outs[int(emit_x)]
    acc_ref = outs[int(emit_x) + 1] if nk > 1 else None
    part = None
    for p in range(n_pairs):
        t = _dot(ab[2 * p][...], ab[2 * p + 1][...])
        part = t if part is None else part + t

    def finish(acc):
        x = res_ref[...] + acc
        if emit_x:
            x_ref[...] = x
        ms = jnp.mean(x * x, axis=-1, keepdims=True)
        h_ref[...] = (x * lax.rsqrt(ms + NORM_EPS) * g_ref[...]).astype(h_ref.dtype)

    if nk == 1:
        finish(part)
        return

    k = pl.program_id(1)

    @pl.when(k == 0)
    def _():
        acc_ref[...] = part

    @pl.when(k > 0)
    def _():
        acc_ref[...] += part

    @pl.when(k == nk - 1)
    def _():
        finish(acc_ref[...])


def _matmul_rows(pairs, res, g, h_dtype, emit_x=True, tm=512, tk=2048, name="matmul_rows"):
    m, n = res.shape
    tm = min(tm, m)
    kdims = [a.shape[1] for a, _, _ in pairs]
    tk = min(tk, min(kdims))
    nk = kdims[0] // tk
    assert all(kd == nk * tk for kd in kdims) and all(row0 % tk == 0 for _, _, row0 in pairs)
    in_specs, args = [], []
    for a, b, row0 in pairs:
        in_specs += [pl.BlockSpec((tm, tk), lambda i, k: (i, k)),
                     pl.BlockSpec((tk, n), lambda i, k, blk0=row0 // tk: (blk0 + k, 0))]
        args += [a, b]
    row_spec = pl.BlockSpec((tm, n), lambda i, k: (i, 0))
    in_specs += [row_spec, pl.BlockSpec((1, n), lambda i, k: (0, 0))]
    out_specs, out_shape = [row_spec], [jax.ShapeDtypeStruct((m, n), h_dtype)]
    if emit_x:
        out_specs, out_shape = [row_spec] + out_specs, [jax.ShapeDtypeStruct((m, n), F32)] + out_shape
    outs = pl.pallas_call(
        functools.partial(_mm_rows_body, n_pairs=len(pairs), nk=nk, emit_x=emit_x),
        grid=(m // tm, nk),
        in_specs=in_specs,
        out_specs=out_specs,
        out_shape=out_shape,
        scratch_shapes=[pltpu.VMEM((tm, n), F32)] if nk > 1 else [],
        compiler_params=_cparams(("arbitrary", "arbitrary")),
        name=name,
    )(*args, res, g.reshape(1, n))
    return tuple(outs) if emit_x else outs[0]


def _ssd_body(z_ref, xs_ref, bc_ref, m_ref, cwx_ref, cbx_ref, cwbc_ref, cbbc_ref, dtb_ref,
              alog_ref, dvec_ref, ng_ref, esel_ref, o_ref, xbuf, bcbuf, st_ref):
    q = SSD_CHUNK
    n = SSD_STATE
    gw = st_ref.shape[2]
    c = pl.program_id(1)

    @pl.when(c == 0)
    def _():
        xbuf[0:SUBLANES, :] = jnp.zeros((SUBLANES, xbuf.shape[1]), F32)
        bcbuf[0:SUBLANES, :] = jnp.zeros((SUBLANES, bcbuf.shape[1]), F32)
        st_ref[...] = jnp.zeros(st_ref.shape, F32)

    @pl.when(c > 0)
    def _():
        xbuf[0:SUBLANES, :] = xbuf[q:q + SUBLANES, :]
        bcbuf[0:SUBLANES, :] = bcbuf[q:q + SUBLANES, :]

    xbuf[SUBLANES:SUBLANES + q, :] = xs_ref[...]
    bcbuf[SUBLANES:SUBLANES + q, :] = bc_ref[...]

    def conv_silu(buf, w_ref, b_ref):
        acc = None
        for k in range(SSD_CONV):
            off = SUBLANES - (SSD_CONV - 1) + k
            t = buf[off:off + q, :] * w_ref[k:k + 1, :]
            acc = t if acc is None else acc + t
        acc = acc + b_ref[...]
        return acc * _sigmoid(acc)

    xc = conv_silu(xbuf, cwx_ref, cbx_ref)
    bcc = conv_silu(bcbuf, cwbc_ref, cbbc_ref)
    g_n = SSD_GROUPS * n

    lane = lax.broadcasted_iota(jnp.int32, (1, LANES), 1)
    n_heads = SSD_GROUPS * gw // HEAD
    dmask = (lane >= SSD_DT_LANE) & (lane < SSD_DT_LANE + n_heads)
    dt = jnp.where(dmask, _softplus(m_ref[...] + dtb_ref[...]), 0.0)
    a = jnp.where(dmask, -jnp.exp(alog_ref[...]), 0.0)
    da = dt * a
    row = lax.broadcasted_iota(jnp.int32, (q, q), 0)
    col = lax.broadcasted_iota(jnp.int32, (q, q), 1)
    tri = row >= col
    tril = jnp.where(tri, 1.0, 0.0).astype(MXU_DTYPE)
    cs = _dot_split_rhs(tril, da, 3)
    ecs = jnp.exp(cs)
    dte = jnp.exp(cs[q - 1:q, :] - cs)
    esel = esel_ref[...]
    dt_e = _dot_split_lhs(dt, esel, 2)
    ecs_e = _dot_split_lhs(ecs, esel, 2)
    dte_e = _dot_split_lhs(dte, esel, 2)
    cs_t = cs.T

    xdt = xc * dt_e
    xdt_m = _mx(xdt)
    xd_m = _mx(xdt * dte_e)
    lane_lo = lane < HEAD

    y_cols = []
    for g in range(SSD_GROUPS):
        bg = bcc[:, g * n:(g + 1) * n]
        cg = _mx(bcc[:, g_n + g * n:g_n + (g + 1) * n])
        cb = _dot_nt(cg, _mx(bg))
        gs = slice(g * gw, (g + 1) * gw)
        st = st_ref[g]
        y_off = _dot(cg, _mx(st)) * ecs_e[:, gs]
        st_ref[g] = st * ecs_e[q - 1:q, gs] + _dot(_mx(bg.T), xd_m[:, gs])
        for pr in range(gw // LANES):
            h0 = (g * gw + pr * LANES) // HEAD
            ps = slice(g * gw + pr * LANES, g * gw + (pr + 1) * LANES)
            res = []
            for hh in (h0, h0 + 1):
                li = SSD_DT_LANE + hh
                seg = cs[:, li:li + 1] - cs_t[li:li + 1, :]
                lm = jnp.where(tri, jnp.exp(jnp.where(tri, seg, 0.0)), 0.0)
                res.append(_dot(_mx(cb * lm), xdt_m[:, ps]))
            y_diag = jnp.where(lane_lo, res[0], res[1])
            y_cols.append(y_diag + y_off[:, pr * LANES:(pr + 1) * LANES])
    y = jnp.concatenate(y_cols, axis=1) + xc * dvec_ref[...]
    zz = z_ref[...]
    y = y * (zz * _sigmoid(zz))
    outs = []
    for g in range(SSD_GROUPS):
        yg = y[:, g * gw:(g + 1) * gw]
        ms = jnp.mean(yg * yg, axis=-1, keepdims=True)
        outs.append(yg * lax.rsqrt(ms + NORM_EPS))
    y = jnp.concatenate(outs, axis=1) * ng_ref[...]
    o_ref[...] = y.astype(o_ref.dtype)


def _ssd(u1, u2, bsz, seq, conv_w, conv_b, dt_bias, a_log, d_skip, norm_g):
    q = SSD_CHUNK
    nc = seq // q
    w = W_MIX
    gw = w // SSD_GROUPS
    heads = w // HEAD
    bcw = SSD_BC
    cw = conv_w[:, 0, :]
    pad = lambda v: jnp.zeros((1, LANES), F32).at[0, SSD_DT_LANE:SSD_DT_LANE + heads].set(v)
    esel = np.zeros((LANES, w), np.float32)
    for h in range(heads):
        esel[SSD_DT_LANE + h, h * HEAD:(h + 1) * HEAD] = 1.0
    rowblk = lambda cb: (lambda b, c: (b * nc + c, cb))
    const = lambda b, c: (0, 0)
    return pl.pallas_call(
        _ssd_body,
        grid=(bsz, nc),
        in_specs=[pl.BlockSpec((q, w), rowblk(0)),
                  pl.BlockSpec((q, w), rowblk(1)),
                  pl.BlockSpec((q, bcw), rowblk(2 * w // bcw)),
                  pl.BlockSpec((q, LANES), rowblk(0)),
                  pl.BlockSpec((SSD_CONV, w), const), pl.BlockSpec((1, w), const),
                  pl.BlockSpec((SSD_CONV, bcw), const), pl.BlockSpec((1, bcw), const),
                  pl.BlockSpec((1, LANES), const), pl.BlockSpec((1, LANES), const),
                  pl.BlockSpec((1, w), const), pl.BlockSpec((1, w), const),
                  pl.BlockSpec((LANES, w), const)],
        out_specs=pl.BlockSpec((q, w), lambda b, c: (b * nc + c, 0)),
        out_shape=jax.ShapeDtypeStruct((bsz * seq, w), MXU_DTYPE),
        scratch_shapes=[pltpu.VMEM((q + SUBLANES, w), F32),
                        pltpu.VMEM((q + SUBLANES, bcw), F32),
                        pltpu.VMEM((SSD_GROUPS, SSD_STATE, gw), F32)],
        compiler_params=_cparams(("arbitrary", "arbitrary")),
        name="ssd_mixer",
    )(u1, u1, u1, u2, cw[:, :w], conv_b[:w].reshape(1, w), cw[:, w:], conv_b[w:].reshape(1, bcw),
      pad(dt_bias), pad(a_log), jnp.repeat(d_skip, HEAD).reshape(1, w), norm_g.reshape(1, w),
      jnp.asarray(esel, MXU_DTYPE))


def _unit_lower_inverse(l_mats, blk_mask, eye_f):
    mm = lambda xs, ys: [_dot(x, y) for x, y in zip(xs, ys)]
    mx = lambda xs: [_mx(x) for x in xs]
    add = lambda xs, ys: [x + y for x, y in zip(xs, ys)]
    d = [jnp.where(blk_mask, l, 0.0) for l in l_mats]
    o = mx([l - x for l, x in zip(l_mats, d)])
    d1 = mx(d)
    d2 = mx(mm(d1, d1))
    t = [eye_f + x for x in d]
    yield
    d4 = mx(mm(d2, d2))
    t = add(t, mm(d2, mx(t)))
    yield
    d8 = mx(mm(d4, d4))
    t = add(t, mm(d4, mx(t)))
    yield
    t_d = add(t, mm(d8, mx(t)))
    t_dm = mx(t_d)
    yield
    m1 = mm(t_dm, o)
    m1m = mx(m1)
    yield
    m2 = mm(m1m, m1m)
    yield
    m3 = mm(m1m, mx(m2))
    w = mx([x + y + z for x, y, z in zip(m1, m2, m3)])
    yield
    return add(t_d, mm(w, t_dm))


def _rwkv_masks():
    lane = lax.broadcasted_iota(jnp.int32, (1, LANES), 1)
    row = lax.broadcasted_iota(jnp.int32, (LANES, LANES), 0)
    col = lax.broadcasted_iota(jnp.int32, (LANES, LANES), 1)
    same = (row >= HEAD) == (col >= HEAD)
    rs = row & (HEAD - 1)
    cs = col & (HEAD - 1)
    eye = row == col
    return dict(lane_lo=lane < HEAD, strict=same & (rs > cs), incl=same & (rs >= cs),
                blk=(row >> 4) == (col >> 4), eye=eye, eye_f=jnp.where(eye, 1.0, 0.0))


def _rwkv_head_sum(x, ones_blk):
    cols = [_dot_split_lhs(x[:, t * LANES:(t + 1) * LANES], ones_blk, 2)
            for t in range(x.shape[1] // LANES)]
    return jnp.concatenate(cols, axis=1)


def _rwkv_prepare(src, buf, mu_ref, prm, masks, first):
    cl = RWKV_CHUNK
    w0_ref, w2_ref, a0_ref, a2_ref, g2_ref, kk_ref, ka_ref, rk_ref, ones_ref = prm
    if first:
        buf[0:SUBLANES, :] = jnp.zeros((SUBLANES, buf.shape[1]), F32)
    else:
        buf[0:SUBLANES, :] = buf[cl:cl + SUBLANES, :]
    cur = src[...]
    buf[SUBLANES:SUBLANES + cl, :] = cur
    prev = buf[SUBLANES - 1:SUBLANES - 1 + cl, :]
    row = cur + (prev - cur) * mu_ref[...]
    row = pltpu.roll(row, row.shape[1] - U2_R, 1)
    w = W_MIX
    r, k, v = (row[:, i * w:(i + 1) * w] for i in range(3))
    misc = row[:, U2_MISC:]
    pw_pa = misc[:, 0:2 * LANES]
    ones_blk = ones_ref[...]
    yield
    w_log = -_softplus(-(w0_ref[...] + _dot(_mx(jnp.tanh(pw_pa)), w2_ref[...]))) - 0.5
    lw = -jnp.exp(w_log)
    iclr = _sigmoid(a0_ref[...] + _dot(_mx(pw_pa), a2_ref[...]))
    gate = _dot(_mx(_sigmoid(misc[:, LANES:])), g2_ref[...])
    yield

    row_c = lax.broadcasted_iota(jnp.int32, (cl, cl), 0)
    col_c = lax.broadcasted_iota(jnp.int32, (cl, cl), 1)
    tril_c = jnp.where(row_c >= col_c, 1.0, 0.0).astype(MXU_DTYPE)
    lane_lo = masks["lane_lo"]
    stack2 = lambda x: jnp.concatenate([jnp.where(lane_lo, x, 0.0), jnp.where(lane_lo, 0.0, x)], axis=0)
    out = dict(a2=[], r2=[], b2=[], k2=[], v2=[], bkp=[])
    p_ends, bonuses = [], []
    for p in range(r.shape[1] // LANES):
        ps = slice(p * LANES, (p + 1) * LANES)
        r_p, k_p, v_p, lw_p, iclr_p = r[:, ps], k[:, ps], v[:, ps], lw[:, ps], iclr[:, ps]
        kk = k_p * kk_ref[:, ps]
        kk = kk / jnp.maximum(jnp.sqrt(_dot_split_lhs(kk * kk, ones_blk, 2)), 1e-12)
        k_p = k_p * (1.0 + (iclr_p - 1.0) * ka_ref[:, ps])
        bonuses.append(_dot_split_lhs(r_p * k_p * rk_ref[:, ps], ones_blk, 2) * v_p)
        cum = _dot_split_rhs(tril_c, lw_p, 3)
        p_in = jnp.exp(cum)
        p_inv = jnp.exp(-cum)
        a_t = -(kk * jnp.exp(cum - lw_p))
        b_t = kk * iclr_p * p_inv
        k_t = k_p * p_inv
        p_end = p_in[cl - 1:cl, :]
        out["a2"].append(_mx(stack2(a_t)))
        out["r2"].append(_mx(stack2(r_p * p_in)))
        out["b2"].append(_mx(stack2(b_t)))
        out["k2"].append(_mx(stack2(k_t)))
        out["v2"].append(_mx(stack2(v_p)))
        out["bkp"].append(_mx(jnp.concatenate([stack2(b_t * p_end), stack2(k_t * p_end)], axis=0).T))
        p_ends.append(p_end)
        yield
    out.update(p_end=jnp.concatenate(p_ends, axis=1), bonus=jnp.concatenate(bonuses, axis=1), gate=gate)
    return out


def _rwkv_scan(ops_list, st_ref, ybuf, lnw_ref, lnb_ref, ones_ref, masks):
    cl = RWKV_CHUNK
    gather = lambda name: [x for ops in ops_list for x in ops[name]]
    a2, r2, b2, k2, v2, bkp_t = (gather(name) for name in _RWKV_STAGED)
    n_tiles = len(ops_list[0]["a2"])
    tiles = range(len(a2))
    cat0 = lambda *xs: jnp.concatenate(xs, axis=0)
    cat1 = lambda *xs: jnp.concatenate(xs, axis=1)
    gram = [_dot_nt(cat0(a2[t], r2[t]), cat0(b2[t], k2[t])) for t in tiles]
    yield
    a_ab = [jnp.where(masks["strict"], g[0:LANES, 0:LANES], 0.0) for g in gram]
    a_ak = [_mx(jnp.where(masks["strict"], g[0:LANES, LANES:], 0.0)) for g in gram]
    a_rb = [_mx(jnp.where(masks["incl"], g[LANES:, 0:LANES], 0.0)) for g in gram]
    a_rk = [_mx(jnp.where(masks["incl"], g[LANES:, LANES:], 0.0)) for g in gram]
    t_inv = yield from _unit_lower_inverse(a_ab, masks["blk"], masks["eye_f"])
    st = [st_ref[t] for t in tiles]
    st_m = [_mx(s) for s in st]
    y0 = [_dot(cat1(a2[t], a_ak[t]), cat0(st_m[t], v2[t])) for t in tiles]
    yield
    sa = [_mx(_dot(_mx(t_inv[t]), _mx(y0[t]))) for t in tiles]
    yield
    o2 = [_dot(cat1(r2[t], a_rb[t], a_rk[t]), cat0(st_m[t], sa[t], v2[t])) for t in tiles]
    yield
    upd = [_dot(bkp_t[t], cat0(sa[t], v2[t])) for t in tiles]
    for t in tiles:
        b, p = divmod(t, n_tiles)
        ps = slice(p * LANES, (p + 1) * LANES)
        ybuf[b, :, ps] = o2[t][0:cl, :] + o2[t][cl:, :]
        p_end = ops_list[b]["p_end"][:, ps]
        p_col = jnp.sum(jnp.where(masks["eye"], p_end, 0.0), axis=1, keepdims=True)
        st_ref[t] = st[t] * p_col + upd[t]
    yield
    ones_blk = ones_ref[...]
    inv_n = 1.0 / HEAD
    outs = []
    for b, ops in enumerate(ops_list):
        y = ybuf[b]
        mean = _rwkv_head_sum(y, ones_blk) * inv_n
        d = y - mean
        var = _rwkv_head_sum(d * d, ones_blk) * inv_n
        y = d * lax.rsqrt(var + RWKV_LN_EPS) * lnw_ref[...] + lnb_ref[...]
        outs.append((y + ops["bonus"]) * ops["gate"])
    return outs


def _run(gen):
    while True:
        try:
            next(gen)
        except StopIteration as stop:
            return stop.value


def _interleave(scan_gen, prep_gens):
    gens = [scan_gen] + list(prep_gens)
    results, done = [None] * len(gens), [False] * len(gens)
    while not all(done):
        for i, gen in enumerate(gens):
            if not done[i]:
                try:
                    next(gen)
                except StopIteration as stop:
                    results[i], done[i] = stop.value, True
    return results[0], results[1:]


_RWKV_STAGED = ("a2", "r2", "b2", "k2", "v2", "bkp")
RWKV_SEQS_PER_STEP = 2


def _rwkv_body(*refs):
    cl = RWKV_CHUNK
    first_src, odd_src, next_src, mu_ref = refs[0:4]
    prm = refs[4:12] + (refs[14],)
    lnw_ref, lnb_ref, ones_ref = refs[12], refs[13], refs[14]
    o_ref = refs[15]
    buf = refs[16]
    st_ref, ybuf = refs[17], refs[18]
    staged = dict(zip(_RWKV_STAGED, refs[19:25]))
    s_pend, s_bonus, s_gate = refs[25], refs[26], refs[27]
    masks = _rwkv_masks()
    seqs = range(o_ref.shape[0])
    n_tiles = st_ref.shape[0] // len(seqs)

    def prepare(src, b, first):
        return _rwkv_prepare(src.at[b], buf.at[b], mu_ref, prm, masks, first)

    def stash(b, ops):
        for name in _RWKV_STAGED:
            for p in range(n_tiles):
                staged[name][b * n_tiles + p] = ops[name][p]
        s_pend[b] = ops["p_end"]
        s_bonus[b] = ops["bonus"]
        s_gate[b] = ops["gate"]

    def staged_ops(b):
        ops = {name: [staged[name][b * n_tiles + p] for p in range(n_tiles)] for name in _RWKV_STAGED}
        ops.update(p_end=s_pend[b], bonus=s_bonus[b], gate=s_gate[b])
        return ops

    @pl.when(pl.program_id(1) == 0)
    def _():
        st_ref[...] = jnp.zeros(st_ref.shape, F32)
        for b in seqs:
            stash(b, _run(prepare(first_src, b, True)))

    scan = lambda ops_list: _rwkv_scan(ops_list, st_ref, ybuf, lnw_ref, lnb_ref, ones_ref, masks)
    y_even, odd = _interleave(scan([staged_ops(b) for b in seqs]), [prepare(odd_src, b, False) for b in seqs])
    for b in seqs:
        o_ref[b, 0:cl, :] = y_even[b].astype(o_ref.dtype)
    y_odd, nxt = _interleave(scan(odd), [prepare(next_src, b, False) for b in seqs])
    for b in seqs:
        o_ref[b, cl:2 * cl, :] = y_odd[b].astype(o_ref.dtype)
        stash(b, nxt[b])


def _rwkv(u2, bsz, seq, mu, w0, w2, a0, a2, g2, k_k, k_a, r_k, ln_w, ln_b):
    cl = RWKV_CHUNK
    nc = seq // cl
    nb = RWKV_SEQS_PER_STEP
    assert nc % 2 == 0 and bsz % nb == 0
    w = W_MIX
    mw = RWKV_MISC
    uw = u2.shape[1]
    n_tiles = w // LANES
    row1 = lambda vec: vec.reshape(1, -1)
    mu_row = jnp.zeros((uw,), F32).at[U2_R:U2_R + mu.shape[0]].set(mu)
    w2e = jnp.zeros((2 * LANES, w), F32).at[MISC_PW:MISC_PA].set(w2)
    a2e = jnp.zeros((2 * LANES, w), F32).at[MISC_PA:MISC_PG].set(a2)
    g2e = jnp.zeros((mw - LANES, w), F32).at[MISC_PG - LANES:MISC_END - LANES].set(g2)
    ones_blk = np.kron(np.eye(LANES // HEAD, dtype=np.float32), np.ones((HEAD, HEAD), np.float32))

    def chunk_spec(chunk_of):
        return pl.BlockSpec((nb, cl, uw), lambda bb, j: (bb, chunk_of(j), 0))

    cvec = lambda width: pl.BlockSpec((1, width), lambda bb, j: (0, 0))
    cmat = lambda rows, width: pl.BlockSpec((rows, width), lambda bb, j: (0, 0))
    stage = lambda cols: pltpu.VMEM((nb * n_tiles, LANES, cols), MXU_DTYPE)
    out = pl.pallas_call(
        _rwkv_body,
        grid=(bsz // nb, nc // 2),
        in_specs=([chunk_spec(lambda j: 0), chunk_spec(lambda j: 2 * j + 1),
                   chunk_spec(lambda j: jnp.minimum(2 * j + 2, nc - 1)), cvec(uw)]
                  + [cvec(w), cmat(2 * LANES, w), cvec(w), cmat(2 * LANES, w), cmat(mw - LANES, w),
                     cvec(w), cvec(w), cvec(w), cvec(w), cvec(w), cmat(LANES, LANES)]),
        out_specs=pl.BlockSpec((nb, 2 * cl, w), lambda bb, j: (bb, j, 0)),
        out_shape=jax.ShapeDtypeStruct((bsz, seq, w), MXU_DTYPE),
        scratch_shapes=[pltpu.VMEM((nb, cl + SUBLANES, uw), F32),
                        pltpu.VMEM((nb * n_tiles, LANES, LANES), F32),
                        pltpu.VMEM((nb, cl, w), F32),
                        stage(LANES), stage(LANES), stage(LANES), stage(LANES), stage(LANES),
                        stage(2 * LANES),
                        pltpu.VMEM((nb, 1, w), F32), pltpu.VMEM((nb, cl, w), F32),
                        pltpu.VMEM((nb, cl, w), F32)],
        compiler_params=_cparams(("arbitrary", "arbitrary")),
        name="rwkv7_mixer",
    )(*([u2.reshape(bsz, seq, uw)] * 3), row1(mu_row), row1(w0), _mx(w2e), row1(a0), _mx(a2e), _mx(g2e), row1(k_k), row1(k_a),
      row1(r_k), row1(ln_w), row1(ln_b), jnp.asarray(ones_blk, MXU_DTYPE))
    return out.reshape(bsz * seq, w)


def _attn_body(q_ref, k_ref, v_ref, o_ref):
    d = q_ref.shape[1]
    hd = d // XATTN_HEADS
    scale = hd ** -0.5
    heads = [slice(h * hd, (h + 1) * hd) for h in range(XATTN_HEADS)]
    scores = [_dot_nt(q_ref[:, hs], k_ref[0, :, hs]) * scale for hs in heads]
    probs = []
    for s in scores:
        e = jnp.exp(s - jnp.max(s, axis=-1, keepdims=True))
        probs.append(_mx(e / jnp.sum(e, axis=-1, keepdims=True)))
    for hs, p in zip(heads, probs):
        o_ref[:, hs] = _dot(p, v_ref[0, :, hs]).astype(o_ref.dtype)


def _attention(q, k, v, bsz, seq, ts=512):
    d = q.shape[1]
    n_mem = k.shape[1]
    ts = min(ts, seq)
    ns = seq // ts
    return pl.pallas_call(
        _attn_body,
        grid=(bsz, ns),
        in_specs=[pl.BlockSpec((ts, d), lambda b, s: (b * ns + s, 0)),
                  pl.BlockSpec((1, n_mem, d), lambda b, s: (b, 0, 0)),
                  pl.BlockSpec((1, n_mem, d), lambda b, s: (b, 0, 0))],
        out_specs=pl.BlockSpec((ts, d), lambda b, s: (b * ns + s, 0)),
        out_shape=jax.ShapeDtypeStruct((bsz * seq, d), MXU_DTYPE),
        compiler_params=_cparams(("arbitrary", "arbitrary")),
        name="mem_attention",
    )(q, k, v)


def _rwkv_side(mat):
    pad = U2_COLS - (mat.shape[-1] - U1_COLS)
    return jnp.pad(mat[..., U1_COLS:], [(0, 0)] * (mat.ndim - 1) + [(0, pad)])


def kernel(x, mem, norm_mix_g, w_in, ssd_conv_w, ssd_conv_b, ssd_dt_bias, ssd_a_log, ssd_d, ssd_norm_g, rwkv_mu, rwkv_w0, rwkv_w2, rwkv_a0, rwkv_a2, rwkv_g2, rwkv_k_k, rwkv_k_a, rwkv_r_k, rwkv_ln_w, rwkv_ln_b, w_out, norm_x_g, norm_mem_g, xattn_wq, xattn_wk, xattn_wv, xattn_wo, norm_ffn_g, ffn_w1, ffn_w2, final_norm_g):
    bsz, seq, d = x.shape
    n_mem = mem.shape[1]
    xr = x.reshape(bsz * seq, d)
    memr = mem.reshape(bsz * n_mem, d)
    n_layers = w_in.shape[0]
    h = _rmsnorm(xr, norm_mix_g[0], MXU_DTYPE)
    for l in range(n_layers):
        u1 = _matmul_resident(h, w_in[l], F32, n=U1_COLS, tm=2048, tn=512, name="in_proj_ssd")
        u2 = _matmul_resident(h, _rwkv_side(w_in[l]), F32, tm=2048, tn=512, name="in_proj_rwkv")
        y_ssd = _ssd(u1, u2, bsz, seq, ssd_conv_w[l], ssd_conv_b[l], ssd_dt_bias[l], ssd_a_log[l],
                     ssd_d[l], ssd_norm_g[l])
        y_rwkv = _rwkv(u2, bsz, seq, rwkv_mu[l], rwkv_w0[l], rwkv_w2[l], rwkv_a0[l], rwkv_a2[l],
                       rwkv_g2[l], rwkv_k_k[l], rwkv_k_a[l], rwkv_r_k[l].reshape(-1),
                       rwkv_ln_w[l], rwkv_ln_b[l])
        wo = _mx(w_out[l])
        xr, h = _matmul_rows([(y_ssd, wo, 0), (y_rwkv, wo, W_MIX)], xr, norm_x_g[l],
                             MXU_DTYPE, name="out_proj")

        m = _rmsnorm(memr, norm_mem_g[l], MXU_DTYPE)
        q = _matmul_resident(h, xattn_wq[l], MXU_DTYPE, name="q_proj")
        kx = _matmul_resident(m, xattn_wk[l], MXU_DTYPE, name="k_proj")
        vx = _matmul_resident(m, xattn_wv[l], MXU_DTYPE, name="v_proj")
        o = _attention(q, kx.reshape(bsz, n_mem, d), vx.reshape(bsz, n_mem, d), bsz, seq)
        xr, h = _matmul_rows([(o, _mx(xattn_wo[l]), 0)], xr, norm_ffn_g[l], MXU_DTYPE, name="o_proj")

        hid = _matmul_resident(h, ffn_w1[l], MXU_DTYPE, act="relu2", name="ffn_up")
        down = [(hid, _mx(ffn_w2[l]), 0)]
        if l + 1 < n_layers:
            xr, h = _matmul_rows(down, xr, norm_mix_g[l + 1], MXU_DTYPE, name="ffn_down")
        else:
            out = _matmul_rows(down, xr, final_norm_g, x.dtype, emit_x=False, name="ffn_down")
    return out.reshape(bsz, seq, d)
```

```python
import functools

import numpy as np
import jax
import jax.numpy as jnp
from jax import lax
from jax.experimental import pallas as pl
from jax.experimental.pallas import tpu as pltpu

F32 = jnp.float32
MXU_DTYPE = jnp.bfloat16

NORM_EPS = 1e-6
RWKV_LN_EPS = 64e-5

HEAD = 64
SSD_STATE = 128
SSD_GROUPS = 2
SSD_CHUNK = 128
SSD_CONV = 4
RWKV_CHUNK = 64
RWKV_DECAY_RANK = 96
RWKV_AAA_RANK = 96
RWKV_GATE_RANK = 256
XATTN_HEADS = 4

LANES = 128
SUBLANES = 8
VMEM_LIMIT = 56 * 1024 * 1024

W_MIX = 1024
SSD_BC = 2 * SSD_GROUPS * SSD_STATE
U1_COLS = 2 * W_MIX + SSD_BC
U2_MISC = 3 * W_MIX
RWKV_MISC = 512
U2_COLS = U2_MISC + RWKV_MISC
MISC_PW = 0
MISC_PA = MISC_PW + RWKV_DECAY_RANK
MISC_PG = MISC_PA + RWKV_AAA_RANK
MISC_DT = MISC_PG + RWKV_GATE_RANK
SSD_DT_LANE = MISC_DT % LANES


def _mx(a):
    return a.astype(MXU_DTYPE)


def _dot(a, b):
    return jnp.dot(a, b, preferred_element_type=F32)


def _dot_nt(a, b):
    return lax.dot_general(a, b, (((1,), (1,)), ((), ())), preferred_element_type=F32)


def _split(v, parts):
    out = []
    rem = v
    for _ in range(parts):
        p = rem.astype(MXU_DTYPE)
        out.append(p)
        rem = rem - p.astype(F32)
    return out


def _dot_split_rhs(a01, v, parts):
    acc = None
    for p in _split(v, parts):
        t = _dot(a01, p)
        acc = t if acc is None else acc + t
    return acc


def _dot_split_lhs(v, b01, parts):
    acc = None
    for p in _split(v, parts):
        t = _dot(p, b01)
        acc = t if acc is None else acc + t
    return acc


def _sigmoid(x):
    return 1.0 / (1.0 + jnp.exp(-x))


def _softplus(x):
    return jnp.maximum(x, 0.0) + jnp.log(1.0 + jnp.exp(-jnp.abs(x)))


def _cparams(sem):
    return pltpu.CompilerParams(dimension_semantics=sem, vmem_limit_bytes=VMEM_LIMIT)


def _rmsnorm_body(x_ref, g_ref, o_ref):
    x = x_ref[...]
    ms = jnp.mean(x * x, axis=-1, keepdims=True)
    o_ref[...] = (x * lax.rsqrt(ms + NORM_EPS) * g_ref[...]).astype(o_ref.dtype)


def _rmsnorm(x, g, out_dtype, tm=512):
    m, d = x.shape
    tm = min(tm, m)
    return pl.pallas_call(
        _rmsnorm_body,
        grid=(m // tm,),
        in_specs=[pl.BlockSpec((tm, d), lambda i: (i, 0)),
                  pl.BlockSpec((1, d), lambda i: (0, 0))],
        out_specs=pl.BlockSpec((tm, d), lambda i: (i, 0)),
        out_shape=jax.ShapeDtypeStruct((m, d), out_dtype),
        compiler_params=_cparams(("arbitrary",)),
        name="rmsnorm",
    )(x, g.reshape(1, d))


def _mm_resident_body(a_ref, b_ref, o_ref, bm_ref, *, act, b_is_transposed):
    @pl.when(pl.program_id(1) == 0)
    def _():
        b = b_ref[...]
        bm_ref[...] = (b.T if b_is_transposed else b).astype(MXU_DTYPE)

    r = _dot(a_ref[...], bm_ref[...])
    if act == "relu2":
        r = jnp.square(jnp.maximum(r, 0.0))
    o_ref[...] = r.astype(o_ref.dtype)


def _matmul_resident(a, b, out_dtype, act=None, n=None, b_is_transposed=False, tm=1024, tn=1024,
                     name="matmul_resident"):
    m, kdim = a.shape
    n = b.shape[0 if b_is_transposed else 1] if n is None else n
    tm, tn = min(tm, m), min(tn, n)
    assert m % tm == 0 and n % tn == 0
    b_spec = (pl.BlockSpec((tn, kdim), lambda j, i: (j, 0)) if b_is_transposed
              else pl.BlockSpec((kdim, tn), lambda j, i: (0, j)))
    return pl.pallas_call(
        functools.partial(_mm_resident_body, act=act, b_is_transposed=b_is_transposed),
        grid=(n // tn, m // tm),
        in_specs=[pl.BlockSpec((tm, kdim), lambda j, i: (i, 0)), b_spec],
        out_specs=pl.BlockSpec((tm, tn), lambda j, i: (i, j)),
        out_shape=jax.ShapeDtypeStruct((m, n), out_dtype),
        scratch_shapes=[pltpu.VMEM((kdim, tn), MXU_DTYPE)],
        compiler_params=_cparams(("arbitrary", "arbitrary")),
        name=name,
    )(a, b)


def _mm_rows_body(*refs, n_pairs, nk, emit_x):
    ab = refs[:2 * n_pairs]
    res_ref, g_ref = refs[2 * n_pairs:2 * n_pairs + 2]
    outs = refs[2 * n_pairs + 2:]
    x_ref = outs[0] if emit_x else None
    h_ref = outs[int(emit_x)]
    acc_ref = outs[int(emit_x) + 1] if nk > 1 else None
    part = None
    for p in range(n_pairs):
        t = _dot(ab[2 * p][...], ab[2 * p + 1][...])
        part = t if part is None else part + t

    def finish(acc):
        x = res_ref[...] + acc
        if emit_x:
            x_ref[...] = x
        ms = jnp.mean(x * x, axis=-1, keepdims=True)
        h_ref[...] = (x * lax.rsqrt(ms + NORM_EPS) * g_ref[...]).astype(h_ref.dtype)

    if nk == 1:
        finish(part)
        return

    k = pl.program_id(1)

    @pl.when(k == 0)
    def _():
        acc_ref[...] = part

    @pl.when(k > 0)
    def _():
        acc_ref[...] += part

    @pl.when(k == nk - 1)
    def _():
        finish(acc_ref[...])


def _matmul_rows(pairs, res, g, h_dtype, emit_x=True, tm=512, tk=2048, name="matmul_rows"):
    m, n = res.shape
    tm = min(tm, m)
    kdims = [a.shape[1] for a, _, _ in pairs]
    tk = min(tk, min(kdims))
    nk = kdims[0] // tk
    assert all(kd == nk * tk for kd in kdims) and all(row0 % tk == 0 for _, _, row0 in pairs)
    in_specs, args = [], []
    for a, b, row0 in pairs:
        in_specs += [pl.BlockSpec((tm, tk), lambda i, k: (i, k)),
                     pl.BlockSpec((tk, n), lambda i, k, blk0=row0 // tk: (blk0 + k, 0))]
        args += [a, b]
    row_spec = pl.BlockSpec((tm, n), lambda i, k: (i, 0))
    in_specs += [row_spec, pl.BlockSpec((1, n), lambda i, k: (0, 0))]
    out_specs, out_shape = [row_spec], [jax.ShapeDtypeStruct((m, n), h_dtype)]
    if emit_x:
        out_specs, out_shape = [row_spec] + out_specs, [jax.ShapeDtypeStruct((m, n), F32)] + out_shape
    outs = pl.pallas_call(
        functools.partial(_mm_rows_body, n_pairs=len(pairs), nk=nk, emit_x=emit_x),
        grid=(m // tm, nk),
        in_specs=in_specs,
        out_specs=out_specs,
        out_shape=out_shape,
        scratch_shapes=[pltpu.VMEM((tm, n), F32)] if nk > 1 else [],
        compiler_params=_cparams(("arbitrary", "arbitrary")),
        name=name,
    )(*args, res, g.reshape(1, n))
    return tuple(outs) if emit_x else outs[0]


def _ssd_body(z_ref, xs_ref, bc_ref, m_ref, cwx_ref, cbx_ref, cwbc_ref, cbbc_ref, dtb_ref,
              alog_ref, dvec_ref, ng_ref, esel_ref, o_ref, xbuf, bcbuf, st_ref):
    q = SSD_CHUNK
    n = SSD_STATE
    gw = st_ref.shape[2]
    c = pl.program_id(1)

    @pl.when(c == 0)
    def _():
        xbuf[0:SUBLANES, :] = jnp.zeros((SUBLANES, xbuf.shape[1]), F32)
        bcbuf[0:SUBLANES, :] = jnp.zeros((SUBLANES, bcbuf.shape[1]), F32)
        st_ref[...] = jnp.zeros(st_ref.shape, F32)

    @pl.when(c > 0)
    def _():
        xbuf[0:SUBLANES, :] = xbuf[q:q + SUBLANES, :]
        bcbuf[0:SUBLANES, :] = bcbuf[q:q + SUBLANES, :]

    xbuf[SUBLANES:SUBLANES + q, :] = xs_ref[...]
    bcbuf[SUBLANES:SUBLANES + q, :] = bc_ref[...]

    def conv_silu(buf, w_ref, b_ref):
        acc = None
        for k in range(SSD_CONV):
            off = SUBLANES - (SSD_CONV - 1) + k
            t = buf[off:off + q, :] * w_ref[k:k + 1, :]
            acc = t if acc is None else acc + t
        acc = acc + b_ref[...]
        return acc * _sigmoid(acc)

    xc = conv_silu(xbuf, cwx_ref, cbx_ref)
    bcc = conv_silu(bcbuf, cwbc_ref, cbbc_ref)
    g_n = SSD_GROUPS * n

    lane = lax.broadcasted_iota(jnp.int32, (1, LANES), 1)
    n_heads = SSD_GROUPS * gw // HEAD
    dmask = (lane >= SSD_DT_LANE) & (lane < SSD_DT_LANE + n_heads)
    dt = jnp.where(dmask, _softplus(m_ref[...] + dtb_ref[...]), 0.0)
    a = jnp.where(dmask, -jnp.exp(alog_ref[...]), 0.0)
    da = dt * a
    row = lax.broadcasted_iota(jnp.int32, (q, q), 0)
    col = lax.broadcasted_iota(jnp.int32, (q, q), 1)
    tri = row >= col
    tril = jnp.where(tri, 1.0, 0.0).astype(MXU_DTYPE)
    cs = _dot_split_rhs(tril, da, 3)
    ecs = jnp.exp(cs)
    dte = jnp.exp(cs[q - 1:q, :] - cs)
    esel = esel_ref[...]
    dt_e = _dot_split_lhs(dt, esel, 2)
    ecs_e = _dot_split_lhs(ecs, esel, 2)
    dte_e = _dot_split_lhs(dte, esel, 2)
    cs_t = cs.T

    xdt = xc * dt_e
    xdt_m = _mx(xdt)
    xd_m = _mx(xdt * dte_e)
    lane_lo = lane < HEAD

    y_cols = []
    for g in range(SSD_GROUPS):
        bg = bcc[:, g * n:(g + 1) * n]
        cg = _mx(bcc[:, g_n + g * n:g_n + (g + 1) * n])
        cb = _dot_nt(cg, _mx(bg))
        gs = slice(g * gw, (g + 1) * gw)
        st = st_ref[g]
        y_off = _dot(cg, _mx(st)) * ecs_e[:, gs]
        st_ref[g] = st * ecs_e[q - 1:q, gs] + _dot(_mx(bg.T), xd_m[:, gs])
        for pr in range(gw // LANES):
            h0 = (g * gw + pr * LANES) // HEAD
            ps = slice(g * gw + pr * LANES, g * gw + (pr + 1) * LANES)
            res = []
            for hh in (h0, h0 + 1):
                li = SSD_DT_LANE + hh
                seg = cs[:, li:li + 1] - cs_t[li:li + 1, :]
                lm = jnp.where(tri, jnp.exp(jnp.where(tri, seg, 0.0)), 0.0)
                res.append(_dot(_mx(cb * lm), xdt_m[:, ps]))
            y_diag = jnp.where(lane_lo, res[0], res[1])
            y_cols.append(y_diag + y_off[:, pr * LANES:(pr + 1) * LANES])
    y = jnp.concatenate(y_cols, axis=1) + xc * dvec_ref[...]
    zz = z_ref[...]
    y = y * (zz * _sigmoid(zz))
    outs = []
    for g in range(SSD_GROUPS):
        yg = y[:, g * gw:(g + 1) * gw]
        ms = jnp.mean(yg * yg, axis=-1, keepdims=True)
        outs.append(yg * lax.rsqrt(ms + NORM_EPS))
    y = jnp.concatenate(outs, axis=1) * ng_ref[...]
    o_ref[...] = y.astype(o_ref.dtype)


def _ssd(u1, u2, bsz, seq, conv_w, conv_b, dt_bias, a_log, d_skip, norm_g):
    q = SSD_CHUNK
    nc = seq // q
    w = W_MIX
    gw = w // SSD_GROUPS
    heads = w // HEAD
    bcw = SSD_BC
    cw = conv_w[:, 0, :]
    pad = lambda v: jnp.zeros((1, LANES), F32).at[0, SSD_DT_LANE:SSD_DT_LANE + heads].set(v)
    esel = np.zeros((LANES, w), np.float32)
    for h in range(heads):
        esel[SSD_DT_LANE + h, h * HEAD:(h + 1) * HEAD] = 1.0
    rowblk = lambda cb: (lambda b, c: (b * nc + c, cb))
    const = lambda b, c: (0, 0)
    return pl.pallas_call(
        _ssd_body,
        grid=(bsz, nc),
        in_specs=[pl.BlockSpec((q, w), rowblk(0)),
                  pl.BlockSpec((q, w), rowblk(1)),
                  pl.BlockSpec((q, bcw), rowblk(2 * w // bcw)),
                  pl.BlockSpec((q, LANES), rowblk((U2_MISC + MISC_DT) // LANES)),
                  pl.BlockSpec((SSD_CONV, w), const), pl.BlockSpec((1, w), const),
                  pl.BlockSpec((SSD_CONV, bcw), const), pl.BlockSpec((1, bcw), const),
                  pl.BlockSpec((1, LANES), const), pl.BlockSpec((1, LANES), const),
                  pl.BlockSpec((1, w), const), pl.BlockSpec((1, w), const),
                  pl.BlockSpec((LANES, w), const)],
        out_specs=pl.BlockSpec((q, w), lambda b, c: (b * nc + c, 0)),
        out_shape=jax.ShapeDtypeStruct((bsz * seq, w), MXU_DTYPE),
        scratch_shapes=[pltpu.VMEM((q + SUBLANES, w), F32),
                        pltpu.VMEM((q + SUBLANES, bcw), F32),
                        pltpu.VMEM((SSD_GROUPS, SSD_STATE, gw), F32)],
        compiler_params=_cparams(("arbitrary", "arbitrary")),
        name="ssd_mixer",
    )(u1, u1, u1, u2, cw[:, :w], conv_b[:w].reshape(1, w), cw[:, w:], conv_b[w:].reshape(1, bcw),
      pad(dt_bias), pad(a_log), jnp.repeat(d_skip, HEAD).reshape(1, w), norm_g.reshape(1, w),
      jnp.asarray(esel, MXU_DTYPE))


def _unit_lower_inverse(l_mats, blk_mask, eye_f):
    mm = lambda xs, ys: [_dot(x, y) for x, y in zip(xs, ys)]
    mx = lambda xs: [_mx(x) for x in xs]
    add = lambda xs, ys: [x + y for x, y in zip(xs, ys)]
    d = [jnp.where(blk_mask, l, 0.0) for l in l_mats]
    o = mx([l - x for l, x in zip(l_mats, d)])
    d1 = mx(d)
    d2 = mx(mm(d1, d1))
    t = [eye_f + x for x in d]
    yield
    d4 = mx(mm(d2, d2))
    t = add(t, mm(d2, mx(t)))
    yield
    d8 = mx(mm(d4, d4))
    t = add(t, mm(d4, mx(t)))
    yield
    t_d = add(t, mm(d8, mx(t)))
    t_dm = mx(t_d)
    yield
    m1 = mm(t_dm, o)
    m1m = mx(m1)
    yield
    m2 = mm(m1m, m1m)
    yield
    m3 = mm(m1m, mx(m2))
    w = mx([x + y + z for x, y, z in zip(m1, m2, m3)])
    yield
    return add(t_d, mm(w, t_dm))


def _rwkv_masks():
    lane = lax.broadcasted_iota(jnp.int32, (1, LANES), 1)
    row = lax.broadcasted_iota(jnp.int32, (LANES, LANES), 0)
    col = lax.broadcasted_iota(jnp.int32, (LANES, LANES), 1)
    same = (row >= HEAD) == (col >= HEAD)
    rs = row & (HEAD - 1)
    cs = col & (HEAD - 1)
    eye = row == col
    return dict(lane_lo=lane < HEAD, strict=same & (rs > cs), incl=same & (rs >= cs),
                blk=(row >> 4) == (col >> 4), eye=eye, eye_f=jnp.where(eye, 1.0, 0.0))


def _rwkv_head_sum(x, ones_blk):
    cols = [_dot_split_lhs(x[:, t * LANES:(t + 1) * LANES], ones_blk, 2)
            for t in range(x.shape[1] // LANES)]
    return jnp.concatenate(cols, axis=1)


def _rwkv_prepare(src, buf, mu_ref, prm, masks, first):
    cl = RWKV_CHUNK
    w0_ref, w2_ref, a0_ref, a2_ref, g2_ref, kk_ref, ka_ref, rk_ref, ones_ref = prm
    if first:
        buf[0:SUBLANES, :] = jnp.zeros((SUBLANES, buf.shape[1]), F32)
    else:
        buf[0:SUBLANES, :] = buf[cl:cl + SUBLANES, :]
    cur = src[...]
    buf[SUBLANES:SUBLANES + cl, :] = cur
    prev = buf[SUBLANES - 1:SUBLANES - 1 + cl, :]
    row = cur + (prev - cur) * mu_ref[...]
    w = W_MIX
    r, k, v = (row[:, i * w:(i + 1) * w] for i in range(3))
    misc = row[:, U2_MISC:]
    pw_pa = misc[:, 0:2 * LANES]
    ones_blk = ones_ref[...]
    yield
    w_log = -_softplus(-(w0_ref[...] + _dot(_mx(jnp.tanh(pw_pa)), w2_ref[...]))) - 0.5
    lw = -jnp.exp(w_log)
    iclr = _sigmoid(a0_ref[...] + _dot(_mx(pw_pa), a2_ref[...]))
    gate = _dot(_mx(_sigmoid(misc[:, LANES:])), g2_ref[...])
    yield

    row_c = lax.broadcasted_iota(jnp.int32, (cl, cl), 0)
    col_c = lax.broadcasted_iota(jnp.int32, (cl, cl), 1)
    tril_c = jnp.where(row_c >= col_c, 1.0, 0.0).astype(MXU_DTYPE)
    lane_lo = masks["lane_lo"]
    stack2 = lambda x: jnp.concatenate([jnp.where(lane_lo, x, 0.0), jnp.where(lane_lo, 0.0, x)], axis=0)
    out = dict(a2=[], r2=[], b2=[], k2=[], v2=[], bkp=[])
    p_ends, bonuses = [], []
    for p in range(r.shape[1] // LANES):
        ps = slice(p * LANES, (p + 1) * LANES)
        r_p, k_p, v_p, lw_p, iclr_p = r[:, ps], k[:, ps], v[:, ps], lw[:, ps], iclr[:, ps]
        kk = k_p * kk_ref[:, ps]
        kk = kk / jnp.maximum(jnp.sqrt(_dot_split_lhs(kk * kk, ones_blk, 2)), 1e-12)
        k_p = k_p * (1.0 + (iclr_p - 1.0) * ka_ref[:, ps])
        bonuses.append(_dot_split_lhs(r_p * k_p * rk_ref[:, ps], ones_blk, 2) * v_p)
        cum = _dot_split_rhs(tril_c, lw_p, 3)
        p_in = jnp.exp(cum)
        p_inv = jnp.exp(-cum)
        a_t = -(kk * jnp.exp(cum - lw_p))
        b_t = kk * iclr_p * p_inv
        k_t = k_p * p_inv
        p_end = p_in[cl - 1:cl, :]
        out["a2"].append(_mx(stack2(a_t)))
        out["r2"].append(_mx(stack2(r_p * p_in)))
        out["b2"].append(_mx(stack2(b_t)))
        out["k2"].append(_mx(stack2(k_t)))
        out["v2"].append(_mx(stack2(v_p)))
        out["bkp"].append(_mx(jnp.concatenate([stack2(b_t * p_end), stack2(k_t * p_end)], axis=0).T))
        p_ends.append(p_end)
        yield
    out.update(p_end=jnp.concatenate(p_ends, axis=1), bonus=jnp.concatenate(bonuses, axis=1), gate=gate)
    return out


def _rwkv_scan(ops_list, st_ref, ybuf, lnw_ref, lnb_ref, ones_ref, masks):
    cl = RWKV_CHUNK
    gather = lambda name: [x for ops in ops_list for x in ops[name]]
    a2, r2, b2, k2, v2, bkp_t = (gather(name) for name in _RWKV_STAGED)
    n_tiles = len(ops_list[0]["a2"])
    tiles = range(len(a2))
    cat0 = lambda *xs: jnp.concatenate(xs, axis=0)
    cat1 = lambda *xs: jnp.concatenate(xs, axis=1)
    gram = [_dot_nt(cat0(a2[t], r2[t]), cat0(b2[t], k2[t])) for t in tiles]
    yield
    a_ab = [jnp.where(masks["strict"], g[0:LANES, 0:LANES], 0.0) for g in gram]
    a_ak = [_mx(jnp.where(masks["strict"], g[0:LANES, LANES:], 0.0)) for g in gram]
    a_rb = [_mx(jnp.where(masks["incl"], g[LANES:, 0:LANES], 0.0)) for g in gram]
    a_rk = [_mx(jnp.where(masks["incl"], g[LANES:, LANES:], 0.0)) for g in gram]
    t_inv = yield from _unit_lower_inverse(a_ab, masks["blk"], masks["eye_f"])
    st = [st_ref[t] for t in tiles]
    st_m = [_mx(s) for s in st]
    y0 = [_dot(cat1(a2[t], a_ak[t]), cat0(st_m[t], v2[t])) for t in tiles]
    yield
    sa = [_mx(_dot(_mx(t_inv[t]), _mx(y0[t]))) for t in tiles]
    yield
    o2 = [_dot(cat1(r2[t], a_rb[t], a_rk[t]), cat0(st_m[t], sa[t], v2[t])) for t in tiles]
    yield
    upd = [_dot(bkp_t[t], cat0(sa[t], v2[t])) for t in tiles]
    for t in tiles:
        b, p = divmod(t, n_tiles)
        ps = slice(p * LANES, (p + 1) * LANES)
        ybuf[b, :, ps] = o2[t][0:cl, :] + o2[t][cl:, :]
        p_end = ops_list[b]["p_end"][:, ps]
        p_col = jnp.sum(jnp.where(masks["eye"], p_end, 0.0), axis=1, keepdims=True)
        st_ref[t] = st[t] * p_col + upd[t]
    yield
    ones_blk = ones_ref[...]
    inv_n = 1.0 / HEAD
    outs = []
    for b, ops in enumerate(ops_list):
        y = ybuf[b]
        mean = _rwkv_head_sum(y, ones_blk) * inv_n
        d = y - mean
        var = _rwkv_head_sum(d * d, ones_blk) * inv_n
        y = d * lax.rsqrt(var + RWKV_LN_EPS) * lnw_ref[...] + lnb_ref[...]
        outs.append((y + ops["bonus"]) * ops["gate"])
    return outs


def _run(gen):
    while True:
        try:
            next(gen)
        except StopIteration as stop:
            return stop.value


def _interleave(scan_gen, prep_gens):
    gens = [scan_gen] + list(prep_gens)
    results, done = [None] * len(gens), [False] * len(gens)
    while not all(done):
        for i, gen in enumerate(gens):
            if not done[i]:
                try:
                    next(gen)
                except StopIteration as stop:
                    results[i], done[i] = stop.value, True
    return results[0], results[1:]


_RWKV_STAGED = ("a2", "r2", "b2", "k2", "v2", "bkp")
RWKV_SEQS_PER_STEP = 2


def _rwkv_body(*refs):
    cl = RWKV_CHUNK
    first_src, odd_src, next_src, mu_ref = refs[0:4]
    prm = refs[4:12] + (refs[14],)
    lnw_ref, lnb_ref, ones_ref = refs[12], refs[13], refs[14]
    o_ref = refs[15]
    buf = refs[16]
    st_ref, ybuf = refs[17], refs[18]
    staged = dict(zip(_RWKV_STAGED, refs[19:25]))
    s_pend, s_bonus, s_gate = refs[25], refs[26], refs[27]
    masks = _rwkv_masks()
    seqs = range(o_ref.shape[0])
    n_tiles = st_ref.shape[0] // len(seqs)

    def prepare(src, b, first):
        return _rwkv_prepare(src.at[b], buf.at[b], mu_ref, prm, masks, first)

    def stash(b, ops):
        for name in _RWKV_STAGED:
            for p in range(n_tiles):
                staged[name][b * n_tiles + p] = ops[name][p]
        s_pend[b] = ops["p_end"]
        s_bonus[b] = ops["bonus"]
        s_gate[b] = ops["gate"]

    def staged_ops(b):
        ops = {name: [staged[name][b * n_tiles + p] for p in range(n_tiles)] for name in _RWKV_STAGED}
        ops.update(p_end=s_pend[b], bonus=s_bonus[b], gate=s_gate[b])
        return ops

    @pl.when(pl.program_id(1) == 0)
    def _():
        st_ref[...] = jnp.zeros(st_ref.shape, F32)
        for b in seqs:
            stash(b, _run(prepare(first_src, b, True)))

    scan = lambda ops_list: _rwkv_scan(ops_list, st_ref, ybuf, lnw_ref, lnb_ref, ones_ref, masks)
    y_even, odd = _interleave(scan([staged_ops(b) for b in seqs]), [prepare(odd_src, b, False) for b in seqs])
    for b in seqs:
        o_ref[b, 0:cl, :] = y_even[b].astype(o_ref.dtype)
    y_odd, nxt = _interleave(scan(odd), [prepare(next_src, b, False) for b in seqs])
    for b in seqs:
        o_ref[b, cl:2 * cl, :] = y_odd[b].astype(o_ref.dtype)
        stash(b, nxt[b])


def _rwkv(u2, bsz, seq, mu, w0, w2, a0, a2, g2, k_k, k_a, r_k, ln_w, ln_b):
    cl = RWKV_CHUNK
    nc = seq // cl
    nb = RWKV_SEQS_PER_STEP
    assert nc % 2 == 0 and bsz % nb == 0
    w = W_MIX
    mw = RWKV_MISC
    uw = u2.shape[1]
    n_tiles = w // LANES
    row1 = lambda vec: vec.reshape(1, -1)
    mu_row = jnp.zeros((uw,), F32).at[0:mu.shape[0]].set(mu)
    w2e = jnp.zeros((2 * LANES, w), F32).at[MISC_PW:MISC_PA].set(w2)
    a2e = jnp.zeros((2 * LANES, w), F32).at[MISC_PA:MISC_PG].set(a2)
    g2e = jnp.zeros((mw - LANES, w), F32).at[MISC_PG - LANES:MISC_DT - LANES].set(g2)
    ones_blk = np.kron(np.eye(LANES // HEAD, dtype=np.float32), np.ones((HEAD, HEAD), np.float32))

    def chunk_spec(chunk_of):
        return pl.BlockSpec((nb, cl, uw), lambda bb, j: (bb, chunk_of(j), 0))

    cvec = lambda width: pl.BlockSpec((1, width), lambda bb, j: (0, 0))
    cmat = lambda rows, width: pl.BlockSpec((rows, width), lambda bb, j: (0, 0))
    stage = lambda cols: pltpu.VMEM((nb * n_tiles, LANES, cols), MXU_DTYPE)
    out = pl.pallas_call(
        _rwkv_body,
        grid=(bsz // nb, nc // 2),
        in_specs=([chunk_spec(lambda j: 0), chunk_spec(lambda j: 2 * j + 1),
                   chunk_spec(lambda j: jnp.minimum(2 * j + 2, nc - 1)), cvec(uw)]
                  + [cvec(w), cmat(2 * LANES, w), cvec(w), cmat(2 * LANES, w), cmat(mw - LANES, w),
                     cvec(w), cvec(w), cvec(w), cvec(w), cvec(w), cmat(LANES, LANES)]),
        out_specs=pl.BlockSpec((nb, 2 * cl, w), lambda bb, j: (bb, j, 0)),
        out_shape=jax.ShapeDtypeStruct((bsz, seq, w), MXU_DTYPE),
        scratch_shapes=[pltpu.VMEM((nb, cl + SUBLANES, uw), F32),
                        pltpu.VMEM((nb * n_tiles, LANES, LANES), F32),
                        pltpu.VMEM((nb, cl, w), F32),
                        stage(LANES), stage(LANES), stage(LANES), stage(LANES), stage(LANES),
                        stage(2 * LANES),
                        pltpu.VMEM((nb, 1, w), F32), pltpu.VMEM((nb, cl, w), F32),
                        pltpu.VMEM((nb, cl, w), F32)],
        compiler_params=_cparams(("arbitrary", "arbitrary")),
        name="rwkv7_mixer",
    )(*([u2.reshape(bsz, seq, uw)] * 3), row1(mu_row), row1(w0), _mx(w2e), row1(a0), _mx(a2e), _mx(g2e), row1(k_k), row1(k_a),
      row1(r_k), row1(ln_w), row1(ln_b), jnp.asarray(ones_blk, MXU_DTYPE))
    return out.reshape(bsz * seq, w)


def _attn_body(q_ref, k_ref, v_ref, o_ref):
    d = q_ref.shape[1]
    hd = d // XATTN_HEADS
    scale = hd ** -0.5
    heads = [slice(h * hd, (h + 1) * hd) for h in range(XATTN_HEADS)]
    scores = [_dot_nt(q_ref[:, hs], k_ref[0, :, hs]) * scale for hs in heads]
    probs = []
    for s in scores:
        e = jnp.exp(s - jnp.max(s, axis=-1, keepdims=True))
        probs.append(_mx(e / jnp.sum(e, axis=-1, keepdims=True)))
    for hs, p in zip(heads, probs):
        o_ref[:, hs] = _dot(p, v_ref[0, :, hs]).astype(o_ref.dtype)


def _attention(q, k, v, bsz, seq, ts=512):
    d = q.shape[1]
    n_mem = k.shape[1]
    ts = min(ts, seq)
    ns = seq // ts
    return pl.pallas_call(
        _attn_body,
        grid=(bsz, ns),
        in_specs=[pl.BlockSpec((ts, d), lambda b, s: (b * ns + s, 0)),
                  pl.BlockSpec((1, n_mem, d), lambda b, s: (b, 0, 0)),
                  pl.BlockSpec((1, n_mem, d), lambda b, s: (b, 0, 0))],
        out_specs=pl.BlockSpec((ts, d), lambda b, s: (b * ns + s, 0)),
        out_shape=jax.ShapeDtypeStruct((bsz * seq, d), MXU_DTYPE),
        compiler_params=_cparams(("arbitrary", "arbitrary")),
        name="mem_attention",
    )(q, k, v)


def _rwkv_side(mat, axis):
    ssd_in = U1_COLS + W_MIX // HEAD
    take = lambda lo, hi: lax.slice_in_dim(mat, lo, hi, axis=axis)
    pad_shape = list(mat.shape)
    pad_shape[axis] = U2_COLS - (mat.shape[axis] - U1_COLS)
    return jnp.concatenate([take(ssd_in, mat.shape[axis]), take(U1_COLS, ssd_in),
                            jnp.zeros(pad_shape, mat.dtype)], axis=axis)


def kernel(x, mem, norm_mix_g, w_in, ssd_conv_w, ssd_conv_b, ssd_dt_bias, ssd_a_log, ssd_d, ssd_norm_g, rwkv_mu, rwkv_w0, rwkv_w2, rwkv_a0, rwkv_a2, rwkv_g2, rwkv_k_k, rwkv_k_a, rwkv_r_k, rwkv_ln_w, rwkv_ln_b, w_out, norm_x_g, norm_mem_g, xattn_wq, xattn_wk, xattn_wv, xattn_wo, norm_ffn_g, ffn_w1, ffn_w2, final_norm_g):
    bsz, seq, d = x.shape
    n_mem = mem.shape[1]
    xr = x.reshape(bsz * seq, d)
    memr = mem.reshape(bsz * n_mem, d)
    n_layers = w_in.shape[0]
    h = _rmsnorm(xr, norm_mix_g[0], MXU_DTYPE)
    for l in range(n_layers):
        w_in_t = w_in[l].T
        u1 = _matmul_resident(h, w_in_t, F32, n=U1_COLS, b_is_transposed=True, tm=512,
                              tn=U1_COLS // 2, name="in_proj_ssd")
        u2 = _matmul_resident(h, _rwkv_side(w_in_t, 0), F32, b_is_transposed=True, tm=512,
                              tn=U2_COLS // 2, name="in_proj_rwkv")
        y_ssd = _ssd(u1, u2, bsz, seq, ssd_conv_w[l], ssd_conv_b[l], ssd_dt_bias[l], ssd_a_log[l],
                     ssd_d[l], ssd_norm_g[l])
        y_rwkv = _rwkv(u2, bsz, seq, rwkv_mu[l], rwkv_w0[l], rwkv_w2[l], rwkv_a0[l], rwkv_a2[l],
                       rwkv_g2[l], rwkv_k_k[l], rwkv_k_a[l], rwkv_r_k[l].reshape(-1),
                       rwkv_ln_w[l], rwkv_ln_b[l])
        wo = _mx(w_out[l])
        xr, h = _matmul_rows([(y_ssd, wo, 0), (y_rwkv, wo, W_MIX)], xr, norm_x_g[l],
                             MXU_DTYPE, name="out_proj")

        m = _rmsnorm(memr, norm_mem_g[l], MXU_DTYPE)
        q = _matmul_resident(h, xattn_wq[l], MXU_DTYPE, name="q_proj")
        kx = _matmul_resident(m, xattn_wk[l], MXU_DTYPE, name="k_proj")
        vx = _matmul_resident(m, xattn_wv[l], MXU_DTYPE, name="v_proj")
        o = _attention(q, kx.reshape(bsz, n_mem, d), vx.reshape(bsz, n_mem, d), bsz, seq)
        xr, h = _matmul_rows([(o, _mx(xattn_wo[l]), 0)], xr, norm_ffn_g[l], MXU_DTYPE, name="o_proj")

        hid = _matmul_resident(h, ffn_w1[l], MXU_DTYPE, act="relu2", name="ffn_up")
        down = [(hid, _mx(ffn_w2[l]), 0)]
        if l + 1 < n_layers:
            xr, h = _matmul_rows(down, xr, norm_mix_g[l + 1], MXU_DTYPE, name="ffn_down")
        else:
            out = _matmul_rows(down, xr, final_norm_g, x.dtype, emit_x=False, name="ffn_down")
    return out.reshape(bsz, seq, d)
```

```python
import functools

import numpy as np
import jax
import jax.numpy as jnp
from jax import lax
from jax.experimental import pallas as pl
from jax.experimental.pallas import tpu as pltpu

F32 = jnp.float32
MXU_DTYPE = jnp.bfloat16

NORM_EPS = 1e-6
RWKV_LN_EPS = 64e-5

HEAD = 64
SSD_STATE = 128
SSD_GROUPS = 2
SSD_CHUNK = 128
SSD_CONV = 4
RWKV_CHUNK = 64
RWKV_DECAY_RANK = 96
RWKV_AAA_RANK = 96
RWKV_GATE_RANK = 256
XATTN_HEADS = 4

LANES = 128
SUBLANES = 8
VMEM_LIMIT = 56 * 1024 * 1024

W_MIX = 1024
SSD_BC = 2 * SSD_GROUPS * SSD_STATE
U1_COLS = 2 * W_MIX + SSD_BC
U2_MISC = 3 * W_MIX
RWKV_MISC = 512
U2_COLS = U2_MISC + RWKV_MISC
MISC_PW = 0
MISC_PA = MISC_PW + RWKV_DECAY_RANK
MISC_PG = MISC_PA + RWKV_AAA_RANK
MISC_DT = MISC_PG + RWKV_GATE_RANK
SSD_DT_LANE = MISC_DT % LANES


def _mx(a):
    return a.astype(MXU_DTYPE)


def _dot(a, b):
    return jnp.dot(a, b, preferred_element_type=F32)


def _dot_nt(a, b):
    return lax.dot_general(a, b, (((1,), (1,)), ((), ())), preferred_element_type=F32)


def _split(v, parts):
    out = []
    rem = v
    for _ in range(parts):
        p = rem.astype(MXU_DTYPE)
        out.append(p)
        rem = rem - p.astype(F32)
    return out


def _dot_split_rhs(a01, v, parts):
    acc = None
    for p in _split(v, parts):
        t = _dot(a01, p)
        acc = t if acc is None else acc + t
    return acc


def _dot_split_lhs(v, b01, parts):
    acc = None
    for p in _split(v, parts):
        t = _dot(p, b01)
        acc = t if acc is None else acc + t
    return acc


def _sigmoid(x):
    return 1.0 / (1.0 + jnp.exp(-x))


def _softplus(x):
    return jnp.maximum(x, 0.0) + jnp.log(1.0 + jnp.exp(-jnp.abs(x)))


def _cparams(sem):
    return pltpu.CompilerParams(dimension_semantics=sem, vmem_limit_bytes=VMEM_LIMIT)


def _rmsnorm_body(x_ref, g_ref, o_ref):
    x = x_ref[...]
    ms = jnp.mean(x * x, axis=-1, keepdims=True)
    o_ref[...] = (x * lax.rsqrt(ms + NORM_EPS) * g_ref[...]).astype(o_ref.dtype)


def _rmsnorm(x, g, out_dtype, tm=512):
    m, d = x.shape
    tm = min(tm, m)
    return pl.pallas_call(
        _rmsnorm_body,
        grid=(m // tm,),
        in_specs=[pl.BlockSpec((tm, d), lambda i: (i, 0)),
                  pl.BlockSpec((1, d), lambda i: (0, 0))],
        out_specs=pl.BlockSpec((tm, d), lambda i: (i, 0)),
        out_shape=jax.ShapeDtypeStruct((m, d), out_dtype),
        compiler_params=_cparams(("arbitrary",)),
        name="rmsnorm",
    )(x, g.reshape(1, d))


def _mm_resident_body(*refs, act, b_is_transposed, normalize):
    if normalize:
        a_ref, g_ref, b_ref, o_ref, bm_ref = refs
    else:
        a_ref, b_ref, o_ref, bm_ref = refs

    @pl.when(pl.program_id(1) == 0)
    def _():
        b = b_ref[...]
        bm_ref[...] = (b.T if b_is_transposed else b).astype(MXU_DTYPE)

    a = a_ref[...]
    if normalize:
        ms = jnp.mean(a * a, axis=-1, keepdims=True)
        a = (a * lax.rsqrt(ms + NORM_EPS) * g_ref[...]).astype(MXU_DTYPE)
    r = _dot(a, bm_ref[...])
    if act == "relu2":
        r = jnp.square(jnp.maximum(r, 0.0))
    o_ref[...] = r.astype(o_ref.dtype)


def _matmul_resident(a, b, out_dtype, act=None, n=None, b_is_transposed=False, norm_g=None,
                     tm=1024, tn=1024, name="matmul_resident"):
    m, kdim = a.shape
    n = b.shape[0 if b_is_transposed else 1] if n is None else n
    tm, tn = min(tm, m), min(tn, n)
    assert m % tm == 0 and n % tn == 0
    b_spec = (pl.BlockSpec((tn, kdim), lambda j, i: (j, 0)) if b_is_transposed
              else pl.BlockSpec((kdim, tn), lambda j, i: (0, j)))
    in_specs, args = [pl.BlockSpec((tm, kdim), lambda j, i: (i, 0))], [a]
    if norm_g is not None:
        in_specs.append(pl.BlockSpec((1, kdim), lambda j, i: (0, 0)))
        args.append(norm_g.reshape(1, kdim))
    return pl.pallas_call(
        functools.partial(_mm_resident_body, act=act, b_is_transposed=b_is_transposed,
                          normalize=norm_g is not None),
        grid=(n // tn, m // tm),
        in_specs=in_specs + [b_spec],
        out_specs=pl.BlockSpec((tm, tn), lambda j, i: (i, j)),
        out_shape=jax.ShapeDtypeStruct((m, n), out_dtype),
        scratch_shapes=[pltpu.VMEM((kdim, tn), MXU_DTYPE)],
        compiler_params=_cparams(("arbitrary", "arbitrary")),
        name=name,
    )(*args, b)


def _mm_rows_body(*refs, n_pairs, nk, emit_x):
    ab = refs[:2 * n_pairs]
    res_ref, g_ref = refs[2 * n_pairs:2 * n_pairs + 2]
    outs = refs[2 * n_pairs + 2:]
    x_ref = outs[0] if emit_x else None
    h_ref = outs[int(emit_x)]
    acc_ref = outs[int(emit_x) + 1] if nk > 1 else None
    part = None
    for p in range(n_pairs):
        t = _dot(ab[2 * p][...], ab[2 * p + 1][...])
        part = t if part is None else part + t

    def finish(acc):
        x = res_ref[...] + acc
        if emit_x:
            x_ref[...] = x
        ms = jnp.mean(x * x, axis=-1, keepdims=True)
        h_ref[...] = (x * lax.rsqrt(ms + NORM_EPS) * g_ref[...]).astype(h_ref.dtype)

    if nk == 1:
        finish(part)
        return

    k = pl.program_id(1)

    @pl.when(k == 0)
    def _():
        acc_ref[...] = part

    @pl.when(k > 0)
    def _():
        acc_ref[...] += part

    @pl.when(k == nk - 1)
    def _():
        finish(acc_ref[...])


def _matmul_rows(pairs, res, g, h_dtype, emit_x=True, tm=512, tk=2048, name="matmul_rows"):
    m, n = res.shape
    tm = min(tm, m)
    kdims = [a.shape[1] for a, _, _ in pairs]
    tk = min(tk, min(kdims))
    nk = kdims[0] // tk
    assert all(kd == nk * tk for kd in kdims) and all(row0 % tk == 0 for _, _, row0 in pairs)
    in_specs, args = [], []
    for a, b, row0 in pairs:
        in_specs += [pl.BlockSpec((tm, tk), lambda i, k: (i, k)),
                     pl.BlockSpec((tk, n), lambda i, k, blk0=row0 // tk: (blk0 + k, 0))]
        args += [a, b]
    row_spec = pl.BlockSpec((tm, n), lambda i, k: (i, 0))
    in_specs += [row_spec, pl.BlockSpec((1, n), lambda i, k: (0, 0))]
    out_specs, out_shape = [row_spec], [jax.ShapeDtypeStruct((m, n), h_dtype)]
    if emit_x:
        out_specs, out_shape = [row_spec] + out_specs, [jax.ShapeDtypeStruct((m, n), F32)] + out_shape
    outs = pl.pallas_call(
        functools.partial(_mm_rows_body, n_pairs=len(pairs), nk=nk, emit_x=emit_x),
        grid=(m // tm, nk),
        in_specs=in_specs,
        out_specs=out_specs,
        out_shape=out_shape,
        scratch_shapes=[pltpu.VMEM((tm, n), F32)] if nk > 1 else [],
        compiler_params=_cparams(("arbitrary", "arbitrary")),
        name=name,
    )(*args, res, g.reshape(1, n))
    return tuple(outs) if emit_x else outs[0]


def _ssd_body(z_ref, xs_ref, bc_ref, m_ref, cwx_ref, cbx_ref, cwbc_ref, cbbc_ref, dtb_ref,
              alog_ref, dvec_ref, ng_ref, esel_ref, o_ref, xbuf, bcbuf, st_ref):
    q = SSD_CHUNK
    n = SSD_STATE
    gw = st_ref.shape[2]
    c = pl.program_id(1)

    @pl.when(c == 0)
    def _():
        xbuf[0:SUBLANES, :] = jnp.zeros((SUBLANES, xbuf.shape[1]), F32)
        bcbuf[0:SUBLANES, :] = jnp.zeros((SUBLANES, bcbuf.shape[1]), F32)
        st_ref[...] = jnp.zeros(st_ref.shape, F32)

    @pl.when(c > 0)
    def _():
        xbuf[0:SUBLANES, :] = xbuf[q:q + SUBLANES, :]
        bcbuf[0:SUBLANES, :] = bcbuf[q:q + SUBLANES, :]

    xbuf[SUBLANES:SUBLANES + q, :] = xs_ref[...]
    bcbuf[SUBLANES:SUBLANES + q, :] = bc_ref[...]

    def conv_silu(buf, w_ref, b_ref):
        acc = None
        for k in range(SSD_CONV):
            off = SUBLANES - (SSD_CONV - 1) + k
            t = buf[off:off + q, :] * w_ref[k:k + 1, :]
            acc = t if acc is None else acc + t
        acc = acc + b_ref[...]
        return acc * _sigmoid(acc)

    xc = conv_silu(xbuf, cwx_ref, cbx_ref)
    bcc = conv_silu(bcbuf, cwbc_ref, cbbc_ref)
    g_n = SSD_GROUPS * n

    lane = lax.broadcasted_iota(jnp.int32, (1, LANES), 1)
    n_heads = SSD_GROUPS * gw // HEAD
    dmask = (lane >= SSD_DT_LANE) & (lane < SSD_DT_LANE + n_heads)
    dt = jnp.where(dmask, _softplus(m_ref[...] + dtb_ref[...]), 0.0)
    a = jnp.where(dmask, -jnp.exp(alog_ref[...]), 0.0)
    da = dt * a
    row = lax.broadcasted_iota(jnp.int32, (q, q), 0)
    col = lax.broadcasted_iota(jnp.int32, (q, q), 1)
    tri = row >= col
    tril = jnp.where(tri, 1.0, 0.0).astype(MXU_DTYPE)
    cs = _dot_split_rhs(tril, da, 3)
    ecs = jnp.exp(cs)
    dte = jnp.exp(cs[q - 1:q, :] - cs)
    esel = esel_ref[...]
    dt_e = _dot_split_lhs(dt, esel, 2)
    ecs_e = _dot_split_lhs(ecs, esel, 2)
    dte_e = _dot_split_lhs(dte, esel, 2)
    cs_t = cs.T

    xdt = xc * dt_e
    xdt_m = _mx(xdt)
    xd_m = _mx(xdt * dte_e)
    lane_lo = lane < HEAD

    y_cols = []
    for g in range(SSD_GROUPS):
        bg = bcc[:, g * n:(g + 1) * n]
        cg = _mx(bcc[:, g_n + g * n:g_n + (g + 1) * n])
        cb = _dot_nt(cg, _mx(bg))
        gs = slice(g * gw, (g + 1) * gw)
        st = st_ref[g]
        y_off = _dot(cg, _mx(st)) * ecs_e[:, gs]
        st_ref[g] = st * ecs_e[q - 1:q, gs] + _dot(_mx(bg.T), xd_m[:, gs])
        for pr in range(gw // LANES):
            h0 = (g * gw + pr * LANES) // HEAD
            ps = slice(g * gw + pr * LANES, g * gw + (pr + 1) * LANES)
            res = []
            for hh in (h0, h0 + 1):
                li = SSD_DT_LANE + hh
                seg = cs[:, li:li + 1] - cs_t[li:li + 1, :]
                lm = jnp.where(tri, jnp.exp(jnp.where(tri, seg, 0.0)), 0.0)
                res.append(_dot(_mx(cb * lm), xdt_m[:, ps]))
            y_diag = jnp.where(lane_lo, res[0], res[1])
            y_cols.append(y_diag + y_off[:, pr * LANES:(pr + 1) * LANES])
    y = jnp.concatenate(y_cols, axis=1) + xc * dvec_ref[...]
    zz = z_ref[...]
    y = y * (zz * _sigmoid(zz))
    outs = []
    for g in range(SSD_GROUPS):
        yg = y[:, g * gw:(g + 1) * gw]
        ms = jnp.mean(yg * yg, axis=-1, keepdims=True)
        outs.append(yg * lax.rsqrt(ms + NORM_EPS))
    y = jnp.concatenate(outs, axis=1) * ng_ref[...]
    o_ref[...] = y.astype(o_ref.dtype)


def _ssd(u1, u2, bsz, seq, conv_w, conv_b, dt_bias, a_log, d_skip, norm_g):
    q = SSD_CHUNK
    nc = seq // q
    w = W_MIX
    gw = w // SSD_GROUPS
    heads = w // HEAD
    bcw = SSD_BC
    cw = conv_w[:, 0, :]
    pad = lambda v: jnp.zeros((1, LANES), F32).at[0, SSD_DT_LANE:SSD_DT_LANE + heads].set(v)
    esel = np.zeros((LANES, w), np.float32)
    for h in range(heads):
        esel[SSD_DT_LANE + h, h * HEAD:(h + 1) * HEAD] = 1.0
    rowblk = lambda cb: (lambda b, c: (b * nc + c, cb))
    const = lambda b, c: (0, 0)
    return pl.pallas_call(
        _ssd_body,
        grid=(bsz, nc),
        in_specs=[pl.BlockSpec((q, w), rowblk(0)),
                  pl.BlockSpec((q, w), rowblk(1)),
                  pl.BlockSpec((q, bcw), rowblk(2 * w // bcw)),
                  pl.BlockSpec((q, LANES), rowblk((U2_MISC + MISC_DT) // LANES)),
                  pl.BlockSpec((SSD_CONV, w), const), pl.BlockSpec((1, w), const),
                  pl.BlockSpec((SSD_CONV, bcw), const), pl.BlockSpec((1, bcw), const),
                  pl.BlockSpec((1, LANES), const), pl.BlockSpec((1, LANES), const),
                  pl.BlockSpec((1, w), const), pl.BlockSpec((1, w), const),
                  pl.BlockSpec((LANES, w), const)],
        out_specs=pl.BlockSpec((q, w), lambda b, c: (b * nc + c, 0)),
        out_shape=jax.ShapeDtypeStruct((bsz * seq, w), MXU_DTYPE),
        scratch_shapes=[pltpu.VMEM((q + SUBLANES, w), F32),
                        pltpu.VMEM((q + SUBLANES, bcw), F32),
                        pltpu.VMEM((SSD_GROUPS, SSD_STATE, gw), F32)],
        compiler_params=_cparams(("arbitrary", "arbitrary")),
        name="ssd_mixer",
    )(u1, u1, u1, u2, cw[:, :w], conv_b[:w].reshape(1, w), cw[:, w:], conv_b[w:].reshape(1, bcw),
      pad(dt_bias), pad(a_log), jnp.repeat(d_skip, HEAD).reshape(1, w), norm_g.reshape(1, w),
      jnp.asarray(esel, MXU_DTYPE))


def _unit_lower_inverse(l_mats, blk_mask, eye_f):
    mm = lambda xs, ys: [_dot(x, y) for x, y in zip(xs, ys)]
    mx = lambda xs: [_mx(x) for x in xs]
    add = lambda xs, ys: [x + y for x, y in zip(xs, ys)]
    d = [jnp.where(blk_mask, l, 0.0) for l in l_mats]
    o = mx([l - x for l, x in zip(l_mats, d)])
    d1 = mx(d)
    d2 = mx(mm(d1, d1))
    t = [eye_f + x for x in d]
    yield
    d4 = mx(mm(d2, d2))
    t = add(t, mm(d2, mx(t)))
    yield
    d8 = mx(mm(d4, d4))
    t = add(t, mm(d4, mx(t)))
    yield
    t_d = add(t, mm(d8, mx(t)))
    t_dm = mx(t_d)
    yield
    m1 = mm(t_dm, o)
    m1m = mx(m1)
    yield
    m2 = mm(m1m, m1m)
    yield
    m3 = mm(m1m, mx(m2))
    w = mx([x + y + z for x, y, z in zip(m1, m2, m3)])
    yield
    return add(t_d, mm(w, t_dm))


def _rwkv_masks():
    lane = lax.broadcasted_iota(jnp.int32, (1, LANES), 1)
    row = lax.broadcasted_iota(jnp.int32, (LANES, LANES), 0)
    col = lax.broadcasted_iota(jnp.int32, (LANES, LANES), 1)
    same = (row >= HEAD) == (col >= HEAD)
    rs = row & (HEAD - 1)
    cs = col & (HEAD - 1)
    eye = row == col
    return dict(lane_lo=lane < HEAD, strict=same & (rs > cs), incl=same & (rs >= cs),
                blk=(row >> 4) == (col >> 4), eye=eye, eye_f=jnp.where(eye, 1.0, 0.0))


def _rwkv_head_sum(x, ones_blk):
    cols = [_dot_split_lhs(x[:, t * LANES:(t + 1) * LANES], ones_blk, 2)
            for t in range(x.shape[1] // LANES)]
    return jnp.concatenate(cols, axis=1)


def _rwkv_prepare(src, buf, mu_ref, prm, masks, first):
    cl = RWKV_CHUNK
    w0_ref, w2_ref, a0_ref, a2_ref, g2_ref, kk_ref, ka_ref, rk_ref, ones_ref = prm
    if first:
        buf[0:SUBLANES, :] = jnp.zeros((SUBLANES, buf.shape[1]), F32)
    else:
        buf[0:SUBLANES, :] = buf[cl:cl + SUBLANES, :]
    cur = src[...]
    buf[SUBLANES:SUBLANES + cl, :] = cur
    prev = buf[SUBLANES - 1:SUBLANES - 1 + cl, :]
    row = cur + (prev - cur) * mu_ref[...]
    w = W_MIX
    r, k, v = (row[:, i * w:(i + 1) * w] for i in range(3))
    misc = row[:, U2_MISC:]
    pw_pa = misc[:, 0:2 * LANES]
    ones_blk = ones_ref[...]
    yield
    w_log = -_softplus(-(w0_ref[...] + _dot(_mx(jnp.tanh(pw_pa)), w2_ref[...]))) - 0.5
    lw = -jnp.exp(w_log)
    iclr = _sigmoid(a0_ref[...] + _dot(_mx(pw_pa), a2_ref[...]))
    gate = _dot(_mx(_sigmoid(misc[:, LANES:])), g2_ref[...])
    yield

    row_c = lax.broadcasted_iota(jnp.int32, (cl, cl), 0)
    col_c = lax.broadcasted_iota(jnp.int32, (cl, cl), 1)
    tril_c = jnp.where(row_c >= col_c, 1.0, 0.0).astype(MXU_DTYPE)
    lane_lo = masks["lane_lo"]
    stack2 = lambda x: jnp.concatenate([jnp.where(lane_lo, x, 0.0), jnp.where(lane_lo, 0.0, x)], axis=0)
    out = dict(a2=[], r2=[], b2=[], k2=[], v2=[], bkp=[])
    p_ends, bonuses = [], []
    for p in range(r.shape[1] // LANES):
        ps = slice(p * LANES, (p + 1) * LANES)
        r_p, k_p, v_p, lw_p, iclr_p = r[:, ps], k[:, ps], v[:, ps], lw[:, ps], iclr[:, ps]
        kk = k_p * kk_ref[:, ps]
        kk = kk / jnp.maximum(jnp.sqrt(_dot_split_lhs(kk * kk, ones_blk, 2)), 1e-12)
        k_p = k_p * (1.0 + (iclr_p - 1.0) * ka_ref[:, ps])
        bonuses.append(_dot_split_lhs(r_p * k_p * rk_ref[:, ps], ones_blk, 2) * v_p)
        cum = _dot_split_rhs(tril_c, lw_p, 3)
        p_in = jnp.exp(cum)
        p_inv = jnp.exp(-cum)
        a_t = -(kk * jnp.exp(cum - lw_p))
        b_t = kk * iclr_p * p_inv
        k_t = k_p * p_inv
        p_end = p_in[cl - 1:cl, :]
        out["a2"].append(_mx(stack2(a_t)))
        out["r2"].append(_mx(stack2(r_p * p_in)))
        out["b2"].append(_mx(stack2(b_t)))
        out["k2"].append(_mx(stack2(k_t)))
        out["v2"].append(_mx(stack2(v_p)))
        out["bkp"].append(_mx(jnp.concatenate([stack2(b_t * p_end), stack2(k_t * p_end)], axis=0).T))
        p_ends.append(p_end)
        yield
    out.update(p_end=jnp.concatenate(p_ends, axis=1), bonus=jnp.concatenate(bonuses, axis=1), gate=gate)
    return out


def _rwkv_scan(ops_list, st_ref, ybuf, lnw_ref, lnb_ref, ones_ref, masks):
    cl = RWKV_CHUNK
    gather = lambda name: [x for ops in ops_list for x in ops[name]]
    a2, r2, b2, k2, v2, bkp_t = (gather(name) for name in _RWKV_STAGED)
    n_tiles = len(ops_list[0]["a2"])
    tiles = range(len(a2))
    cat0 = lambda *xs: jnp.concatenate(xs, axis=0)
    cat1 = lambda *xs: jnp.concatenate(xs, axis=1)
    gram = [_dot_nt(cat0(a2[t], r2[t]), cat0(b2[t], k2[t])) for t in tiles]
    yield
    a_ab = [jnp.where(masks["strict"], g[0:LANES, 0:LANES], 0.0) for g in gram]
    a_ak = [_mx(jnp.where(masks["strict"], g[0:LANES, LANES:], 0.0)) for g in gram]
    a_rb = [_mx(jnp.where(masks["incl"], g[LANES:, 0:LANES], 0.0)) for g in gram]
    a_rk = [_mx(jnp.where(masks["incl"], g[LANES:, LANES:], 0.0)) for g in gram]
    t_inv = yield from _unit_lower_inverse(a_ab, masks["blk"], masks["eye_f"])
    st = [st_ref[t] for t in tiles]
    st_m = [_mx(s) for s in st]
    y0 = [_dot(cat1(a2[t], a_ak[t]), cat0(st_m[t], v2[t])) for t in tiles]
    yield
    sa = [_mx(_dot(_mx(t_inv[t]), _mx(y0[t]))) for t in tiles]
    yield
    o2 = [_dot(cat1(r2[t], a_rb[t], a_rk[t]), cat0(st_m[t], sa[t], v2[t])) for t in tiles]
    yield
    upd = [_dot(bkp_t[t], cat0(sa[t], v2[t])) for t in tiles]
    for t in tiles:
        b, p = divmod(t, n_tiles)
        ps = slice(p * LANES, (p + 1) * LANES)
        ybuf[b, :, ps] = o2[t][0:cl, :] + o2[t][cl:, :]
        p_end = ops_list[b]["p_end"][:, ps]
        p_col = jnp.sum(jnp.where(masks["eye"], p_end, 0.0), axis=1, keepdims=True)
        st_ref[t] = st[t] * p_col + upd[t]
    yield
    ones_blk = ones_ref[...]
    inv_n = 1.0 / HEAD
    outs = []
    for b, ops in enumerate(ops_list):
        y = ybuf[b]
        mean = _rwkv_head_sum(y, ones_blk) * inv_n
        d = y - mean
        var = _rwkv_head_sum(d * d, ones_blk) * inv_n
        y = d * lax.rsqrt(var + RWKV_LN_EPS) * lnw_ref[...] + lnb_ref[...]
        outs.append((y + ops["bonus"]) * ops["gate"])
    return outs


def _run(gen):
    while True:
        try:
            next(gen)
        except StopIteration as stop:
            return stop.value


def _interleave(scan_gen, prep_gens):
    gens = [scan_gen] + list(prep_gens)
    results, done = [None] * len(gens), [False] * len(gens)
    while not all(done):
        for i, gen in enumerate(gens):
            if not done[i]:
                try:
                    next(gen)
                except StopIteration as stop:
                    results[i], done[i] = stop.value, True
    return results[0], results[1:]


_RWKV_STAGED = ("a2", "r2", "b2", "k2", "v2", "bkp")
RWKV_SEQS_PER_STEP = 2


def _rwkv_body(*refs):
    cl = RWKV_CHUNK
    first_src, odd_src, next_src, mu_ref = refs[0:4]
    prm = refs[4:12] + (refs[14],)
    lnw_ref, lnb_ref, ones_ref = refs[12], refs[13], refs[14]
    o_ref = refs[15]
    buf = refs[16]
    st_ref, ybuf = refs[17], refs[18]
    staged = dict(zip(_RWKV_STAGED, refs[19:25]))
    s_pend, s_bonus, s_gate = refs[25], refs[26], refs[27]
    masks = _rwkv_masks()
    seqs = range(o_ref.shape[0])
    n_tiles = st_ref.shape[0] // len(seqs)

    def prepare(src, b, first):
        return _rwkv_prepare(src.at[b], buf.at[b], mu_ref, prm, masks, first)

    def stash(b, ops):
        for name in _RWKV_STAGED:
            for p in range(n_tiles):
                staged[name][b * n_tiles + p] = ops[name][p]
        s_pend[b] = ops["p_end"]
        s_bonus[b] = ops["bonus"]
        s_gate[b] = ops["gate"]

    def staged_ops(b):
        ops = {name: [staged[name][b * n_tiles + p] for p in range(n_tiles)] for name in _RWKV_STAGED}
        ops.update(p_end=s_pend[b], bonus=s_bonus[b], gate=s_gate[b])
        return ops

    @pl.when(pl.program_id(1) == 0)
    def _():
        st_ref[...] = jnp.zeros(st_ref.shape, F32)
        for b in seqs:
            stash(b, _run(prepare(first_src, b, True)))

    scan = lambda ops_list: _rwkv_scan(ops_list, st_ref, ybuf, lnw_ref, lnb_ref, ones_ref, masks)
    y_even, odd = _interleave(scan([staged_ops(b) for b in seqs]), [prepare(odd_src, b, False) for b in seqs])
    for b in seqs:
        o_ref[b, 0:cl, :] = y_even[b].astype(o_ref.dtype)
    y_odd, nxt = _interleave(scan(odd), [prepare(next_src, b, False) for b in seqs])
    for b in seqs:
        o_ref[b, cl:2 * cl, :] = y_odd[b].astype(o_ref.dtype)
        stash(b, nxt[b])


def _rwkv(u2, bsz, seq, mu, w0, w2, a0, a2, g2, k_k, k_a, r_k, ln_w, ln_b):
    cl = RWKV_CHUNK
    nc = seq // cl
    nb = RWKV_SEQS_PER_STEP
    assert nc % 2 == 0 and bsz % nb == 0
    w = W_MIX
    mw = RWKV_MISC
    uw = u2.shape[1]
    n_tiles = w // LANES
    row1 = lambda vec: vec.reshape(1, -1)
    mu_row = jnp.zeros((uw,), F32).at[0:mu.shape[0]].set(mu)
    w2e = jnp.zeros((2 * LANES, w), F32).at[MISC_PW:MISC_PA].set(w2)
    a2e = jnp.zeros((2 * LANES, w), F32).at[MISC_PA:MISC_PG].set(a2)
    g2e = jnp.zeros((mw - LANES, w), F32).at[MISC_PG - LANES:MISC_DT - LANES].set(g2)
    ones_blk = np.kron(np.eye(LANES // HEAD, dtype=np.float32), np.ones((HEAD, HEAD), np.float32))

    def chunk_spec(chunk_of):
        return pl.BlockSpec((nb, cl, uw), lambda bb, j: (bb, chunk_of(j), 0))

    cvec = lambda width: pl.BlockSpec((1, width), lambda bb, j: (0, 0))
    cmat = lambda rows, width: pl.BlockSpec((rows, width), lambda bb, j: (0, 0))
    stage = lambda cols: pltpu.VMEM((nb * n_tiles, LANES, cols), MXU_DTYPE)
    out = pl.pallas_call(
        _rwkv_body,
        grid=(bsz // nb, nc // 2),
        in_specs=([chunk_spec(lambda j: 0), chunk_spec(lambda j: 2 * j + 1),
                   chunk_spec(lambda j: jnp.minimum(2 * j + 2, nc - 1)), cvec(uw)]
                  + [cvec(w), cmat(2 * LANES, w), cvec(w), cmat(2 * LANES, w), cmat(mw - LANES, w),
                     cvec(w), cvec(w), cvec(w), cvec(w), cvec(w), cmat(LANES, LANES)]),
        out_specs=pl.BlockSpec((nb, 2 * cl, w), lambda bb, j: (bb, j, 0)),
        out_shape=jax.ShapeDtypeStruct((bsz, seq, w), MXU_DTYPE),
        scratch_shapes=[pltpu.VMEM((nb, cl + SUBLANES, uw), F32),
                        pltpu.VMEM((nb * n_tiles, LANES, LANES), F32),
                        pltpu.VMEM((nb, cl, w), F32),
                        stage(LANES), stage(LANES), stage(LANES), stage(LANES), stage(LANES),
                        stage(2 * LANES),
                        pltpu.VMEM((nb, 1, w), F32), pltpu.VMEM((nb, cl, w), F32),
                        pltpu.VMEM((nb, cl, w), F32)],
        compiler_params=_cparams(("arbitrary", "arbitrary")),
        name="rwkv7_mixer",
    )(*([u2.reshape(bsz, seq, uw)] * 3), row1(mu_row), row1(w0), _mx(w2e), row1(a0), _mx(a2e), _mx(g2e), row1(k_k), row1(k_a),
      row1(r_k), row1(ln_w), row1(ln_b), jnp.asarray(ones_blk, MXU_DTYPE))
    return out.reshape(bsz * seq, w)


def _attn_body(q_ref, k_ref, v_ref, o_ref):
    d = q_ref.shape[1]
    hd = d // XATTN_HEADS
    scale = hd ** -0.5
    heads = [slice(h * hd, (h + 1) * hd) for h in range(XATTN_HEADS)]
    scores = [_dot_nt(q_ref[:, hs], k_ref[0, :, hs]) * scale for hs in heads]
    probs = []
    for s in scores:
        e = jnp.exp(s - jnp.max(s, axis=-1, keepdims=True))
        probs.append(_mx(e / jnp.sum(e, axis=-1, keepdims=True)))
    for hs, p in zip(heads, probs):
        o_ref[:, hs] = _dot(p, v_ref[0, :, hs]).astype(o_ref.dtype)


def _attention(q, k, v, bsz, seq, ts=512):
    d = q.shape[1]
    n_mem = k.shape[1]
    ts = min(ts, seq)
    ns = seq // ts
    return pl.pallas_call(
        _attn_body,
        grid=(bsz, ns),
        in_specs=[pl.BlockSpec((ts, d), lambda b, s: (b * ns + s, 0)),
                  pl.BlockSpec((1, n_mem, d), lambda b, s: (b, 0, 0)),
                  pl.BlockSpec((1, n_mem, d), lambda b, s: (b, 0, 0))],
        out_specs=pl.BlockSpec((ts, d), lambda b, s: (b * ns + s, 0)),
        out_shape=jax.ShapeDtypeStruct((bsz * seq, d), MXU_DTYPE),
        compiler_params=_cparams(("arbitrary", "arbitrary")),
        name="mem_attention",
    )(q, k, v)


def _rwkv_side(mat, axis):
    ssd_in = U1_COLS + W_MIX // HEAD
    take = lambda lo, hi: lax.slice_in_dim(mat, lo, hi, axis=axis)
    pad_shape = list(mat.shape)
    pad_shape[axis] = U2_COLS - (mat.shape[axis] - U1_COLS)
    return jnp.concatenate([take(ssd_in, mat.shape[axis]), take(U1_COLS, ssd_in),
                            jnp.zeros(pad_shape, mat.dtype)], axis=axis)


def kernel(x, mem, norm_mix_g, w_in, ssd_conv_w, ssd_conv_b, ssd_dt_bias, ssd_a_log, ssd_d, ssd_norm_g, rwkv_mu, rwkv_w0, rwkv_w2, rwkv_a0, rwkv_a2, rwkv_g2, rwkv_k_k, rwkv_k_a, rwkv_r_k, rwkv_ln_w, rwkv_ln_b, w_out, norm_x_g, norm_mem_g, xattn_wq, xattn_wk, xattn_wv, xattn_wo, norm_ffn_g, ffn_w1, ffn_w2, final_norm_g):
    bsz, seq, d = x.shape
    n_mem = mem.shape[1]
    xr = x.reshape(bsz * seq, d)
    memr = mem.reshape(bsz * n_mem, d)
    n_layers = w_in.shape[0]
    for l in range(n_layers):
        w_in_t = w_in[l].T
        u1 = _matmul_resident(xr, w_in_t, F32, n=U1_COLS, b_is_transposed=True,
                              norm_g=norm_mix_g[l], tm=512, tn=U1_COLS // 2, name="in_proj_ssd")
        u2 = _matmul_resident(xr, _rwkv_side(w_in_t, 0), F32, b_is_transposed=True,
                              norm_g=norm_mix_g[l], tm=512, tn=U2_COLS // 2, name="in_proj_rwkv")
        y_ssd = _ssd(u1, u2, bsz, seq, ssd_conv_w[l], ssd_conv_b[l], ssd_dt_bias[l], ssd_a_log[l],
                     ssd_d[l], ssd_norm_g[l])
        y_rwkv = _rwkv(u2, bsz, seq, rwkv_mu[l], rwkv_w0[l], rwkv_w2[l], rwkv_a0[l], rwkv_a2[l],
                       rwkv_g2[l], rwkv_k_k[l], rwkv_k_a[l], rwkv_r_k[l].reshape(-1),
                       rwkv_ln_w[l], rwkv_ln_b[l])
        wo = _mx(w_out[l])
        xr, h = _matmul_rows([(y_ssd, wo, 0), (y_rwkv, wo, W_MIX)], xr, norm_x_g[l],
                             MXU_DTYPE, name="out_proj")

        m = _rmsnorm(memr, norm_mem_g[l], MXU_DTYPE)
        q = _matmul_resident(h, xattn_wq[l], MXU_DTYPE, tm=2048, name="q_proj")
        kx = _matmul_resident(m, xattn_wk[l], MXU_DTYPE, name="k_proj")
        vx = _matmul_resident(m, xattn_wv[l], MXU_DTYPE, name="v_proj")
        o = _attention(q, kx.reshape(bsz, n_mem, d), vx.reshape(bsz, n_mem, d), bsz, seq)
        xr, h = _matmul_rows([(o, _mx(xattn_wo[l]), 0)], xr, norm_ffn_g[l], MXU_DTYPE, name="o_proj")

        hid = _matmul_resident(h, ffn_w1[l], MXU_DTYPE, act="relu2", tm=2048, name="ffn_up")
        down = [(hid, _mx(ffn_w2[l]), 0)]
        if l + 1 < n_layers:
            xr, _ = _matmul_rows(down, xr, norm_mix_g[l + 1], MXU_DTYPE, name="ffn_down")
        else:
            out = _matmul_rows(down, xr, final_norm_g, x.dtype, emit_x=False, name="ffn_down")
    return out.reshape(bsz, seq, d)
```

```python
import functools

import numpy as np
import jax
import jax.numpy as jnp
from jax import lax
from jax.experimental import pallas as pl
from jax.experimental.pallas import tpu as pltpu

F32 = jnp.float32
MXU_DTYPE = jnp.bfloat16

NORM_EPS = 1e-6
RWKV_LN_EPS = 64e-5

HEAD = 64
SSD_STATE = 128
SSD_GROUPS = 2
SSD_CHUNK = 128
SSD_CONV = 4
RWKV_CHUNK = 64
RWKV_DECAY_RANK = 96
RWKV_AAA_RANK = 96
RWKV_GATE_RANK = 256
XATTN_HEADS = 4

LANES = 128
SUBLANES = 8
VMEM_LIMIT = 56 * 1024 * 1024

W_MIX = 1024
SSD_BC = 2 * SSD_GROUPS * SSD_STATE
U1_COLS = 2 * W_MIX + SSD_BC
U2_MISC = 3 * W_MIX
RWKV_MISC = 512
U2_COLS = U2_MISC + RWKV_MISC
MISC_PW = 0
MISC_PA = MISC_PW + RWKV_DECAY_RANK
MISC_PG = MISC_PA + RWKV_AAA_RANK
MISC_DT = MISC_PG + RWKV_GATE_RANK
SSD_DT_LANE = MISC_DT % LANES


def _mx(a):
    return a.astype(MXU_DTYPE)


def _dot(a, b):
    return jnp.dot(a, b, preferred_element_type=F32)


def _dot_nt(a, b):
    return lax.dot_general(a, b, (((1,), (1,)), ((), ())), preferred_element_type=F32)


def _split(v, parts):
    out = []
    rem = v
    for _ in range(parts):
        p = rem.astype(MXU_DTYPE)
        out.append(p)
        rem = rem - p.astype(F32)
    return out


def _dot_split_rhs(a01, v, parts):
    acc = None
    for p in _split(v, parts):
        t = _dot(a01, p)
        acc = t if acc is None else acc + t
    return acc


def _dot_split_lhs(v, b01, parts):
    acc = None
    for p in _split(v, parts):
        t = _dot(p, b01)
        acc = t if acc is None else acc + t
    return acc


def _sigmoid(x):
    return 1.0 / (1.0 + jnp.exp(-x))


def _softplus(x):
    return jnp.maximum(x, 0.0) + jnp.log(1.0 + jnp.exp(-jnp.abs(x)))


def _cparams(sem):
    return pltpu.CompilerParams(dimension_semantics=sem, vmem_limit_bytes=VMEM_LIMIT)


def _rmsnorm_body(x_ref, g_ref, o_ref):
    x = x_ref[...]
    ms = jnp.mean(x * x, axis=-1, keepdims=True)
    o_ref[...] = (x * lax.rsqrt(ms + NORM_EPS) * g_ref[...]).astype(o_ref.dtype)


def _rmsnorm(x, g, out_dtype, tm=512):
    m, d = x.shape
    tm = min(tm, m)
    return pl.pallas_call(
        _rmsnorm_body,
        grid=(m // tm,),
        in_specs=[pl.BlockSpec((tm, d), lambda i: (i, 0)),
                  pl.BlockSpec((1, d), lambda i: (0, 0))],
        out_specs=pl.BlockSpec((tm, d), lambda i: (i, 0)),
        out_shape=jax.ShapeDtypeStruct((m, d), out_dtype),
        compiler_params=_cparams(("arbitrary",)),
        name="rmsnorm",
    )(x, g.reshape(1, d))


def _mm_resident_body(*refs, act, b_is_transposed, normalize):
    if normalize:
        a_ref, g_ref, b_ref, o_ref, bm_ref = refs
    else:
        a_ref, b_ref, o_ref, bm_ref = refs

    @pl.when(pl.program_id(1) == 0)
    def _():
        b = b_ref[...]
        bm_ref[...] = (b.T if b_is_transposed else b).astype(MXU_DTYPE)

    a = a_ref[...]
    if normalize:
        ms = jnp.mean(a * a, axis=-1, keepdims=True)
        a = (a * lax.rsqrt(ms + NORM_EPS) * g_ref[...]).astype(MXU_DTYPE)
    r = _dot(a, bm_ref[...])
    if act == "relu2":
        r = jnp.square(jnp.maximum(r, 0.0))
    o_ref[...] = r.astype(o_ref.dtype)


def _matmul_resident(a, b, out_dtype, act=None, n=None, b_is_transposed=False, norm_g=None,
                     tm=1024, tn=1024, name="matmul_resident"):
    m, kdim = a.shape
    n = b.shape[0 if b_is_transposed else 1] if n is None else n
    tm, tn = min(tm, m), min(tn, n)
    assert m % tm == 0 and n % tn == 0
    b_spec = (pl.BlockSpec((tn, kdim), lambda j, i: (j, 0)) if b_is_transposed
              else pl.BlockSpec((kdim, tn), lambda j, i: (0, j)))
    in_specs, args = [pl.BlockSpec((tm, kdim), lambda j, i: (i, 0))], [a]
    if norm_g is not None:
        in_specs.append(pl.BlockSpec((1, kdim), lambda j, i: (0, 0)))
        args.append(norm_g.reshape(1, kdim))
    return pl.pallas_call(
        functools.partial(_mm_resident_body, act=act, b_is_transposed=b_is_transposed,
                          normalize=norm_g is not None),
        grid=(n // tn, m // tm),
        in_specs=in_specs + [b_spec],
        out_specs=pl.BlockSpec((tm, tn), lambda j, i: (i, j)),
        out_shape=jax.ShapeDtypeStruct((m, n), out_dtype),
        scratch_shapes=[pltpu.VMEM((kdim, tn), MXU_DTYPE)],
        compiler_params=_cparams(("arbitrary", "arbitrary")),
        name=name,
    )(*args, b)


def _mm_rows_body(*refs, n_pairs, nk, emit_x):
    ab = refs[:2 * n_pairs]
    res_ref, g_ref = refs[2 * n_pairs:2 * n_pairs + 2]
    outs = refs[2 * n_pairs + 2:]
    x_ref = outs[0] if emit_x else None
    h_ref = outs[int(emit_x)]
    acc_ref = outs[int(emit_x) + 1] if nk > 1 else None
    part = None
    for p in range(n_pairs):
        t = _dot(ab[2 * p][...], ab[2 * p + 1][...])
        part = t if part is None else part + t

    def finish(acc):
        x = res_ref[...] + acc
        if emit_x:
            x_ref[...] = x
        ms = jnp.mean(x * x, axis=-1, keepdims=True)
        h_ref[...] = (x * lax.rsqrt(ms + NORM_EPS) * g_ref[...]).astype(h_ref.dtype)

    if nk == 1:
        finish(part)
        return

    k = pl.program_id(1)

    @pl.when(k == 0)
    def _():
        acc_ref[...] = part

    @pl.when(k > 0)
    def _():
        acc_ref[...] += part

    @pl.when(k == nk - 1)
    def _():
        finish(acc_ref[...])


def _matmul_rows(pairs, res, g, h_dtype, emit_x=True, tm=512, tk=2048, name="matmul_rows"):
    m, n = res.shape
    tm = min(tm, m)
    kdims = [a.shape[1] for a, _, _ in pairs]
    tk = min(tk, min(kdims))
    nk = kdims[0] // tk
    assert all(kd == nk * tk for kd in kdims) and all(row0 % tk == 0 for _, _, row0 in pairs)
    in_specs, args = [], []
    for a, b, row0 in pairs:
        in_specs += [pl.BlockSpec((tm, tk), lambda i, k: (i, k)),
                     pl.BlockSpec((tk, n), lambda i, k, blk0=row0 // tk: (blk0 + k, 0))]
        args += [a, b]
    row_spec = pl.BlockSpec((tm, n), lambda i, k: (i, 0))
    in_specs += [row_spec, pl.BlockSpec((1, n), lambda i, k: (0, 0))]
    out_specs, out_shape = [row_spec], [jax.ShapeDtypeStruct((m, n), h_dtype)]
    if emit_x:
        out_specs, out_shape = [row_spec] + out_specs, [jax.ShapeDtypeStruct((m, n), F32)] + out_shape
    outs = pl.pallas_call(
        functools.partial(_mm_rows_body, n_pairs=len(pairs), nk=nk, emit_x=emit_x),
        grid=(m // tm, nk),
        in_specs=in_specs,
        out_specs=out_specs,
        out_shape=out_shape,
        scratch_shapes=[pltpu.VMEM((tm, n), F32)] if nk > 1 else [],
        compiler_params=_cparams(("arbitrary", "arbitrary")),
        name=name,
    )(*args, res, g.reshape(1, n))
    return tuple(outs) if emit_x else outs[0]


def _ssd_body(z_ref, xs_ref, bc_ref, m_ref, cwx_ref, cbx_ref, cwbc_ref, cbbc_ref, dtb_ref,
              alog_ref, dvec_ref, ng_ref, esel_ref, o_ref, xbuf, bcbuf, st_ref):
    q = SSD_CHUNK
    n = SSD_STATE
    gw = st_ref.shape[2]
    c = pl.program_id(1)

    @pl.when(c == 0)
    def _():
        xbuf[0:SUBLANES, :] = jnp.zeros((SUBLANES, xbuf.shape[1]), F32)
        bcbuf[0:SUBLANES, :] = jnp.zeros((SUBLANES, bcbuf.shape[1]), F32)
        st_ref[...] = jnp.zeros(st_ref.shape, F32)

    @pl.when(c > 0)
    def _():
        xbuf[0:SUBLANES, :] = xbuf[q:q + SUBLANES, :]
        bcbuf[0:SUBLANES, :] = bcbuf[q:q + SUBLANES, :]

    xbuf[SUBLANES:SUBLANES + q, :] = xs_ref[...]
    bcbuf[SUBLANES:SUBLANES + q, :] = bc_ref[...]

    def conv_silu(buf, w_ref, b_ref):
        acc = None
        for k in range(SSD_CONV):
            off = SUBLANES - (SSD_CONV - 1) + k
            t = buf[off:off + q, :] * w_ref[k:k + 1, :]
            acc = t if acc is None else acc + t
        acc = acc + b_ref[...]
        return acc * _sigmoid(acc)

    xc = conv_silu(xbuf, cwx_ref, cbx_ref)
    bcc = conv_silu(bcbuf, cwbc_ref, cbbc_ref)
    g_n = SSD_GROUPS * n

    lane = lax.broadcasted_iota(jnp.int32, (1, LANES), 1)
    n_heads = SSD_GROUPS * gw // HEAD
    dmask = (lane >= SSD_DT_LANE) & (lane < SSD_DT_LANE + n_heads)
    dt = jnp.where(dmask, _softplus(m_ref[...] + dtb_ref[...]), 0.0)
    a = jnp.where(dmask, -jnp.exp(alog_ref[...]), 0.0)
    da = dt * a
    row = lax.broadcasted_iota(jnp.int32, (q, q), 0)
    col = lax.broadcasted_iota(jnp.int32, (q, q), 1)
    tri = row >= col
    tril = jnp.where(tri, 1.0, 0.0).astype(MXU_DTYPE)
    cs = _dot_split_rhs(tril, da, 3)
    ecs = jnp.exp(cs)
    dte = jnp.exp(cs[q - 1:q, :] - cs)
    esel = esel_ref[...]
    dt_e = _dot_split_lhs(dt, esel, 2)
    ecs_e = _dot_split_lhs(ecs, esel, 2)
    dte_e = _dot_split_lhs(dte, esel, 2)
    cs_t = cs.T

    xdt = xc * dt_e
    xdt_m = _mx(xdt)
    xd_m = _mx(xdt * dte_e)
    lane_lo = lane < HEAD

    y_cols = []
    for g in range(SSD_GROUPS):
        bg = bcc[:, g * n:(g + 1) * n]
        cg = _mx(bcc[:, g_n + g * n:g_n + (g + 1) * n])
        cb = _dot_nt(cg, _mx(bg))
        gs = slice(g * gw, (g + 1) * gw)
        st = st_ref[g]
        y_off = _dot(cg, _mx(st)) * ecs_e[:, gs]
        st_ref[g] = st * ecs_e[q - 1:q, gs] + _dot(_mx(bg.T), xd_m[:, gs])
        for pr in range(gw // LANES):
            h0 = (g * gw + pr * LANES) // HEAD
            ps = slice(g * gw + pr * LANES, g * gw + (pr + 1) * LANES)
            res = []
            for hh in (h0, h0 + 1):
                li = SSD_DT_LANE + hh
                seg = cs[:, li:li + 1] - cs_t[li:li + 1, :]
                lm = jnp.where(tri, jnp.exp(jnp.where(tri, seg, 0.0)), 0.0)
                res.append(_dot(_mx(cb * lm), xdt_m[:, ps]))
            y_diag = jnp.where(lane_lo, res[0], res[1])
            y_cols.append(y_diag + y_off[:, pr * LANES:(pr + 1) * LANES])
    y = jnp.concatenate(y_cols, axis=1) + xc * dvec_ref[...]
    zz = z_ref[...]
    y = y * (zz * _sigmoid(zz))
    outs = []
    for g in range(SSD_GROUPS):
        yg = y[:, g * gw:(g + 1) * gw]
        ms = jnp.mean(yg * yg, axis=-1, keepdims=True)
        outs.append(yg * lax.rsqrt(ms + NORM_EPS))
    y = jnp.concatenate(outs, axis=1) * ng_ref[...]
    o_ref[...] = y.astype(o_ref.dtype)


def _ssd(u1, u2, bsz, seq, conv_w, conv_b, dt_bias, a_log, d_skip, norm_g):
    q = SSD_CHUNK
    nc = seq // q
    w = W_MIX
    gw = w // SSD_GROUPS
    heads = w // HEAD
    bcw = SSD_BC
    cw = conv_w[:, 0, :]
    pad = lambda v: jnp.zeros((1, LANES), F32).at[0, SSD_DT_LANE:SSD_DT_LANE + heads].set(v)
    esel = np.zeros((LANES, w), np.float32)
    for h in range(heads):
        esel[SSD_DT_LANE + h, h * HEAD:(h + 1) * HEAD] = 1.0
    rowblk = lambda cb: (lambda b, c: (b * nc + c, cb))
    const = lambda b, c: (0, 0)
    return pl.pallas_call(
        _ssd_body,
        grid=(bsz, nc),
        in_specs=[pl.BlockSpec((q, w), rowblk(0)),
                  pl.BlockSpec((q, w), rowblk(1)),
                  pl.BlockSpec((q, bcw), rowblk(2 * w // bcw)),
                  pl.BlockSpec((q, LANES), rowblk((U2_MISC + MISC_DT) // LANES)),
                  pl.BlockSpec((SSD_CONV, w), const), pl.BlockSpec((1, w), const),
                  pl.BlockSpec((SSD_CONV, bcw), const), pl.BlockSpec((1, bcw), const),
                  pl.BlockSpec((1, LANES), const), pl.BlockSpec((1, LANES), const),
                  pl.BlockSpec((1, w), const), pl.BlockSpec((1, w), const),
                  pl.BlockSpec((LANES, w), const)],
        out_specs=pl.BlockSpec((q, w), lambda b, c: (b * nc + c, 0)),
        out_shape=jax.ShapeDtypeStruct((bsz * seq, w), MXU_DTYPE),
        scratch_shapes=[pltpu.VMEM((q + SUBLANES, w), F32),
                        pltpu.VMEM((q + SUBLANES, bcw), F32),
                        pltpu.VMEM((SSD_GROUPS, SSD_STATE, gw), F32)],
        compiler_params=_cparams(("arbitrary", "arbitrary")),
        name="ssd_mixer",
    )(u1, u1, u1, u2, cw[:, :w], conv_b[:w].reshape(1, w), cw[:, w:], conv_b[w:].reshape(1, bcw),
      pad(dt_bias), pad(a_log), jnp.repeat(d_skip, HEAD).reshape(1, w), norm_g.reshape(1, w),
      jnp.asarray(esel, MXU_DTYPE))


def _unit_lower_inverse(l_mats, blk_mask, eye_f):
    mm = lambda xs, ys: [_dot(x, y) for x, y in zip(xs, ys)]
    mx = lambda xs: [_mx(x) for x in xs]
    add = lambda xs, ys: [x + y for x, y in zip(xs, ys)]
    d = [jnp.where(blk_mask, l, 0.0) for l in l_mats]
    o = mx([l - x for l, x in zip(l_mats, d)])
    d1 = mx(d)
    d2 = mx(mm(d1, d1))
    t = [eye_f + x for x in d]
    yield
    d4 = mx(mm(d2, d2))
    t = add(t, mm(d2, mx(t)))
    yield
    d8 = mx(mm(d4, d4))
    t = add(t, mm(d4, mx(t)))
    yield
    t_d = add(t, mm(d8, mx(t)))
    t_dm = mx(t_d)
    yield
    m1 = mm(t_dm, o)
    m1m = mx(m1)
    yield
    m2 = mm(m1m, m1m)
    yield
    m3 = mm(m1m, mx(m2))
    w = mx([x + y + z for x, y, z in zip(m1, m2, m3)])
    yield
    return add(t_d, mm(w, t_dm))


def _rwkv_masks():
    lane = lax.broadcasted_iota(jnp.int32, (1, LANES), 1)
    row = lax.broadcasted_iota(jnp.int32, (LANES, LANES), 0)
    col = lax.broadcasted_iota(jnp.int32, (LANES, LANES), 1)
    same = (row >= HEAD) == (col >= HEAD)
    rs = row & (HEAD - 1)
    cs = col & (HEAD - 1)
    eye = row == col
    return dict(lane_lo=lane < HEAD, strict=same & (rs > cs), incl=same & (rs >= cs),
                blk=(row >> 4) == (col >> 4), eye=eye, eye_f=jnp.where(eye, 1.0, 0.0))


def _rwkv_head_sum(x, ones_blk):
    cols = [_dot_split_lhs(x[:, t * LANES:(t + 1) * LANES], ones_blk, 2)
            for t in range(x.shape[1] // LANES)]
    return jnp.concatenate(cols, axis=1)


def _rwkv_prepare(src, buf, mu_ref, prm, masks, first):
    cl = RWKV_CHUNK
    w0_ref, w2_ref, a0_ref, a2_ref, g2_ref, kk_ref, ka_ref, rk_ref, ones_ref = prm
    if first:
        buf[0:SUBLANES, :] = jnp.zeros((SUBLANES, buf.shape[1]), F32)
    else:
        buf[0:SUBLANES, :] = buf[cl:cl + SUBLANES, :]
    cur = src[...]
    buf[SUBLANES:SUBLANES + cl, :] = cur
    prev = buf[SUBLANES - 1:SUBLANES - 1 + cl, :]
    row = cur + (prev - cur) * mu_ref[...]
    w = W_MIX
    r, k, v = (row[:, i * w:(i + 1) * w] for i in range(3))
    misc = row[:, U2_MISC:]
    pw_pa = misc[:, 0:2 * LANES]
    ones_blk = ones_ref[...]
    yield
    w_log = -_softplus(-(w0_ref[...] + _dot(_mx(jnp.tanh(pw_pa)), w2_ref[...]))) - 0.5
    lw = -jnp.exp(w_log)
    iclr = _sigmoid(a0_ref[...] + _dot(_mx(pw_pa), a2_ref[...]))
    gate = _dot(_mx(_sigmoid(misc[:, LANES:])), g2_ref[...])
    yield

    row_c = lax.broadcasted_iota(jnp.int32, (cl, cl), 0)
    col_c = lax.broadcasted_iota(jnp.int32, (cl, cl), 1)
    tril_c = jnp.where(row_c >= col_c, 1.0, 0.0).astype(MXU_DTYPE)
    lane_lo = masks["lane_lo"]
    stack2 = lambda x: jnp.concatenate([jnp.where(lane_lo, x, 0.0), jnp.where(lane_lo, 0.0, x)], axis=0)
    out = dict(a2=[], r2=[], b2=[], k2=[], v2=[], bkp=[])
    p_ends, bonuses = [], []
    for p in range(r.shape[1] // LANES):
        ps = slice(p * LANES, (p + 1) * LANES)
        r_p, k_p, v_p, lw_p, iclr_p = r[:, ps], k[:, ps], v[:, ps], lw[:, ps], iclr[:, ps]
        kk = k_p * kk_ref[:, ps]
        kk = kk / jnp.maximum(jnp.sqrt(_dot_split_lhs(kk * kk, ones_blk, 2)), 1e-12)
        k_p = k_p * (1.0 + (iclr_p - 1.0) * ka_ref[:, ps])
        bonuses.append(_dot_split_lhs(r_p * k_p * rk_ref[:, ps], ones_blk, 2) * v_p)
        cum = _dot_split_rhs(tril_c, lw_p, 3)
        p_in = jnp.exp(cum)
        p_inv = jnp.exp(-cum)
        a_t = -(kk * jnp.exp(cum - lw_p))
        b_t = kk * iclr_p * p_inv
        k_t = k_p * p_inv
        p_end = p_in[cl - 1:cl, :]
        out["a2"].append(_mx(stack2(a_t)))
        out["r2"].append(_mx(stack2(r_p * p_in)))
        out["b2"].append(_mx(stack2(b_t)))
        out["k2"].append(_mx(stack2(k_t)))
        out["v2"].append(_mx(stack2(v_p)))
        out["bkp"].append(_mx(jnp.concatenate([stack2(b_t * p_end), stack2(k_t * p_end)], axis=0).T))
        p_ends.append(p_end)
        yield
    out.update(p_end=jnp.concatenate(p_ends, axis=1), bonus=jnp.concatenate(bonuses, axis=1), gate=gate)
    return out


def _rwkv_scan(ops_list, st_ref, ybuf, lnw_ref, lnb_ref, ones_ref, masks):
    cl = RWKV_CHUNK
    gather = lambda name: [x for ops in ops_list for x in ops[name]]
    a2, r2, b2, k2, v2, bkp_t = (gather(name) for name in _RWKV_STAGED)
    n_tiles = len(ops_list[0]["a2"])
    tiles = range(len(a2))
    cat0 = lambda *xs: jnp.concatenate(xs, axis=0)
    cat1 = lambda *xs: jnp.concatenate(xs, axis=1)
    gram = [_dot_nt(cat0(a2[t], r2[t]), cat0(b2[t], k2[t])) for t in tiles]
    yield
    a_ab = [jnp.where(masks["strict"], g[0:LANES, 0:LANES], 0.0) for g in gram]
    a_ak = [_mx(jnp.where(masks["strict"], g[0:LANES, LANES:], 0.0)) for g in gram]
    a_rb = [_mx(jnp.where(masks["incl"], g[LANES:, 0:LANES], 0.0)) for g in gram]
    a_rk = [_mx(jnp.where(masks["incl"], g[LANES:, LANES:], 0.0)) for g in gram]
    t_inv = yield from _unit_lower_inverse(a_ab, masks["blk"], masks["eye_f"])
    st = [st_ref[t] for t in tiles]
    st_m = [_mx(s) for s in st]
    y0 = [_dot(cat1(a2[t], a_ak[t]), cat0(st_m[t], v2[t])) for t in tiles]
    yield
    sa = [_mx(_dot(_mx(t_inv[t]), _mx(y0[t]))) for t in tiles]
    yield
    o2 = [_dot(cat1(r2[t], a_rb[t], a_rk[t]), cat0(st_m[t], sa[t], v2[t])) for t in tiles]
    yield
    upd = [_dot(bkp_t[t], cat0(sa[t], v2[t])) for t in tiles]
    for t in tiles:
        b, p = divmod(t, n_tiles)
        ps = slice(p * LANES, (p + 1) * LANES)
        ybuf[b, :, ps] = o2[t][0:cl, :] + o2[t][cl:, :]
        p_end = ops_list[b]["p_end"][:, ps]
        p_col = jnp.sum(jnp.where(masks["eye"], p_end, 0.0), axis=1, keepdims=True)
        st_ref[t] = st[t] * p_col + upd[t]
    yield
    ones_blk = ones_ref[...]
    inv_n = 1.0 / HEAD
    outs = []
    for b, ops in enumerate(ops_list):
        y = ybuf[b]
        mean = _rwkv_head_sum(y, ones_blk) * inv_n
        d = y - mean
        var = _rwkv_head_sum(d * d, ones_blk) * inv_n
        y = d * lax.rsqrt(var + RWKV_LN_EPS) * lnw_ref[...] + lnb_ref[...]
        outs.append((y + ops["bonus"]) * ops["gate"])
    return outs


def _run(gen):
    while True:
        try:
            next(gen)
        except StopIteration as stop:
            return stop.value


def _interleave(scan_gen, prep_gens):
    gens = [scan_gen] + list(prep_gens)
    results, done = [None] * len(gens), [False] * len(gens)
    while not all(done):
        for i, gen in enumerate(gens):
            if not done[i]:
                try:
                    next(gen)
                except StopIteration as stop:
                    results[i], done[i] = stop.value, True
    return results[0], results[1:]


_RWKV_STAGED = ("a2", "r2", "b2", "k2", "v2", "bkp")
RWKV_SEQS_PER_STEP = 2


def _rwkv_body(*refs):
    cl = RWKV_CHUNK
    first_src, odd_src, next_src, mu_ref = refs[0:4]
    prm = refs[4:12] + (refs[14],)
    lnw_ref, lnb_ref, ones_ref = refs[12], refs[13], refs[14]
    o_ref = refs[15]
    buf = refs[16]
    st_ref, ybuf = refs[17], refs[18]
    staged = dict(zip(_RWKV_STAGED, refs[19:25]))
    s_pend, s_bonus, s_gate = refs[25], refs[26], refs[27]
    masks = _rwkv_masks()
    seqs = range(o_ref.shape[0])
    n_tiles = st_ref.shape[0] // len(seqs)

    def prepare(src, b, first):
        return _rwkv_prepare(src.at[b], buf.at[b], mu_ref, prm, masks, first)

    def stash(b, ops):
        for name in _RWKV_STAGED:
            for p in range(n_tiles):
                staged[name][b * n_tiles + p] = ops[name][p]
        s_pend[b] = ops["p_end"]
        s_bonus[b] = ops["bonus"]
        s_gate[b] = ops["gate"]

    def staged_ops(b):
        ops = {name: [staged[name][b * n_tiles + p] for p in range(n_tiles)] for name in _RWKV_STAGED}
        ops.update(p_end=s_pend[b], bonus=s_bonus[b], gate=s_gate[b])
        return ops

    @pl.when(pl.program_id(1) == 0)
    def _():
        st_ref[...] = jnp.zeros(st_ref.shape, F32)
        for b in seqs:
            stash(b, _run(prepare(first_src, b, True)))

    scan = lambda ops_list: _rwkv_scan(ops_list, st_ref, ybuf, lnw_ref, lnb_ref, ones_ref, masks)
    y_even, odd = _interleave(scan([staged_ops(b) for b in seqs]), [prepare(odd_src, b, False) for b in seqs])
    for b in seqs:
        o_ref[b, 0:cl, :] = y_even[b].astype(o_ref.dtype)
    y_odd, nxt = _interleave(scan(odd), [prepare(next_src, b, False) for b in seqs])
    for b in seqs:
        o_ref[b, cl:2 * cl, :] = y_odd[b].astype(o_ref.dtype)
        stash(b, nxt[b])


def _rwkv(u2, bsz, seq, mu, w0, w2, a0, a2, g2, k_k, k_a, r_k, ln_w, ln_b):
    cl = RWKV_CHUNK
    nc = seq // cl
    nb = RWKV_SEQS_PER_STEP
    assert nc % 2 == 0 and bsz % nb == 0
    w = W_MIX
    mw = RWKV_MISC
    uw = u2.shape[1]
    n_tiles = w // LANES
    row1 = lambda vec: vec.reshape(1, -1)
    mu_row = jnp.zeros((uw,), F32).at[0:mu.shape[0]].set(mu)
    w2e = jnp.zeros((2 * LANES, w), F32).at[MISC_PW:MISC_PA].set(w2)
    a2e = jnp.zeros((2 * LANES, w), F32).at[MISC_PA:MISC_PG].set(a2)
    g2e = jnp.zeros((mw - LANES, w), F32).at[MISC_PG - LANES:MISC_DT - LANES].set(g2)
    ones_blk = np.kron(np.eye(LANES // HEAD, dtype=np.float32), np.ones((HEAD, HEAD), np.float32))

    def chunk_spec(chunk_of):
        return pl.BlockSpec((nb, cl, uw), lambda bb, j: (bb, chunk_of(j), 0))

    cvec = lambda width: pl.BlockSpec((1, width), lambda bb, j: (0, 0))
    cmat = lambda rows, width: pl.BlockSpec((rows, width), lambda bb, j: (0, 0))
    stage = lambda cols: pltpu.VMEM((nb * n_tiles, LANES, cols), MXU_DTYPE)
    out = pl.pallas_call(
        _rwkv_body,
        grid=(bsz // nb, nc // 2),
        in_specs=([chunk_spec(lambda j: 0), chunk_spec(lambda j: 2 * j + 1),
                   chunk_spec(lambda j: jnp.minimum(2 * j + 2, nc - 1)), cvec(uw)]
                  + [cvec(w), cmat(2 * LANES, w), cvec(w), cmat(2 * LANES, w), cmat(mw - LANES, w),
                     cvec(w), cvec(w), cvec(w), cvec(w), cvec(w), cmat(LANES, LANES)]),
        out_specs=pl.BlockSpec((nb, 2 * cl, w), lambda bb, j: (bb, j, 0)),
        out_shape=jax.ShapeDtypeStruct((bsz, seq, w), MXU_DTYPE),
        scratch_shapes=[pltpu.VMEM((nb, cl + SUBLANES, uw), F32),
                        pltpu.VMEM((nb * n_tiles, LANES, LANES), F32),
                        pltpu.VMEM((nb, cl, w), F32),
                        stage(LANES), stage(LANES), stage(LANES), stage(LANES), stage(LANES),
                        stage(2 * LANES),
                        pltpu.VMEM((nb, 1, w), F32), pltpu.VMEM((nb, cl, w), F32),
                        pltpu.VMEM((nb, cl, w), F32)],
        compiler_params=_cparams(("arbitrary", "arbitrary")),
        name="rwkv7_mixer",
    )(*([u2.reshape(bsz, seq, uw)] * 3), row1(mu_row), row1(w0), _mx(w2e), row1(a0), _mx(a2e), _mx(g2e), row1(k_k), row1(k_a),
      row1(r_k), row1(ln_w), row1(ln_b), jnp.asarray(ones_blk, MXU_DTYPE))
    return out.reshape(bsz * seq, w)


def _attn_body(q_ref, k_ref, v_ref, o_ref):
    d = q_ref.shape[1]
    hd = d // XATTN_HEADS
    scale = hd ** -0.5
    heads = [slice(h * hd, (h + 1) * hd) for h in range(XATTN_HEADS)]
    scores = [_dot_nt(q_ref[:, hs], k_ref[0, :, hs]) * scale for hs in heads]
    probs = []
    for s in scores:
        e = jnp.exp(s - jnp.max(s, axis=-1, keepdims=True))
        probs.append(_mx(e / jnp.sum(e, axis=-1, keepdims=True)))
    for hs, p in zip(heads, probs):
        o_ref[:, hs] = _dot(p, v_ref[0, :, hs]).astype(o_ref.dtype)


def _attention(q, k, v, bsz, seq, ts=512):
    d = q.shape[1]
    n_mem = k.shape[1]
    ts = min(ts, seq)
    ns = seq // ts
    return pl.pallas_call(
        _attn_body,
        grid=(bsz, ns),
        in_specs=[pl.BlockSpec((ts, d), lambda b, s: (b * ns + s, 0)),
                  pl.BlockSpec((1, n_mem, d), lambda b, s: (b, 0, 0)),
                  pl.BlockSpec((1, n_mem, d), lambda b, s: (b, 0, 0))],
        out_specs=pl.BlockSpec((ts, d), lambda b, s: (b * ns + s, 0)),
        out_shape=jax.ShapeDtypeStruct((bsz * seq, d), MXU_DTYPE),
        compiler_params=_cparams(("arbitrary", "arbitrary")),
        name="mem_attention",
    )(q, k, v)


def _rwkv_side(mat, axis):
    ssd_in = U1_COLS + W_MIX // HEAD
    take = lambda lo, hi: lax.slice_in_dim(mat, lo, hi, axis=axis)
    pad_shape = list(mat.shape)
    pad_shape[axis] = U2_COLS - (mat.shape[axis] - U1_COLS)
    return jnp.concatenate([take(ssd_in, mat.shape[axis]), take(U1_COLS, ssd_in),
                            jnp.zeros(pad_shape, mat.dtype)], axis=axis)


def kernel(x, mem, norm_mix_g, w_in, ssd_conv_w, ssd_conv_b, ssd_dt_bias, ssd_a_log, ssd_d, ssd_norm_g, rwkv_mu, rwkv_w0, rwkv_w2, rwkv_a0, rwkv_a2, rwkv_g2, rwkv_k_k, rwkv_k_a, rwkv_r_k, rwkv_ln_w, rwkv_ln_b, w_out, norm_x_g, norm_mem_g, xattn_wq, xattn_wk, xattn_wv, xattn_wo, norm_ffn_g, ffn_w1, ffn_w2, final_norm_g):
    bsz, seq, d = x.shape
    n_mem = mem.shape[1]
    xr = x.reshape(bsz * seq, d)
    memr = mem.reshape(bsz * n_mem, d)
    n_layers = w_in.shape[0]
    for l in range(n_layers):
        w_in_t = w_in[l].T
        u1 = _matmul_resident(xr, w_in_t, F32, n=U1_COLS, b_is_transposed=True,
                              norm_g=norm_mix_g[l], tm=512, tn=U1_COLS // 2, name="in_proj_ssd")
        u2 = _matmul_resident(xr, _rwkv_side(w_in_t, 0), F32, b_is_transposed=True,
                              norm_g=norm_mix_g[l], tm=512, tn=U2_COLS // 2, name="in_proj_rwkv")
        y_ssd = _ssd(u1, u2, bsz, seq, ssd_conv_w[l], ssd_conv_b[l], ssd_dt_bias[l], ssd_a_log[l],
                     ssd_d[l], ssd_norm_g[l])
        y_rwkv = _rwkv(u2, bsz, seq, rwkv_mu[l], rwkv_w0[l], rwkv_w2[l], rwkv_a0[l], rwkv_a2[l],
                       rwkv_g2[l], rwkv_k_k[l], rwkv_k_a[l], rwkv_r_k[l].reshape(-1),
                       rwkv_ln_w[l], rwkv_ln_b[l])
        wo = _mx(w_out[l])
        xr, h = _matmul_rows([(y_ssd, wo, 0), (y_rwkv, wo, W_MIX)], xr, norm_x_g[l],
                             MXU_DTYPE, name="out_proj")

        m = _rmsnorm(memr, norm_mem_g[l], MXU_DTYPE)
        q = _matmul_resident(h, xattn_wq[l], MXU_DTYPE, tm=2048, name="q_proj")
        kx = _matmul_resident(m, xattn_wk[l], MXU_DTYPE, name="k_proj")
        vx = _matmul_resident(m, xattn_wv[l], MXU_DTYPE, name="v_proj")
        o = _attention(q, kx.reshape(bsz, n_mem, d), vx.reshape(bsz, n_mem, d), bsz, seq)
        xr, h = _matmul_rows([(o, _mx(xattn_wo[l]), 0)], xr, norm_ffn_g[l], MXU_DTYPE, name="o_proj")

        hid = _matmul_resident(h, ffn_w1[l], MXU_DTYPE, act="relu2", tm=2048, name="ffn_up")
        down = [(hid, _mx(ffn_w2[l]), 0)]
        if l + 1 < n_layers:
            xr, _ = _matmul_rows(down, xr, norm_mix_g[l + 1], MXU_DTYPE, name="ffn_down")
        else:
            out = _matmul_rows(down, xr, final_norm_g, x.dtype, emit_x=False, tk=1024, name="ffn_down")
    return out.reshape(bsz, seq, d)
```

```python
import functools

import numpy as np
import jax
import jax.numpy as jnp
from jax import lax
from jax.experimental import pallas as pl
from jax.experimental.pallas import tpu as pltpu

F32 = jnp.float32
MXU_DTYPE = jnp.bfloat16

NORM_EPS = 1e-6
RWKV_LN_EPS = 64e-5

HEAD = 64
SSD_STATE = 128
SSD_GROUPS = 2
SSD_CHUNK = 128
SSD_CONV = 4
RWKV_CHUNK = 64
RWKV_DECAY_RANK = 96
RWKV_AAA_RANK = 96
RWKV_GATE_RANK = 256
XATTN_HEADS = 4

LANES = 128
SUBLANES = 8
VMEM_LIMIT = 56 * 1024 * 1024

W_MIX = 1024
SSD_BC = 2 * SSD_GROUPS * SSD_STATE
U1_COLS = 2 * W_MIX + SSD_BC
U2_MISC = 3 * W_MIX
RWKV_MISC = 512
U2_COLS = U2_MISC + RWKV_MISC
MISC_PW = 0
MISC_PA = MISC_PW + RWKV_DECAY_RANK
MISC_PG = MISC_PA + RWKV_AAA_RANK
MISC_DT = MISC_PG + RWKV_GATE_RANK
SSD_DT_LANE = MISC_DT % LANES


def _mx(a):
    return a.astype(MXU_DTYPE)


def _dot(a, b):
    return jnp.dot(a, b, preferred_element_type=F32)


def _dot_nt(a, b):
    return lax.dot_general(a, b, (((1,), (1,)), ((), ())), preferred_element_type=F32)


def _split(v, parts):
    out = []
    rem = v
    for _ in range(parts):
        p = rem.astype(MXU_DTYPE)
        out.append(p)
        rem = rem - p.astype(F32)
    return out


def _dot_split_rhs(a01, v, parts):
    acc = None
    for p in _split(v, parts):
        t = _dot(a01, p)
        acc = t if acc is None else acc + t
    return acc


def _dot_split_lhs(v, b01, parts):
    acc = None
    for p in _split(v, parts):
        t = _dot(p, b01)
        acc = t if acc is None else acc + t
    return acc


def _sigmoid(x):
    return 1.0 / (1.0 + jnp.exp(-x))


def _softplus(x):
    return jnp.maximum(x, 0.0) + jnp.log(1.0 + jnp.exp(-jnp.abs(x)))


def _cparams(sem):
    return pltpu.CompilerParams(dimension_semantics=sem, vmem_limit_bytes=VMEM_LIMIT)


def _rmsnorm_body(x_ref, g_ref, o_ref):
    x = x_ref[...]
    ms = jnp.mean(x * x, axis=-1, keepdims=True)
    o_ref[...] = (x * lax.rsqrt(ms + NORM_EPS) * g_ref[...]).astype(o_ref.dtype)


def _rmsnorm(x, g, out_dtype, tm=512):
    m, d = x.shape
    tm = min(tm, m)
    return pl.pallas_call(
        _rmsnorm_body,
        grid=(m // tm,),
        in_specs=[pl.BlockSpec((tm, d), lambda i: (i, 0)),
                  pl.BlockSpec((1, d), lambda i: (0, 0))],
        out_specs=pl.BlockSpec((tm, d), lambda i: (i, 0)),
        out_shape=jax.ShapeDtypeStruct((m, d), out_dtype),
        compiler_params=_cparams(("arbitrary",)),
        name="rmsnorm",
    )(x, g.reshape(1, d))


def _mm_resident_body(*refs, act, b_is_transposed, normalize):
    if normalize:
        a_ref, g_ref, b_ref, o_ref, bm_ref = refs
    else:
        a_ref, b_ref, o_ref, bm_ref = refs

    @pl.when(pl.program_id(1) == 0)
    def _():
        b = b_ref[...]
        bm_ref[...] = (b.T if b_is_transposed else b).astype(MXU_DTYPE)

    a = a_ref[...]
    if normalize:
        ms = jnp.mean(a * a, axis=-1, keepdims=True)
        a = (a * lax.rsqrt(ms + NORM_EPS) * g_ref[...]).astype(MXU_DTYPE)
    r = _dot(a, bm_ref[...])
    if act == "relu2":
        r = jnp.square(jnp.maximum(r, 0.0))
    o_ref[...] = r.astype(o_ref.dtype)


def _matmul_resident(a, b, out_dtype, act=None, n=None, b_is_transposed=False, norm_g=None,
                     tm=1024, tn=1024, name="matmul_resident"):
    m, kdim = a.shape
    n = b.shape[0 if b_is_transposed else 1] if n is None else n
    tm, tn = min(tm, m), min(tn, n)
    assert m % tm == 0 and n % tn == 0
    b_spec = (pl.BlockSpec((tn, kdim), lambda j, i: (j, 0)) if b_is_transposed
              else pl.BlockSpec((kdim, tn), lambda j, i: (0, j)))
    in_specs, args = [pl.BlockSpec((tm, kdim), lambda j, i: (i, 0))], [a]
    if norm_g is not None:
        in_specs.append(pl.BlockSpec((1, kdim), lambda j, i: (0, 0)))
        args.append(norm_g.reshape(1, kdim))
    return pl.pallas_call(
        functools.partial(_mm_resident_body, act=act, b_is_transposed=b_is_transposed,
                          normalize=norm_g is not None),
        grid=(n // tn, m // tm),
        in_specs=in_specs + [b_spec],
        out_specs=pl.BlockSpec((tm, tn), lambda j, i: (i, j)),
        out_shape=jax.ShapeDtypeStruct((m, n), out_dtype),
        scratch_shapes=[pltpu.VMEM((kdim, tn), MXU_DTYPE)],
        compiler_params=_cparams(("arbitrary", "arbitrary")),
        name=name,
    )(*args, b)


def _mm_rows_body(*refs, n_pairs, nk, emit_x, side):
    ab = refs[:2 * n_pairs]
    rest = list(refs[2 * n_pairs:])
    res_ref, g_ref = rest.pop(0), rest.pop(0)
    side_in = rest.pop(0) if side else None
    x_ref = rest.pop(0) if emit_x else None
    h_ref = rest.pop(0)
    side_out = rest.pop(0) if side else None
    acc_ref = rest.pop(0) if nk > 1 else None
    if side:
        side_out[...] = side_in[...].astype(side_out.dtype)
    part = None
    for p in range(n_pairs):
        t = _dot(ab[2 * p][...], ab[2 * p + 1][...])
        part = t if part is None else part + t

    def finish(acc):
        x = res_ref[...] + acc
        if emit_x:
            x_ref[...] = x
        ms = jnp.mean(x * x, axis=-1, keepdims=True)
        h_ref[...] = (x * lax.rsqrt(ms + NORM_EPS) * g_ref[...]).astype(h_ref.dtype)

    if nk == 1:
        finish(part)
        return

    k = pl.program_id(1)

    @pl.when(k == 0)
    def _():
        acc_ref[...] = part

    @pl.when(k > 0)
    def _():
        acc_ref[...] += part

    @pl.when(k == nk - 1)
    def _():
        finish(acc_ref[...])


def _matmul_rows(pairs, res, g, h_dtype, emit_x=True, side_cast=None, tm=512, tk=2048,
                 name="matmul_rows"):
    m, n = res.shape
    tm = min(tm, m)
    kdims = [a.shape[1] for a, _, _ in pairs]
    tk = min(tk, min(kdims))
    nk = kdims[0] // tk
    assert all(kd == nk * tk for kd in kdims) and all(row0 % tk == 0 for _, _, row0 in pairs)
    in_specs, args = [], []
    for a, b, row0 in pairs:
        in_specs += [pl.BlockSpec((tm, tk), lambda i, k: (i, k)),
                     pl.BlockSpec((tk, n), lambda i, k, blk0=row0 // tk: (blk0 + k, 0))]
        args += [a, b]
    row_spec = pl.BlockSpec((tm, n), lambda i, k: (i, 0))
    in_specs += [row_spec, pl.BlockSpec((1, n), lambda i, k: (0, 0))]
    args += [res, g.reshape(1, n)]
    out_specs, out_shape = [row_spec], [jax.ShapeDtypeStruct((m, n), h_dtype)]
    if emit_x:
        out_specs, out_shape = [row_spec] + out_specs, [jax.ShapeDtypeStruct((m, n), F32)] + out_shape
    if side_cast is not None:
        assert nk == 1 and side_cast.shape[0] % (m // tm) == 0
        slab = pl.BlockSpec((side_cast.shape[0] // (m // tm), side_cast.shape[1]), lambda i, k: (i, 0))
        in_specs.append(slab)
        args.append(side_cast)
        out_specs.append(slab)
        out_shape.append(jax.ShapeDtypeStruct(side_cast.shape, MXU_DTYPE))
    outs = pl.pallas_call(
        functools.partial(_mm_rows_body, n_pairs=len(pairs), nk=nk, emit_x=emit_x,
                          side=side_cast is not None),
        grid=(m // tm, nk),
        in_specs=in_specs,
        out_specs=out_specs,
        out_shape=out_shape,
        scratch_shapes=[pltpu.VMEM((tm, n), F32)] if nk > 1 else [],
        compiler_params=_cparams(("arbitrary", "arbitrary")),
        name=name,
    )(*args)
    return tuple(outs) if len(outs) > 1 else outs[0]


def _ssd_body(z_ref, xs_ref, bc_ref, m_ref, cwx_ref, cbx_ref, cwbc_ref, cbbc_ref, dtb_ref,
              alog_ref, dvec_ref, ng_ref, esel_ref, o_ref, xbuf, bcbuf, st_ref):
    q = SSD_CHUNK
    n = SSD_STATE
    gw = st_ref.shape[2]
    c = pl.program_id(1)

    @pl.when(c == 0)
    def _():
        xbuf[0:SUBLANES, :] = jnp.zeros((SUBLANES, xbuf.shape[1]), F32)
        bcbuf[0:SUBLANES, :] = jnp.zeros((SUBLANES, bcbuf.shape[1]), F32)
        st_ref[...] = jnp.zeros(st_ref.shape, F32)

    @pl.when(c > 0)
    def _():
        xbuf[0:SUBLANES, :] = xbuf[q:q + SUBLANES, :]
        bcbuf[0:SUBLANES, :] = bcbuf[q:q + SUBLANES, :]

    xbuf[SUBLANES:SUBLANES + q, :] = xs_ref[...]
    bcbuf[SUBLANES:SUBLANES + q, :] = bc_ref[...]

    def conv_silu(buf, w_ref, b_ref):
        acc = None
        for k in range(SSD_CONV):
            off = SUBLANES - (SSD_CONV - 1) + k
            t = buf[off:off + q, :] * w_ref[k:k + 1, :]
            acc = t if acc is None else acc + t
        acc = acc + b_ref[...]
        return acc * _sigmoid(acc)

    xc = conv_silu(xbuf, cwx_ref, cbx_ref)
    bcc = conv_silu(bcbuf, cwbc_ref, cbbc_ref)
    g_n = SSD_GROUPS * n

    lane = lax.broadcasted_iota(jnp.int32, (1, LANES), 1)
    n_heads = SSD_GROUPS * gw // HEAD
    dmask = (lane >= SSD_DT_LANE) & (lane < SSD_DT_LANE + n_heads)
    dt = jnp.where(dmask, _softplus(m_ref[...] + dtb_ref[...]), 0.0)
    a = jnp.where(dmask, -jnp.exp(alog_ref[...]), 0.0)
    da = dt * a
    row = lax.broadcasted_iota(jnp.int32, (q, q), 0)
    col = lax.broadcasted_iota(jnp.int32, (q, q), 1)
    tri = row >= col
    tril = jnp.where(tri, 1.0, 0.0).astype(MXU_DTYPE)
    cs = _dot_split_rhs(tril, da, 3)
    ecs = jnp.exp(cs)
    dte = jnp.exp(cs[q - 1:q, :] - cs)
    esel = esel_ref[...]
    dt_e = _dot_split_lhs(dt, esel, 2)
    ecs_e = _dot_split_lhs(ecs, esel, 2)
    dte_e = _dot_split_lhs(dte, esel, 2)
    cs_t = cs.T

    xdt = xc * dt_e
    xdt_m = _mx(xdt)
    xd_m = _mx(xdt * dte_e)
    lane_lo = lane < HEAD

    y_cols = []
    for g in range(SSD_GROUPS):
        bg = bcc[:, g * n:(g + 1) * n]
        cg = _mx(bcc[:, g_n + g * n:g_n + (g + 1) * n])
        cb = _dot_nt(cg, _mx(bg))
        gs = slice(g * gw, (g + 1) * gw)
        st = st_ref[g]
        y_off = _dot(cg, _mx(st)) * ecs_e[:, gs]
        st_ref[g] = st * ecs_e[q - 1:q, gs] + _dot(_mx(bg.T), xd_m[:, gs])
        for pr in range(gw // LANES):
            h0 = (g * gw + pr * LANES) // HEAD
            ps = slice(g * gw + pr * LANES, g * gw + (pr + 1) * LANES)
            res = []
            for hh in (h0, h0 + 1):
                li = SSD_DT_LANE + hh
                seg = cs[:, li:li + 1] - cs_t[li:li + 1, :]
                lm = jnp.where(tri, jnp.exp(jnp.where(tri, seg, 0.0)), 0.0)
                res.append(_dot(_mx(cb * lm), xdt_m[:, ps]))
            y_diag = jnp.where(lane_lo, res[0], res[1])
            y_cols.append(y_diag + y_off[:, pr * LANES:(pr + 1) * LANES])
    y = jnp.concatenate(y_cols, axis=1) + xc * dvec_ref[...]
    zz = z_ref[...]
    y = y * (zz * _sigmoid(zz))
    outs = []
    for g in range(SSD_GROUPS):
        yg = y[:, g * gw:(g + 1) * gw]
        ms = jnp.mean(yg * yg, axis=-1, keepdims=True)
        outs.append(yg * lax.rsqrt(ms + NORM_EPS))
    y = jnp.concatenate(outs, axis=1) * ng_ref[...]
    o_ref[...] = y.astype(o_ref.dtype)


def _ssd(u1, u2, bsz, seq, conv_w, conv_b, dt_bias, a_log, d_skip, norm_g):
    q = SSD_CHUNK
    nc = seq // q
    w = W_MIX
    gw = w // SSD_GROUPS
    heads = w // HEAD
    bcw = SSD_BC
    cw = conv_w[:, 0, :]
    pad = lambda v: jnp.zeros((1, LANES), F32).at[0, SSD_DT_LANE:SSD_DT_LANE + heads].set(v)
    esel = np.zeros((LANES, w), np.float32)
    for h in range(heads):
        esel[SSD_DT_LANE + h, h * HEAD:(h + 1) * HEAD] = 1.0
    rowblk = lambda cb: (lambda b, c: (b * nc + c, cb))
    const = lambda b, c: (0, 0)
    return pl.pallas_call(
        _ssd_body,
        grid=(bsz, nc),
        in_specs=[pl.BlockSpec((q, w), rowblk(0)),
                  pl.BlockSpec((q, w), rowblk(1)),
                  pl.BlockSpec((q, bcw), rowblk(2 * w // bcw)),
                  pl.BlockSpec((q, LANES), rowblk((U2_MISC + MISC_DT) // LANES)),
                  pl.BlockSpec((SSD_CONV, w), const), pl.BlockSpec((1, w), const),
                  pl.BlockSpec((SSD_CONV, bcw), const), pl.BlockSpec((1, bcw), const),
                  pl.BlockSpec((1, LANES), const), pl.BlockSpec((1, LANES), const),
                  pl.BlockSpec((1, w), const), pl.BlockSpec((1, w), const),
                  pl.BlockSpec((LANES, w), const)],
        out_specs=pl.BlockSpec((q, w), lambda b, c: (b * nc + c, 0)),
        out_shape=jax.ShapeDtypeStruct((bsz * seq, w), MXU_DTYPE),
        scratch_shapes=[pltpu.VMEM((q + SUBLANES, w), F32),
                        pltpu.VMEM((q + SUBLANES, bcw), F32),
                        pltpu.VMEM((SSD_GROUPS, SSD_STATE, gw), F32)],
        compiler_params=_cparams(("arbitrary", "arbitrary")),
        name="ssd_mixer",
    )(u1, u1, u1, u2, cw[:, :w], conv_b[:w].reshape(1, w), cw[:, w:], conv_b[w:].reshape(1, bcw),
      pad(dt_bias), pad(a_log), jnp.repeat(d_skip, HEAD).reshape(1, w), norm_g.reshape(1, w),
      jnp.asarray(esel, MXU_DTYPE))


def _unit_lower_inverse(l_mats, blk_mask, eye_f):
    mm = lambda xs, ys: [_dot(x, y) for x, y in zip(xs, ys)]
    mx = lambda xs: [_mx(x) for x in xs]
    add = lambda xs, ys: [x + y for x, y in zip(xs, ys)]
    d = [jnp.where(blk_mask, l, 0.0) for l in l_mats]
    o = mx([l - x for l, x in zip(l_mats, d)])
    d1 = mx(d)
    d2 = mx(mm(d1, d1))
    t = [eye_f + x for x in d]
    yield
    d4 = mx(mm(d2, d2))
    t = add(t, mm(d2, mx(t)))
    yield
    d8 = mx(mm(d4, d4))
    t = add(t, mm(d4, mx(t)))
    yield
    t_d = add(t, mm(d8, mx(t)))
    t_dm = mx(t_d)
    yield
    m1 = mm(t_dm, o)
    m1m = mx(m1)
    yield
    m2 = mm(m1m, m1m)
    yield
    m3 = mm(m1m, mx(m2))
    w = mx([x + y + z for x, y, z in zip(m1, m2, m3)])
    yield
    return add(t_d, mm(w, t_dm))


def _rwkv_masks():
    lane = lax.broadcasted_iota(jnp.int32, (1, LANES), 1)
    row = lax.broadcasted_iota(jnp.int32, (LANES, LANES), 0)
    col = lax.broadcasted_iota(jnp.int32, (LANES, LANES), 1)
    same = (row >= HEAD) == (col >= HEAD)
    rs = row & (HEAD - 1)
    cs = col & (HEAD - 1)
    eye = row == col
    return dict(lane_lo=lane < HEAD, strict=same & (rs > cs), incl=same & (rs >= cs),
                blk=(row >> 4) == (col >> 4), eye=eye, eye_f=jnp.where(eye, 1.0, 0.0))


def _rwkv_head_sum(x, ones_blk):
    cols = [_dot_split_lhs(x[:, t * LANES:(t + 1) * LANES], ones_blk, 2)
            for t in range(x.shape[1] // LANES)]
    return jnp.concatenate(cols, axis=1)


def _rwkv_prepare(src, buf, mu_ref, prm, masks, first):
    cl = RWKV_CHUNK
    w0_ref, w2_ref, a0_ref, a2_ref, g2_ref, kk_ref, ka_ref, rk_ref, ones_ref = prm
    if first:
        buf[0:SUBLANES, :] = jnp.zeros((SUBLANES, buf.shape[1]), F32)
    else:
        buf[0:SUBLANES, :] = buf[cl:cl + SUBLANES, :]
    cur = src[...]
    buf[SUBLANES:SUBLANES + cl, :] = cur
    prev = buf[SUBLANES - 1:SUBLANES - 1 + cl, :]
    row = cur + (prev - cur) * mu_ref[...]
    w = W_MIX
    r, k, v = (row[:, i * w:(i + 1) * w] for i in range(3))
    misc = row[:, U2_MISC:]
    pw_pa = misc[:, 0:2 * LANES]
    ones_blk = ones_ref[...]
    yield
    w_log = -_softplus(-(w0_ref[...] + _dot(_mx(jnp.tanh(pw_pa)), w2_ref[...]))) - 0.5
    lw = -jnp.exp(w_log)
    iclr = _sigmoid(a0_ref[...] + _dot(_mx(pw_pa), a2_ref[...]))
    gate = _dot(_mx(_sigmoid(misc[:, LANES:])), g2_ref[...])
    yield

    row_c = lax.broadcasted_iota(jnp.int32, (cl, cl), 0)
    col_c = lax.broadcasted_iota(jnp.int32, (cl, cl), 1)
    tril_c = jnp.where(row_c >= col_c, 1.0, 0.0).astype(MXU_DTYPE)
    lane_lo = masks["lane_lo"]
    stack2 = lambda x: jnp.concatenate([jnp.where(lane_lo, x, 0.0), jnp.where(lane_lo, 0.0, x)], axis=0)
    out = dict(a2=[], r2=[], b2=[], k2=[], v2=[], bkp=[])
    p_ends, bonuses = [], []
    for p in range(r.shape[1] // LANES):
        ps = slice(p * LANES, (p + 1) * LANES)
        r_p, k_p, v_p, lw_p, iclr_p = r[:, ps], k[:, ps], v[:, ps], lw[:, ps], iclr[:, ps]
        kk = k_p * kk_ref[:, ps]
        kk = kk / jnp.maximum(jnp.sqrt(_dot_split_lhs(kk * kk, ones_blk, 2)), 1e-12)
        k_p = k_p * (1.0 + (iclr_p - 1.0) * ka_ref[:, ps])
        bonuses.append(_dot_split_lhs(r_p * k_p * rk_ref[:, ps], ones_blk, 2) * v_p)
        cum = _dot_split_rhs(tril_c, lw_p, 3)
        p_in = jnp.exp(cum)
        p_inv = jnp.exp(-cum)
        a_t = -(kk * jnp.exp(cum - lw_p))
        b_t = kk * iclr_p * p_inv
        k_t = k_p * p_inv
        p_end = p_in[cl - 1:cl, :]
        out["a2"].append(_mx(stack2(a_t)))
        out["r2"].append(_mx(stack2(r_p * p_in)))
        out["b2"].append(_mx(stack2(b_t)))
        out["k2"].append(_mx(stack2(k_t)))
        out["v2"].append(_mx(stack2(v_p)))
        out["bkp"].append(_mx(jnp.concatenate([stack2(b_t * p_end), stack2(k_t * p_end)], axis=0).T))
        p_ends.append(p_end)
        yield
    out.update(p_end=jnp.concatenate(p_ends, axis=1), bonus=jnp.concatenate(bonuses, axis=1), gate=gate)
    return out


def _rwkv_scan(ops_list, st_ref, ybuf, lnw_ref, lnb_ref, ones_ref, masks):
    cl = RWKV_CHUNK
    gather = lambda name: [x for ops in ops_list for x in ops[name]]
    a2, r2, b2, k2, v2, bkp_t = (gather(name) for name in _RWKV_STAGED)
    n_tiles = len(ops_list[0]["a2"])
    tiles = range(len(a2))
    cat0 = lambda *xs: jnp.concatenate(xs, axis=0)
    cat1 = lambda *xs: jnp.concatenate(xs, axis=1)
    gram = [_dot_nt(cat0(a2[t], r2[t]), cat0(b2[t], k2[t])) for t in tiles]
    yield
    a_ab = [jnp.where(masks["strict"], g[0:LANES, 0:LANES], 0.0) for g in gram]
    a_ak = [_mx(jnp.where(masks["strict"], g[0:LANES, LANES:], 0.0)) for g in gram]
    a_rb = [_mx(jnp.where(masks["incl"], g[LANES:, 0:LANES], 0.0)) for g in gram]
    a_rk = [_mx(jnp.where(masks["incl"], g[LANES:, LANES:], 0.0)) for g in gram]
    t_inv = yield from _unit_lower_inverse(a_ab, masks["blk"], masks["eye_f"])
    st = [st_ref[t] for t in tiles]
    st_m = [_mx(s) for s in st]
    y0 = [_dot(cat1(a2[t], a_ak[t]), cat0(st_m[t], v2[t])) for t in tiles]
    yield
    sa = [_mx(_dot(_mx(t_inv[t]), _mx(y0[t]))) for t in tiles]
    yield
    o2 = [_dot(cat1(r2[t], a_rb[t], a_rk[t]), cat0(st_m[t], sa[t], v2[t])) for t in tiles]
    yield
    upd = [_dot(bkp_t[t], cat0(sa[t], v2[t])) for t in tiles]
    for t in tiles:
        b, p = divmod(t, n_tiles)
        ps = slice(p * LANES, (p + 1) * LANES)
        ybuf[b, :, ps] = o2[t][0:cl, :] + o2[t][cl:, :]
        p_end = ops_list[b]["p_end"][:, ps]
        p_col = jnp.sum(jnp.where(masks["eye"], p_end, 0.0), axis=1, keepdims=True)
        st_ref[t] = st[t] * p_col + upd[t]
    yield
    ones_blk = ones_ref[...]
    inv_n = 1.0 / HEAD
    outs = []
    for b, ops in enumerate(ops_list):
        y = ybuf[b]
        mean = _rwkv_head_sum(y, ones_blk) * inv_n
        d = y - mean
        var = _rwkv_head_sum(d * d, ones_blk) * inv_n
        y = d * lax.rsqrt(var + RWKV_LN_EPS) * lnw_ref[...] + lnb_ref[...]
        outs.append((y + ops["bonus"]) * ops["gate"])
    return outs


def _run(gen):
    while True:
        try:
            next(gen)
        except StopIteration as stop:
            return stop.value


def _interleave(scan_gen, prep_gens):
    gens = [scan_gen] + list(prep_gens)
    results, done = [None] * len(gens), [False] * len(gens)
    while not all(done):
        for i, gen in enumerate(gens):
            if not done[i]:
                try:
                    next(gen)
                except StopIteration as stop:
                    results[i], done[i] = stop.value, True
    return results[0], results[1:]


_RWKV_STAGED = ("a2", "r2", "b2", "k2", "v2", "bkp")
RWKV_SEQS_PER_STEP = 2


def _rwkv_body(*refs):
    cl = RWKV_CHUNK
    first_src, odd_src, next_src, mu_ref = refs[0:4]
    prm = refs[4:12] + (refs[14],)
    lnw_ref, lnb_ref, ones_ref = refs[12], refs[13], refs[14]
    o_ref = refs[15]
    buf = refs[16]
    st_ref, ybuf = refs[17], refs[18]
    staged = dict(zip(_RWKV_STAGED, refs[19:25]))
    s_pend, s_bonus, s_gate = refs[25], refs[26], refs[27]
    masks = _rwkv_masks()
    seqs = range(o_ref.shape[0])
    n_tiles = st_ref.shape[0] // len(seqs)

    def prepare(src, b, first):
        return _rwkv_prepare(src.at[b], buf.at[b], mu_ref, prm, masks, first)

    def stash(b, ops):
        for name in _RWKV_STAGED:
            for p in range(n_tiles):
                staged[name][b * n_tiles + p] = ops[name][p]
        s_pend[b] = ops["p_end"]
        s_bonus[b] = ops["bonus"]
        s_gate[b] = ops["gate"]

    def staged_ops(b):
        ops = {name: [staged[name][b * n_tiles + p] for p in range(n_tiles)] for name in _RWKV_STAGED}
        ops.update(p_end=s_pend[b], bonus=s_bonus[b], gate=s_gate[b])
        return ops

    @pl.when(pl.program_id(1) == 0)
    def _():
        st_ref[...] = jnp.zeros(st_ref.shape, F32)
        for b in seqs:
            stash(b, _run(prepare(first_src, b, True)))

    scan = lambda ops_list: _rwkv_scan(ops_list, st_ref, ybuf, lnw_ref, lnb_ref, ones_ref, masks)
    y_even, odd = _interleave(scan([staged_ops(b) for b in seqs]), [prepare(odd_src, b, False) for b in seqs])
    for b in seqs:
        o_ref[b, 0:cl, :] = y_even[b].astype(o_ref.dtype)
    y_odd, nxt = _interleave(scan(odd), [prepare(next_src, b, False) for b in seqs])
    for b in seqs:
        o_ref[b, cl:2 * cl, :] = y_odd[b].astype(o_ref.dtype)
        stash(b, nxt[b])


def _rwkv(u2, bsz, seq, mu, w0, w2, a0, a2, g2, k_k, k_a, r_k, ln_w, ln_b):
    cl = RWKV_CHUNK
    nc = seq // cl
    nb = RWKV_SEQS_PER_STEP
    assert nc % 2 == 0 and bsz % nb == 0
    w = W_MIX
    mw = RWKV_MISC
    uw = u2.shape[1]
    n_tiles = w // LANES
    row1 = lambda vec: vec.reshape(1, -1)
    mu_row = jnp.zeros((uw,), F32).at[0:mu.shape[0]].set(mu)
    w2e = jnp.zeros((2 * LANES, w), F32).at[MISC_PW:MISC_PA].set(w2)
    a2e = jnp.zeros((2 * LANES, w), F32).at[MISC_PA:MISC_PG].set(a2)
    g2e = jnp.zeros((mw - LANES, w), F32).at[MISC_PG - LANES:MISC_DT - LANES].set(g2)
    ones_blk = np.kron(np.eye(LANES // HEAD, dtype=np.float32), np.ones((HEAD, HEAD), np.float32))

    def chunk_spec(chunk_of):
        return pl.BlockSpec((nb, cl, uw), lambda bb, j: (bb, chunk_of(j), 0))

    cvec = lambda width: pl.BlockSpec((1, width), lambda bb, j: (0, 0))
    cmat = lambda rows, width: pl.BlockSpec((rows, width), lambda bb, j: (0, 0))
    stage = lambda cols: pltpu.VMEM((nb * n_tiles, LANES, cols), MXU_DTYPE)
    out = pl.pallas_call(
        _rwkv_body,
        grid=(bsz // nb, nc // 2),
        in_specs=([chunk_spec(lambda j: 0), chunk_spec(lambda j: 2 * j + 1),
                   chunk_spec(lambda j: jnp.minimum(2 * j + 2, nc - 1)), cvec(uw)]
                  + [cvec(w), cmat(2 * LANES, w), cvec(w), cmat(2 * LANES, w), cmat(mw - LANES, w),
                     cvec(w), cvec(w), cvec(w), cvec(w), cvec(w), cmat(LANES, LANES)]),
        out_specs=pl.BlockSpec((nb, 2 * cl, w), lambda bb, j: (bb, j, 0)),
        out_shape=jax.ShapeDtypeStruct((bsz, seq, w), MXU_DTYPE),
        scratch_shapes=[pltpu.VMEM((nb, cl + SUBLANES, uw), F32),
                        pltpu.VMEM((nb * n_tiles, LANES, LANES), F32),
                        pltpu.VMEM((nb, cl, w), F32),
                        stage(LANES), stage(LANES), stage(LANES), stage(LANES), stage(LANES),
                        stage(2 * LANES),
                        pltpu.VMEM((nb, 1, w), F32), pltpu.VMEM((nb, cl, w), F32),
                        pltpu.VMEM((nb, cl, w), F32)],
        compiler_params=_cparams(("arbitrary", "arbitrary")),
        name="rwkv7_mixer",
    )(*([u2.reshape(bsz, seq, uw)] * 3), row1(mu_row), row1(w0), _mx(w2e), row1(a0), _mx(a2e), _mx(g2e), row1(k_k), row1(k_a),
      row1(r_k), row1(ln_w), row1(ln_b), jnp.asarray(ones_blk, MXU_DTYPE))
    return out.reshape(bsz * seq, w)


def _attn_body(q_ref, k_ref, v_ref, o_ref):
    d = q_ref.shape[1]
    hd = d // XATTN_HEADS
    scale = hd ** -0.5
    heads = [slice(h * hd, (h + 1) * hd) for h in range(XATTN_HEADS)]
    scores = [_dot_nt(q_ref[:, hs], k_ref[0, :, hs]) * scale for hs in heads]
    probs = []
    for s in scores:
        e = jnp.exp(s - jnp.max(s, axis=-1, keepdims=True))
        probs.append(_mx(e / jnp.sum(e, axis=-1, keepdims=True)))
    for hs, p in zip(heads, probs):
        o_ref[:, hs] = _dot(p, v_ref[0, :, hs]).astype(o_ref.dtype)


def _attention(q, k, v, bsz, seq, ts=512):
    d = q.shape[1]
    n_mem = k.shape[1]
    ts = min(ts, seq)
    ns = seq // ts
    return pl.pallas_call(
        _attn_body,
        grid=(bsz, ns),
        in_specs=[pl.BlockSpec((ts, d), lambda b, s: (b * ns + s, 0)),
                  pl.BlockSpec((1, n_mem, d), lambda b, s: (b, 0, 0)),
                  pl.BlockSpec((1, n_mem, d), lambda b, s: (b, 0, 0))],
        out_specs=pl.BlockSpec((ts, d), lambda b, s: (b * ns + s, 0)),
        out_shape=jax.ShapeDtypeStruct((bsz * seq, d), MXU_DTYPE),
        compiler_params=_cparams(("arbitrary", "arbitrary")),
        name="mem_attention",
    )(q, k, v)


def _rwkv_side(mat, axis):
    ssd_in = U1_COLS + W_MIX // HEAD
    take = lambda lo, hi: lax.slice_in_dim(mat, lo, hi, axis=axis)
    pad_shape = list(mat.shape)
    pad_shape[axis] = U2_COLS - (mat.shape[axis] - U1_COLS)
    return jnp.concatenate([take(ssd_in, mat.shape[axis]), take(U1_COLS, ssd_in),
                            jnp.zeros(pad_shape, mat.dtype)], axis=axis)


def kernel(x, mem, norm_mix_g, w_in, ssd_conv_w, ssd_conv_b, ssd_dt_bias, ssd_a_log, ssd_d, ssd_norm_g, rwkv_mu, rwkv_w0, rwkv_w2, rwkv_a0, rwkv_a2, rwkv_g2, rwkv_k_k, rwkv_k_a, rwkv_r_k, rwkv_ln_w, rwkv_ln_b, w_out, norm_x_g, norm_mem_g, xattn_wq, xattn_wk, xattn_wv, xattn_wo, norm_ffn_g, ffn_w1, ffn_w2, final_norm_g):
    bsz, seq, d = x.shape
    n_mem = mem.shape[1]
    xr = x.reshape(bsz * seq, d)
    memr = mem.reshape(bsz * n_mem, d)
    n_layers = w_in.shape[0]
    for l in range(n_layers):
        w_in_t = w_in[l].T
        u1 = _matmul_resident(xr, w_in_t, F32, n=U1_COLS, b_is_transposed=True,
                              norm_g=norm_mix_g[l], tm=512, tn=U1_COLS // 2, name="in_proj_ssd")
        u2 = _matmul_resident(xr, _rwkv_side(w_in_t, 0), F32, b_is_transposed=True,
                              norm_g=norm_mix_g[l], tm=512, tn=U2_COLS // 2, name="in_proj_rwkv")
        y_ssd = _ssd(u1, u2, bsz, seq, ssd_conv_w[l], ssd_conv_b[l], ssd_dt_bias[l], ssd_a_log[l],
                     ssd_d[l], ssd_norm_g[l])
        y_rwkv = _rwkv(u2, bsz, seq, rwkv_mu[l], rwkv_w0[l], rwkv_w2[l], rwkv_a0[l], rwkv_a2[l],
                       rwkv_g2[l], rwkv_k_k[l], rwkv_k_a[l], rwkv_r_k[l].reshape(-1),
                       rwkv_ln_w[l], rwkv_ln_b[l])
        wo = _mx(w_out[l])
        xr, h, wo_attn = _matmul_rows([(y_ssd, wo, 0), (y_rwkv, wo, W_MIX)], xr, norm_x_g[l],
                                      MXU_DTYPE, side_cast=xattn_wo[l], name="out_proj")

        m = _rmsnorm(memr, norm_mem_g[l], MXU_DTYPE)
        q = _matmul_resident(h, xattn_wq[l], MXU_DTYPE, tm=2048, name="q_proj")
        kx = _matmul_resident(m, xattn_wk[l], MXU_DTYPE, name="k_proj")
        vx = _matmul_resident(m, xattn_wv[l], MXU_DTYPE, name="v_proj")
        o = _attention(q, kx.reshape(bsz, n_mem, d), vx.reshape(bsz, n_mem, d), bsz, seq)
        xr, h, w2 = _matmul_rows([(o, wo_attn, 0)], xr, norm_ffn_g[l], MXU_DTYPE,
                                 side_cast=ffn_w2[l], name="o_proj")

        hid = _matmul_resident(h, ffn_w1[l], MXU_DTYPE, act="relu2", tm=2048, name="ffn_up")
        down = [(hid, w2, 0)]
        if l + 1 < n_layers:
            xr, _ = _matmul_rows(down, xr, norm_mix_g[l + 1], MXU_DTYPE, name="ffn_down")
        else:
            out = _matmul_rows(down, xr, final_norm_g, x.dtype, emit_x=False, name="ffn_down")
    return out.reshape(bsz, seq, d)
```

```python
import functools

import numpy as np
import jax
import jax.numpy as jnp
from jax import lax
from jax.experimental import pallas as pl
from jax.experimental.pallas import tpu as pltpu

F32 = jnp.float32
MXU_DTYPE = jnp.bfloat16

NORM_EPS = 1e-6
RWKV_LN_EPS = 64e-5

HEAD = 64
SSD_STATE = 128
SSD_GROUPS = 2
SSD_CHUNK = 128
SSD_CONV = 4
RWKV_CHUNK = 64
RWKV_DECAY_RANK = 96
RWKV_AAA_RANK = 96
RWKV_GATE_RANK = 256
XATTN_HEADS = 4

LANES = 128
SUBLANES = 8
VMEM_LIMIT = 56 * 1024 * 1024

W_MIX = 1024
SSD_BC = 2 * SSD_GROUPS * SSD_STATE
U1_COLS = 2 * W_MIX + SSD_BC
U2_MISC = 3 * W_MIX
RWKV_MISC = 512
U2_COLS = U2_MISC + RWKV_MISC
MISC_PW = 0
MISC_PA = MISC_PW + RWKV_DECAY_RANK
MISC_PG = MISC_PA + RWKV_AAA_RANK
MISC_DT = MISC_PG + RWKV_GATE_RANK
SSD_DT_LANE = MISC_DT % LANES


def _mx(a):
    return a.astype(MXU_DTYPE)


def _dot(a, b):
    return jnp.dot(a, b, preferred_element_type=F32)


def _dot_nt(a, b):
    return lax.dot_general(a, b, (((1,), (1,)), ((), ())), preferred_element_type=F32)


def _split(v, parts):
    out = []
    rem = v
    for _ in range(parts):
        p = rem.astype(MXU_DTYPE)
        out.append(p)
        rem = rem - p.astype(F32)
    return out


def _dot_split_rhs(a01, v, parts):
    acc = None
    for p in _split(v, parts):
        t = _dot(a01, p)
        acc = t if acc is None else acc + t
    return acc


def _dot_split_lhs(v, b01, parts):
    acc = None
    for p in _split(v, parts):
        t = _dot(p, b01)
        acc = t if acc is None else acc + t
    return acc


def _sigmoid(x):
    return 1.0 / (1.0 + jnp.exp(-x))


def _softplus(x):
    return jnp.maximum(x, 0.0) + jnp.log(1.0 + jnp.exp(-jnp.abs(x)))


def _cparams(sem):
    return pltpu.CompilerParams(dimension_semantics=sem, vmem_limit_bytes=VMEM_LIMIT)


def _rmsnorm_body(x_ref, g_ref, o_ref):
    x = x_ref[...]
    ms = jnp.mean(x * x, axis=-1, keepdims=True)
    o_ref[...] = (x * lax.rsqrt(ms + NORM_EPS) * g_ref[...]).astype(o_ref.dtype)


def _rmsnorm(x, g, out_dtype, tm=512):
    m, d = x.shape
    tm = min(tm, m)
    return pl.pallas_call(
        _rmsnorm_body,
        grid=(m // tm,),
        in_specs=[pl.BlockSpec((tm, d), lambda i: (i, 0)),
                  pl.BlockSpec((1, d), lambda i: (0, 0))],
        out_specs=pl.BlockSpec((tm, d), lambda i: (i, 0)),
        out_shape=jax.ShapeDtypeStruct((m, d), out_dtype),
        compiler_params=_cparams(("arbitrary",)),
        name="rmsnorm",
    )(x, g.reshape(1, d))


def _mm_resident_body(*refs, act, b_is_transposed, normalize):
    if normalize:
        a_ref, g_ref, b_ref, o_ref, bm_ref = refs
    else:
        a_ref, b_ref, o_ref, bm_ref = refs

    @pl.when(pl.program_id(1) == 0)
    def _():
        b = b_ref[...]
        bm_ref[...] = (b.T if b_is_transposed else b).astype(MXU_DTYPE)

    a = a_ref[...]
    if normalize:
        ms = jnp.mean(a * a, axis=-1, keepdims=True)
        a = (a * lax.rsqrt(ms + NORM_EPS) * g_ref[...]).astype(MXU_DTYPE)
    r = _dot(a, bm_ref[...])
    if act == "relu2":
        r = jnp.square(jnp.maximum(r, 0.0))
    o_ref[...] = r.astype(o_ref.dtype)


def _matmul_resident(a, b, out_dtype, act=None, n=None, b_is_transposed=False, norm_g=None,
                     tm=1024, tn=1024, name="matmul_resident"):
    m, kdim = a.shape
    n = b.shape[0 if b_is_transposed else 1] if n is None else n
    tm, tn = min(tm, m), min(tn, n)
    assert m % tm == 0 and n % tn == 0
    b_spec = (pl.BlockSpec((tn, kdim), lambda j, i: (j, 0)) if b_is_transposed
              else pl.BlockSpec((kdim, tn), lambda j, i: (0, j)))
    in_specs, args = [pl.BlockSpec((tm, kdim), lambda j, i: (i, 0))], [a]
    if norm_g is not None:
        in_specs.append(pl.BlockSpec((1, kdim), lambda j, i: (0, 0)))
        args.append(norm_g.reshape(1, kdim))
    return pl.pallas_call(
        functools.partial(_mm_resident_body, act=act, b_is_transposed=b_is_transposed,
                          normalize=norm_g is not None),
        grid=(n // tn, m // tm),
        in_specs=in_specs + [b_spec],
        out_specs=pl.BlockSpec((tm, tn), lambda j, i: (i, j)),
        out_shape=jax.ShapeDtypeStruct((m, n), out_dtype),
        scratch_shapes=[pltpu.VMEM((kdim, tn), MXU_DTYPE)],
        compiler_params=_cparams(("arbitrary", "arbitrary")),
        name=name,
    )(*args, b)


def _mm_rows_body(*refs, n_pairs, nk, emit_x, side):
    ab = refs[:2 * n_pairs]
    rest = list(refs[2 * n_pairs:])
    res_ref, g_ref = rest.pop(0), rest.pop(0)
    side_in = rest.pop(0) if side else None
    x_ref = rest.pop(0) if emit_x else None
    h_ref = rest.pop(0)
    side_out = rest.pop(0) if side else None
    acc_ref = rest.pop(0) if nk > 1 else None
    if side:
        side_out[...] = side_in[...].astype(side_out.dtype)
    part = None
    for p in range(n_pairs):
        t = _dot(ab[2 * p][...], ab[2 * p + 1][...])
        part = t if part is None else part + t

    def finish(acc):
        x = res_ref[...] + acc
        if emit_x:
            x_ref[...] = x
        ms = jnp.mean(x * x, axis=-1, keepdims=True)
        h_ref[...] = (x * lax.rsqrt(ms + NORM_EPS) * g_ref[...]).astype(h_ref.dtype)

    if nk == 1:
        finish(part)
        return

    k = pl.program_id(1)

    @pl.when(k == 0)
    def _():
        acc_ref[...] = part

    @pl.when(k > 0)
    def _():
        acc_ref[...] += part

    @pl.when(k == nk - 1)
    def _():
        finish(acc_ref[...])


def _matmul_rows(pairs, res, g, h_dtype, emit_x=True, side_cast=None, tm=512, tk=2048,
                 name="matmul_rows"):
    m, n = res.shape
    tm = min(tm, m)
    kdims = [a.shape[1] for a, _, _ in pairs]
    tk = min(tk, min(kdims))
    nk = kdims[0] // tk
    assert all(kd == nk * tk for kd in kdims) and all(row0 % tk == 0 for _, _, row0 in pairs)
    in_specs, args = [], []
    for a, b, row0 in pairs:
        in_specs += [pl.BlockSpec((tm, tk), lambda i, k: (i, k)),
                     pl.BlockSpec((tk, n), lambda i, k, blk0=row0 // tk: (blk0 + k, 0))]
        args += [a, b]
    row_spec = pl.BlockSpec((tm, n), lambda i, k: (i, 0))
    in_specs += [row_spec, pl.BlockSpec((1, n), lambda i, k: (0, 0))]
    args += [res, g.reshape(1, n)]
    out_specs, out_shape = [row_spec], [jax.ShapeDtypeStruct((m, n), h_dtype)]
    if emit_x:
        out_specs, out_shape = [row_spec] + out_specs, [jax.ShapeDtypeStruct((m, n), F32)] + out_shape
    if side_cast is not None:
        assert nk == 1 and side_cast.shape[0] % (m // tm) == 0
        slab = pl.BlockSpec((side_cast.shape[0] // (m // tm), side_cast.shape[1]), lambda i, k: (i, 0))
        in_specs.append(slab)
        args.append(side_cast)
        out_specs.append(slab)
        out_shape.append(jax.ShapeDtypeStruct(side_cast.shape, MXU_DTYPE))
    outs = pl.pallas_call(
        functools.partial(_mm_rows_body, n_pairs=len(pairs), nk=nk, emit_x=emit_x,
                          side=side_cast is not None),
        grid=(m // tm, nk),
        in_specs=in_specs,
        out_specs=out_specs,
        out_shape=out_shape,
        scratch_shapes=[pltpu.VMEM((tm, n), F32)] if nk > 1 else [],
        compiler_params=_cparams(("arbitrary", "arbitrary")),
        name=name,
    )(*args)
    return tuple(outs) if len(outs) > 1 else outs[0]


def _ssd_body(z_ref, xs_ref, bc_ref, m_ref, cwx_ref, cbx_ref, cwbc_ref, cbbc_ref, dtb_ref,
              alog_ref, dvec_ref, ng_ref, esel_ref, o_ref, xbuf, bcbuf, st_ref):
    q = SSD_CHUNK
    n = SSD_STATE
    gw = st_ref.shape[2]
    c = pl.program_id(1)

    @pl.when(c == 0)
    def _():
        xbuf[0:SUBLANES, :] = jnp.zeros((SUBLANES, xbuf.shape[1]), F32)
        bcbuf[0:SUBLANES, :] = jnp.zeros((SUBLANES, bcbuf.shape[1]), F32)
        st_ref[...] = jnp.zeros(st_ref.shape, F32)

    @pl.when(c > 0)
    def _():
        xbuf[0:SUBLANES, :] = xbuf[q:q + SUBLANES, :]
        bcbuf[0:SUBLANES, :] = bcbuf[q:q + SUBLANES, :]

    xbuf[SUBLANES:SUBLANES + q, :] = xs_ref[...]
    bcbuf[SUBLANES:SUBLANES + q, :] = bc_ref[...]

    def conv_silu(buf, w_ref, b_ref):
        full = buf[0:SUBLANES + q, :]
        acc = None
        for k in range(SSD_CONV):
            back = SSD_CONV - 1 - k
            rows = full if back == 0 else pltpu.roll(full, back, 0)
            t = rows[SUBLANES:, :] * w_ref[k:k + 1, :]
            acc = t if acc is None else acc + t
        acc = acc + b_ref[...]
        return acc * _sigmoid(acc)

    xc = conv_silu(xbuf, cwx_ref, cbx_ref)
    bcc = conv_silu(bcbuf, cwbc_ref, cbbc_ref)
    g_n = SSD_GROUPS * n

    lane = lax.broadcasted_iota(jnp.int32, (1, LANES), 1)
    n_heads = SSD_GROUPS * gw // HEAD
    dmask = (lane >= SSD_DT_LANE) & (lane < SSD_DT_LANE + n_heads)
    dt = jnp.where(dmask, _softplus(m_ref[...] + dtb_ref[...]), 0.0)
    a = jnp.where(dmask, -jnp.exp(alog_ref[...]), 0.0)
    da = dt * a
    row = lax.broadcasted_iota(jnp.int32, (q, q), 0)
    col = lax.broadcasted_iota(jnp.int32, (q, q), 1)
    tri = row >= col
    tril = jnp.where(tri, 1.0, 0.0).astype(MXU_DTYPE)
    cs = _dot_split_rhs(tril, da, 3)
    ecs = jnp.exp(cs)
    dte = jnp.exp(cs[q - 1:q, :] - cs)
    esel = esel_ref[...]
    dt_e = _dot_split_lhs(dt, esel, 2)
    ecs_e = _dot_split_lhs(ecs, esel, 2)
    dte_e = _dot_split_lhs(dte, esel, 2)
    cs_t = cs.T

    xdt = xc * dt_e
    xdt_m = _mx(xdt)
    xd_m = _mx(xdt * dte_e)
    lane_lo = lane < HEAD

    y_cols = []
    for g in range(SSD_GROUPS):
        bg = bcc[:, g * n:(g + 1) * n]
        cg = _mx(bcc[:, g_n + g * n:g_n + (g + 1) * n])
        cb = _dot_nt(cg, _mx(bg))
        gs = slice(g * gw, (g + 1) * gw)
        st = st_ref[g]
        y_off = _dot(cg, _mx(st)) * ecs_e[:, gs]
        st_ref[g] = st * ecs_e[q - 1:q, gs] + _dot(_mx(bg.T), xd_m[:, gs])
        for pr in range(gw // LANES):
            h0 = (g * gw + pr * LANES) // HEAD
            ps = slice(g * gw + pr * LANES, g * gw + (pr + 1) * LANES)
            res = []
            for hh in (h0, h0 + 1):
                li = SSD_DT_LANE + hh
                seg = cs[:, li:li + 1] - cs_t[li:li + 1, :]
                lm = jnp.where(tri, jnp.exp(jnp.where(tri, seg, 0.0)), 0.0)
                res.append(_dot(_mx(cb * lm), xdt_m[:, ps]))
            y_diag = jnp.where(lane_lo, res[0], res[1])
            y_cols.append(y_diag + y_off[:, pr * LANES:(pr + 1) * LANES])
    y = jnp.concatenate(y_cols, axis=1) + xc * dvec_ref[...]
    zz = z_ref[...]
    y = y * (zz * _sigmoid(zz))
    outs = []
    for g in range(SSD_GROUPS):
        yg = y[:, g * gw:(g + 1) * gw]
        ms = jnp.mean(yg * yg, axis=-1, keepdims=True)
        outs.append(yg * lax.rsqrt(ms + NORM_EPS))
    y = jnp.concatenate(outs, axis=1) * ng_ref[...]
    o_ref[...] = y.astype(o_ref.dtype)


def _ssd(u1, u2, bsz, seq, conv_w, conv_b, dt_bias, a_log, d_skip, norm_g):
    q = SSD_CHUNK
    nc = seq // q
    w = W_MIX
    gw = w // SSD_GROUPS
    heads = w // HEAD
    bcw = SSD_BC
    cw = conv_w[:, 0, :]
    pad = lambda v: jnp.zeros((1, LANES), F32).at[0, SSD_DT_LANE:SSD_DT_LANE + heads].set(v)
    esel = np.zeros((LANES, w), np.float32)
    for h in range(heads):
        esel[SSD_DT_LANE + h, h * HEAD:(h + 1) * HEAD] = 1.0
    rowblk = lambda cb: (lambda b, c: (b * nc + c, cb))
    const = lambda b, c: (0, 0)
    return pl.pallas_call(
        _ssd_body,
        grid=(bsz, nc),
        in_specs=[pl.BlockSpec((q, w), rowblk(0)),
                  pl.BlockSpec((q, w), rowblk(1)),
                  pl.BlockSpec((q, bcw), rowblk(2 * w // bcw)),
                  pl.BlockSpec((q, LANES), rowblk((U2_MISC + MISC_DT) // LANES)),
                  pl.BlockSpec((SSD_CONV, w), const), pl.BlockSpec((1, w), const),
                  pl.BlockSpec((SSD_CONV, bcw), const), pl.BlockSpec((1, bcw), const),
                  pl.BlockSpec((1, LANES), const), pl.BlockSpec((1, LANES), const),
                  pl.BlockSpec((1, w), const), pl.BlockSpec((1, w), const),
                  pl.BlockSpec((LANES, w), const)],
        out_specs=pl.BlockSpec((q, w), lambda b, c: (b * nc + c, 0)),
        out_shape=jax.ShapeDtypeStruct((bsz * seq, w), MXU_DTYPE),
        scratch_shapes=[pltpu.VMEM((q + SUBLANES, w), F32),
                        pltpu.VMEM((q + SUBLANES, bcw), F32),
                        pltpu.VMEM((SSD_GROUPS, SSD_STATE, gw), F32)],
        compiler_params=_cparams(("arbitrary", "arbitrary")),
        name="ssd_mixer",
    )(u1, u1, u1, u2, cw[:, :w], conv_b[:w].reshape(1, w), cw[:, w:], conv_b[w:].reshape(1, bcw),
      pad(dt_bias), pad(a_log), jnp.repeat(d_skip, HEAD).reshape(1, w), norm_g.reshape(1, w),
      jnp.asarray(esel, MXU_DTYPE))


def _unit_lower_inverse(l_mats, blk_mask, eye_f):
    mm = lambda xs, ys: [_dot(x, y) for x, y in zip(xs, ys)]
    mx = lambda xs: [_mx(x) for x in xs]
    add = lambda xs, ys: [x + y for x, y in zip(xs, ys)]
    d = [jnp.where(blk_mask, l, 0.0) for l in l_mats]
    o = mx([l - x for l, x in zip(l_mats, d)])
    d1 = mx(d)
    d2 = mx(mm(d1, d1))
    t = [eye_f + x for x in d]
    yield
    d4 = mx(mm(d2, d2))
    t = add(t, mm(d2, mx(t)))
    yield
    d8 = mx(mm(d4, d4))
    t = add(t, mm(d4, mx(t)))
    yield
    t_d = add(t, mm(d8, mx(t)))
    t_dm = mx(t_d)
    yield
    m1 = mm(t_dm, o)
    m1m = mx(m1)
    yield
    m2 = mm(m1m, m1m)
    yield
    m3 = mm(m1m, mx(m2))
    w = mx([x + y + z for x, y, z in zip(m1, m2, m3)])
    yield
    return add(t_d, mm(w, t_dm))


def _rwkv_masks():
    lane = lax.broadcasted_iota(jnp.int32, (1, LANES), 1)
    row = lax.broadcasted_iota(jnp.int32, (LANES, LANES), 0)
    col = lax.broadcasted_iota(jnp.int32, (LANES, LANES), 1)
    same = (row >= HEAD) == (col >= HEAD)
    rs = row & (HEAD - 1)
    cs = col & (HEAD - 1)
    eye = row == col
    return dict(lane_lo=lane < HEAD, strict=same & (rs > cs), incl=same & (rs >= cs),
                blk=(row >> 4) == (col >> 4), eye=eye, eye_f=jnp.where(eye, 1.0, 0.0))


def _rwkv_head_sum(x, ones_blk):
    cols = [_dot_split_lhs(x[:, t * LANES:(t + 1) * LANES], ones_blk, 2)
            for t in range(x.shape[1] // LANES)]
    return jnp.concatenate(cols, axis=1)


def _rwkv_prepare(src, buf, mu_ref, prm, masks, first):
    cl = RWKV_CHUNK
    w0_ref, w2_ref, a0_ref, a2_ref, g2_ref, kk_ref, ka_ref, rk_ref, ones_ref = prm
    if first:
        buf[0:SUBLANES, :] = jnp.zeros((SUBLANES, buf.shape[1]), F32)
    else:
        buf[0:SUBLANES, :] = buf[cl:cl + SUBLANES, :]
    cur = src[...]
    buf[SUBLANES:SUBLANES + cl, :] = cur
    prev = pltpu.roll(buf[0:SUBLANES + cl, :], 1, 0)[SUBLANES:, :]
    row = cur + (prev - cur) * mu_ref[...]
    w = W_MIX
    r, k, v = (row[:, i * w:(i + 1) * w] for i in range(3))
    misc = row[:, U2_MISC:]
    pw_pa = misc[:, 0:2 * LANES]
    ones_blk = ones_ref[...]
    yield
    w_log = -_softplus(-(w0_ref[...] + _dot(_mx(jnp.tanh(pw_pa)), w2_ref[...]))) - 0.5
    lw = -jnp.exp(w_log)
    iclr = _sigmoid(a0_ref[...] + _dot(_mx(pw_pa), a2_ref[...]))
    gate = _dot(_mx(_sigmoid(misc[:, LANES:])), g2_ref[...])
    yield

    row_c = lax.broadcasted_iota(jnp.int32, (cl, cl), 0)
    col_c = lax.broadcasted_iota(jnp.int32, (cl, cl), 1)
    tril_c = jnp.where(row_c >= col_c, 1.0, 0.0).astype(MXU_DTYPE)
    lane_lo = masks["lane_lo"]
    stack2 = lambda x: jnp.concatenate([jnp.where(lane_lo, x, 0.0), jnp.where(lane_lo, 0.0, x)], axis=0)
    out = dict(a2=[], r2=[], b2=[], k2=[], v2=[], bkp=[])
    p_ends, bonuses = [], []
    for p in range(r.shape[1] // LANES):
        ps = slice(p * LANES, (p + 1) * LANES)
        r_p, k_p, v_p, lw_p, iclr_p = r[:, ps], k[:, ps], v[:, ps], lw[:, ps], iclr[:, ps]
        kk = k_p * kk_ref[:, ps]
        kk = kk / jnp.maximum(jnp.sqrt(_dot_split_lhs(kk * kk, ones_blk, 2)), 1e-12)
        k_p = k_p * (1.0 + (iclr_p - 1.0) * ka_ref[:, ps])
        bonuses.append(_dot_split_lhs(r_p * k_p * rk_ref[:, ps], ones_blk, 2) * v_p)
        cum = _dot_split_rhs(tril_c, lw_p, 3)
        p_in = jnp.exp(cum)
        p_inv = jnp.exp(-cum)
        a_t = -(kk * jnp.exp(cum - lw_p))
        b_t = kk * iclr_p * p_inv
        k_t = k_p * p_inv
        p_end = p_in[cl - 1:cl, :]
        out["a2"].append(_mx(stack2(a_t)))
        out["r2"].append(_mx(stack2(r_p * p_in)))
        out["b2"].append(_mx(stack2(b_t)))
        out["k2"].append(_mx(stack2(k_t)))
        out["v2"].append(_mx(stack2(v_p)))
        out["bkp"].append(_mx(jnp.concatenate([stack2(b_t * p_end), stack2(k_t * p_end)], axis=0).T))
        p_ends.append(p_end)
        yield
    out.update(p_end=jnp.concatenate(p_ends, axis=1), bonus=jnp.concatenate(bonuses, axis=1), gate=gate)
    return out


def _rwkv_scan(ops_list, st_ref, ybuf, lnw_ref, lnb_ref, ones_ref, masks):
    cl = RWKV_CHUNK
    gather = lambda name: [x for ops in ops_list for x in ops[name]]
    a2, r2, b2, k2, v2, bkp_t = (gather(name) for name in _RWKV_STAGED)
    n_tiles = len(ops_list[0]["a2"])
    tiles = range(len(a2))
    cat0 = lambda *xs: jnp.concatenate(xs, axis=0)
    cat1 = lambda *xs: jnp.concatenate(xs, axis=1)
    gram = [_dot_nt(cat0(a2[t], r2[t]), cat0(b2[t], k2[t])) for t in tiles]
    yield
    a_ab = [jnp.where(masks["strict"], g[0:LANES, 0:LANES], 0.0) for g in gram]
    a_ak = [_mx(jnp.where(masks["strict"], g[0:LANES, LANES:], 0.0)) for g in gram]
    a_rb = [_mx(jnp.where(masks["incl"], g[LANES:, 0:LANES], 0.0)) for g in gram]
    a_rk = [_mx(jnp.where(masks["incl"], g[LANES:, LANES:], 0.0)) for g in gram]
    t_inv = yield from _unit_lower_inverse(a_ab, masks["blk"], masks["eye_f"])
    st = [st_ref[t] for t in tiles]
    st_m = [_mx(s) for s in st]
    y0 = [_dot(cat1(a2[t], a_ak[t]), cat0(st_m[t], v2[t])) for t in tiles]
    yield
    sa = [_mx(_dot(_mx(t_inv[t]), _mx(y0[t]))) for t in tiles]
    yield
    o2 = [_dot(cat1(r2[t], a_rb[t], a_rk[t]), cat0(st_m[t], sa[t], v2[t])) for t in tiles]
    yield
    upd = [_dot(bkp_t[t], cat0(sa[t], v2[t])) for t in tiles]
    for t in tiles:
        b, p = divmod(t, n_tiles)
        ps = slice(p * LANES, (p + 1) * LANES)
        ybuf[b, :, ps] = o2[t][0:cl, :] + o2[t][cl:, :]
        p_end = ops_list[b]["p_end"][:, ps]
        p_col = jnp.sum(jnp.where(masks["eye"], p_end, 0.0), axis=1, keepdims=True)
        st_ref[t] = st[t] * p_col + upd[t]
    yield
    ones_blk = ones_ref[...]
    inv_n = 1.0 / HEAD
    outs = []
    for b, ops in enumerate(ops_list):
        y = ybuf[b]
        mean = _rwkv_head_sum(y, ones_blk) * inv_n
        d = y - mean
        var = _rwkv_head_sum(d * d, ones_blk) * inv_n
        y = d * lax.rsqrt(var + RWKV_LN_EPS) * lnw_ref[...] + lnb_ref[...]
        outs.append((y + ops["bonus"]) * ops["gate"])
    return outs


def _run(gen):
    while True:
        try:
            next(gen)
        except StopIteration as stop:
            return stop.value


def _interleave(scan_gen, prep_gens):
    gens = [scan_gen] + list(prep_gens)
    results, done = [None] * len(gens), [False] * len(gens)
    while not all(done):
        for i, gen in enumerate(gens):
            if not done[i]:
                try:
                    next(gen)
                except StopIteration as stop:
                    results[i], done[i] = stop.value, True
    return results[0], results[1:]


_RWKV_STAGED = ("a2", "r2", "b2", "k2", "v2", "bkp")
RWKV_SEQS_PER_STEP = 2


def _rwkv_body(*refs):
    cl = RWKV_CHUNK
    first_src, odd_src, next_src, mu_ref = refs[0:4]
    prm = refs[4:12] + (refs[14],)
    lnw_ref, lnb_ref, ones_ref = refs[12], refs[13], refs[14]
    o_ref = refs[15]
    buf = refs[16]
    st_ref, ybuf = refs[17], refs[18]
    staged = dict(zip(_RWKV_STAGED, refs[19:25]))
    s_pend, s_bonus, s_gate = refs[25], refs[26], refs[27]
    masks = _rwkv_masks()
    seqs = range(o_ref.shape[0])
    n_tiles = st_ref.shape[0] // len(seqs)

    def prepare(src, b, first):
        return _rwkv_prepare(src.at[b], buf.at[b], mu_ref, prm, masks, first)

    def stash(b, ops):
        for name in _RWKV_STAGED:
            for p in range(n_tiles):
                staged[name][b * n_tiles + p] = ops[name][p]
        s_pend[b] = ops["p_end"]
        s_bonus[b] = ops["bonus"]
        s_gate[b] = ops["gate"]

    def staged_ops(b):
        ops = {name: [staged[name][b * n_tiles + p] for p in range(n_tiles)] for name in _RWKV_STAGED}
        ops.update(p_end=s_pend[b], bonus=s_bonus[b], gate=s_gate[b])
        return ops

    @pl.when(pl.program_id(1) == 0)
    def _():
        st_ref[...] = jnp.zeros(st_ref.shape, F32)
        for b in seqs:
            stash(b, _run(prepare(first_src, b, True)))

    scan = lambda ops_list: _rwkv_scan(ops_list, st_ref, ybuf, lnw_ref, lnb_ref, ones_ref, masks)
    y_even, odd = _interleave(scan([staged_ops(b) for b in seqs]), [prepare(odd_src, b, False) for b in seqs])
    for b in seqs:
        o_ref[b, 0:cl, :] = y_even[b].astype(o_ref.dtype)
    y_odd, nxt = _interleave(scan(odd), [prepare(next_src, b, False) for b in seqs])
    for b in seqs:
        o_ref[b, cl:2 * cl, :] = y_odd[b].astype(o_ref.dtype)
        stash(b, nxt[b])


def _rwkv(u2, bsz, seq, mu, w0, w2, a0, a2, g2, k_k, k_a, r_k, ln_w, ln_b):
    cl = RWKV_CHUNK
    nc = seq // cl
    nb = RWKV_SEQS_PER_STEP
    assert nc % 2 == 0 and bsz % nb == 0
    w = W_MIX
    mw = RWKV_MISC
    uw = u2.shape[1]
    n_tiles = w // LANES
    row1 = lambda vec: vec.reshape(1, -1)
    mu_row = jnp.zeros((uw,), F32).at[0:mu.shape[0]].set(mu)
    w2e = jnp.zeros((2 * LANES, w), F32).at[MISC_PW:MISC_PA].set(w2)
    a2e = jnp.zeros((2 * LANES, w), F32).at[MISC_PA:MISC_PG].set(a2)
    g2e = jnp.zeros((mw - LANES, w), F32).at[MISC_PG - LANES:MISC_DT - LANES].set(g2)
    ones_blk = np.kron(np.eye(LANES // HEAD, dtype=np.float32), np.ones((HEAD, HEAD), np.float32))

    def chunk_spec(chunk_of):
        return pl.BlockSpec((nb, cl, uw), lambda bb, j: (bb, chunk_of(j), 0))

    cvec = lambda width: pl.BlockSpec((1, width), lambda bb, j: (0, 0))
    cmat = lambda rows, width: pl.BlockSpec((rows, width), lambda bb, j: (0, 0))
    stage = lambda cols: pltpu.VMEM((nb * n_tiles, LANES, cols), MXU_DTYPE)
    out = pl.pallas_call(
        _rwkv_body,
        grid=(bsz // nb, nc // 2),
        in_specs=([chunk_spec(lambda j: 0), chunk_spec(lambda j: 2 * j + 1),
                   chunk_spec(lambda j: jnp.minimum(2 * j + 2, nc - 1)), cvec(uw)]
                  + [cvec(w), cmat(2 * LANES, w), cvec(w), cmat(2 * LANES, w), cmat(mw - LANES, w),
                     cvec(w), cvec(w), cvec(w), cvec(w), cvec(w), cmat(LANES, LANES)]),
        out_specs=pl.BlockSpec((nb, 2 * cl, w), lambda bb, j: (bb, j, 0)),
        out_shape=jax.ShapeDtypeStruct((bsz, seq, w), MXU_DTYPE),
        scratch_shapes=[pltpu.VMEM((nb, cl + SUBLANES, uw), F32),
                        pltpu.VMEM((nb * n_tiles, LANES, LANES), F32),
                        pltpu.VMEM((nb, cl, w), F32),
                        stage(LANES), stage(LANES), stage(LANES), stage(LANES), stage(LANES),
                        stage(2 * LANES),
                        pltpu.VMEM((nb, 1, w), F32), pltpu.VMEM((nb, cl, w), F32),
                        pltpu.VMEM((nb, cl, w), F32)],
        compiler_params=_cparams(("arbitrary", "arbitrary")),
        name="rwkv7_mixer",
    )(*([u2.reshape(bsz, seq, uw)] * 3), row1(mu_row), row1(w0), _mx(w2e), row1(a0), _mx(a2e), _mx(g2e), row1(k_k), row1(k_a),
      row1(r_k), row1(ln_w), row1(ln_b), jnp.asarray(ones_blk, MXU_DTYPE))
    return out.reshape(bsz * seq, w)


def _attn_body(q_ref, k_ref, v_ref, o_ref):
    d = q_ref.shape[1]
    hd = d // XATTN_HEADS
    scale = hd ** -0.5
    heads = [slice(h * hd, (h + 1) * hd) for h in range(XATTN_HEADS)]
    scores = [_dot_nt(q_ref[:, hs], k_ref[0, :, hs]) * scale for hs in heads]
    probs = []
    for s in scores:
        e = jnp.exp(s - jnp.max(s, axis=-1, keepdims=True))
        probs.append(_mx(e / jnp.sum(e, axis=-1, keepdims=True)))
    for hs, p in zip(heads, probs):
        o_ref[:, hs] = _dot(p, v_ref[0, :, hs]).astype(o_ref.dtype)


def _attention(q, k, v, bsz, seq, ts=1024):
    d = q.shape[1]
    n_mem = k.shape[1]
    ts = min(ts, seq)
    ns = seq // ts
    return pl.pallas_call(
        _attn_body,
        grid=(bsz, ns),
        in_specs=[pl.BlockSpec((ts, d), lambda b, s: (b * ns + s, 0)),
                  pl.BlockSpec((1, n_mem, d), lambda b, s: (b, 0, 0)),
                  pl.BlockSpec((1, n_mem, d), lambda b, s: (b, 0, 0))],
        out_specs=pl.BlockSpec((ts, d), lambda b, s: (b * ns + s, 0)),
        out_shape=jax.ShapeDtypeStruct((bsz * seq, d), MXU_DTYPE),
        compiler_params=_cparams(("arbitrary", "arbitrary")),
        name="mem_attention",
    )(q, k, v)


def _rwkv_side(mat, axis):
    ssd_in = U1_COLS + W_MIX // HEAD
    take = lambda lo, hi: lax.slice_in_dim(mat, lo, hi, axis=axis)
    pad_shape = list(mat.shape)
    pad_shape[axis] = U2_COLS - (mat.shape[axis] - U1_COLS)
    return jnp.concatenate([take(ssd_in, mat.shape[axis]), take(U1_COLS, ssd_in),
                            jnp.zeros(pad_shape, mat.dtype)], axis=axis)


def kernel(x, mem, norm_mix_g, w_in, ssd_conv_w, ssd_conv_b, ssd_dt_bias, ssd_a_log, ssd_d, ssd_norm_g, rwkv_mu, rwkv_w0, rwkv_w2, rwkv_a0, rwkv_a2, rwkv_g2, rwkv_k_k, rwkv_k_a, rwkv_r_k, rwkv_ln_w, rwkv_ln_b, w_out, norm_x_g, norm_mem_g, xattn_wq, xattn_wk, xattn_wv, xattn_wo, norm_ffn_g, ffn_w1, ffn_w2, final_norm_g):
    bsz, seq, d = x.shape
    n_mem = mem.shape[1]
    xr = x.reshape(bsz * seq, d)
    memr = mem.reshape(bsz * n_mem, d)
    n_layers = w_in.shape[0]
    for l in range(n_layers):
        w_in_t = w_in[l].T
        u1 = _matmul_resident(xr, w_in_t, F32, n=U1_COLS, b_is_transposed=True,
                              norm_g=norm_mix_g[l], tm=512, tn=U1_COLS // 2, name="in_proj_ssd")
        u2 = _matmul_resident(xr, _rwkv_side(w_in_t, 0), F32, b_is_transposed=True,
                              norm_g=norm_mix_g[l], tm=512, tn=U2_COLS // 2, name="in_proj_rwkv")
        y_ssd = _ssd(u1, u2, bsz, seq, ssd_conv_w[l], ssd_conv_b[l], ssd_dt_bias[l], ssd_a_log[l],
                     ssd_d[l], ssd_norm_g[l])
        y_rwkv = _rwkv(u2, bsz, seq, rwkv_mu[l], rwkv_w0[l], rwkv_w2[l], rwkv_a0[l], rwkv_a2[l],
                       rwkv_g2[l], rwkv_k_k[l], rwkv_k_a[l], rwkv_r_k[l].reshape(-1),
                       rwkv_ln_w[l], rwkv_ln_b[l])
        wo = _mx(w_out[l])
        xr, h, wo_attn = _matmul_rows([(y_ssd, wo, 0), (y_rwkv, wo, W_MIX)], xr, norm_x_g[l],
                                      MXU_DTYPE, side_cast=xattn_wo[l], name="out_proj")

        m = _rmsnorm(memr, norm_mem_g[l], MXU_DTYPE)
        q = _matmul_resident(h, xattn_wq[l], MXU_DTYPE, tm=2048, name="q_proj")
        kx = _matmul_resident(m, xattn_wk[l], MXU_DTYPE, name="k_proj")
        vx = _matmul_resident(m, xattn_wv[l], MXU_DTYPE, name="v_proj")
        o = _attention(q, kx.reshape(bsz, n_mem, d), vx.reshape(bsz, n_mem, d), bsz, seq)
        xr, h, w2 = _matmul_rows([(o, wo_attn, 0)], xr, norm_ffn_g[l], MXU_DTYPE,
                                 side_cast=ffn_w2[l], name="o_proj")

        hid = _matmul_resident(h, ffn_w1[l], MXU_DTYPE, act="relu2", tm=2048, name="ffn_up")
        down = [(hid, w2, 0)]
        if l + 1 < n_layers:
            xr, _ = _matmul_rows(down, xr, norm_mix_g[l + 1], MXU_DTYPE, name="ffn_down")
        else:
            out = _matmul_rows(down, xr, final_norm_g, x.dtype, emit_x=False, name="ffn_down")
    return out.reshape(bsz, seq, d)
```

```python
import functools

import numpy as np
import jax
import jax.numpy as jnp
from jax import lax
from jax.experimental import pallas as pl
from jax.experimental.pallas import tpu as pltpu

F32 = jnp.float32
MXU_DTYPE = jnp.bfloat16

NORM_EPS = 1e-6
RWKV_LN_EPS = 64e-5

HEAD = 64
SSD_STATE = 128
SSD_GROUPS = 2
SSD_CHUNK = 128
SSD_CONV = 4
RWKV_CHUNK = 64
RWKV_DECAY_RANK = 96
RWKV_AAA_RANK = 96
RWKV_GATE_RANK = 256
XATTN_HEADS = 4

LANES = 128
SUBLANES = 8
VMEM_LIMIT = 56 * 1024 * 1024

W_MIX = 1024
SSD_BC = 2 * SSD_GROUPS * SSD_STATE
U1_COLS = 2 * W_MIX + SSD_BC
U2_MISC = 3 * W_MIX
RWKV_MISC = 512
U2_COLS = U2_MISC + RWKV_MISC
MISC_PW = 0
MISC_PA = MISC_PW + RWKV_DECAY_RANK
MISC_PG = MISC_PA + RWKV_AAA_RANK
MISC_DT = MISC_PG + RWKV_GATE_RANK
SSD_DT_LANE = MISC_DT % LANES


def _mx(a):
    return a.astype(MXU_DTYPE)


def _dot(a, b):
    return jnp.dot(a, b, preferred_element_type=F32)


def _dot_nt(a, b):
    return lax.dot_general(a, b, (((1,), (1,)), ((), ())), preferred_element_type=F32)


def _split(v, parts):
    out = []
    rem = v
    for _ in range(parts):
        p = rem.astype(MXU_DTYPE)
        out.append(p)
        rem = rem - p.astype(F32)
    return out


def _dot_split_rhs(a01, v, parts):
    acc = None
    for p in _split(v, parts):
        t = _dot(a01, p)
        acc = t if acc is None else acc + t
    return acc


def _dot_split_lhs(v, b01, parts):
    acc = None
    for p in _split(v, parts):
        t = _dot(p, b01)
        acc = t if acc is None else acc + t
    return acc


def _sigmoid(x):
    return 1.0 / (1.0 + jnp.exp(-x))


def _softplus(x):
    return jnp.maximum(x, 0.0) + jnp.log(1.0 + jnp.exp(-jnp.abs(x)))


def _cparams(sem):
    return pltpu.CompilerParams(dimension_semantics=sem, vmem_limit_bytes=VMEM_LIMIT)


def _mm_resident_body(*refs, act, b_is_transposed, normalize):
    if normalize:
        a_ref, g_ref, b_ref, o_ref, bm_ref = refs
    else:
        a_ref, b_ref, o_ref, bm_ref = refs

    @pl.when(pl.program_id(1) == 0)
    def _():
        b = b_ref[...]
        bm_ref[...] = (b.T if b_is_transposed else b).astype(MXU_DTYPE)

    a = a_ref[...]
    if normalize:
        ms = jnp.mean(a * a, axis=-1, keepdims=True)
        a = (a * lax.rsqrt(ms + NORM_EPS) * g_ref[...]).astype(MXU_DTYPE)
    r = _dot(a, bm_ref[...])
    if act == "relu2":
        r = jnp.square(jnp.maximum(r, 0.0))
    o_ref[...] = r.astype(o_ref.dtype)


def _matmul_resident(a, b, out_dtype, act=None, n=None, b_is_transposed=False, norm_g=None,
                     tm=1024, tn=1024, name="matmul_resident"):
    m, kdim = a.shape
    n = b.shape[0 if b_is_transposed else 1] if n is None else n
    tm, tn = min(tm, m), min(tn, n)
    assert m % tm == 0 and n % tn == 0
    b_spec = (pl.BlockSpec((tn, kdim), lambda j, i: (j, 0)) if b_is_transposed
              else pl.BlockSpec((kdim, tn), lambda j, i: (0, j)))
    in_specs, args = [pl.BlockSpec((tm, kdim), lambda j, i: (i, 0))], [a]
    if norm_g is not None:
        in_specs.append(pl.BlockSpec((1, kdim), lambda j, i: (0, 0)))
        args.append(norm_g.reshape(1, kdim))
    return pl.pallas_call(
        functools.partial(_mm_resident_body, act=act, b_is_transposed=b_is_transposed,
                          normalize=norm_g is not None),
        grid=(n // tn, m // tm),
        in_specs=in_specs + [b_spec],
        out_specs=pl.BlockSpec((tm, tn), lambda j, i: (i, j)),
        out_shape=jax.ShapeDtypeStruct((m, n), out_dtype),
        scratch_shapes=[pltpu.VMEM((kdim, tn), MXU_DTYPE)],
        compiler_params=_cparams(("arbitrary", "arbitrary")),
        name=name,
    )(*args, b)


def _mm_rows_body(*refs, n_pairs, nk, emit_x, side):
    ab = refs[:2 * n_pairs]
    rest = list(refs[2 * n_pairs:])
    res_ref, g_ref = rest.pop(0), rest.pop(0)
    side_in = rest.pop(0) if side else None
    x_ref = rest.pop(0) if emit_x else None
    h_ref = rest.pop(0)
    side_out = rest.pop(0) if side else None
    acc_ref = rest.pop(0) if nk > 1 else None
    if side:
        side_out[...] = side_in[...].astype(side_out.dtype)
    part = None
    for p in range(n_pairs):
        t = _dot(ab[2 * p][...], ab[2 * p + 1][...])
        part = t if part is None else part + t

    def finish(acc):
        x = res_ref[...] + acc
        if emit_x:
            x_ref[...] = x
        ms = jnp.mean(x * x, axis=-1, keepdims=True)
        h_ref[...] = (x * lax.rsqrt(ms + NORM_EPS) * g_ref[...]).astype(h_ref.dtype)

    if nk == 1:
        finish(part)
        return

    k = pl.program_id(1)

    @pl.when(k == 0)
    def _():
        acc_ref[...] = part

    @pl.when(k > 0)
    def _():
        acc_ref[...] += part

    @pl.when(k == nk - 1)
    def _():
        finish(acc_ref[...])


def _matmul_rows(pairs, res, g, h_dtype, emit_x=True, side_cast=None, tm=512, tk=2048,
                 name="matmul_rows"):
    m, n = res.shape
    tm = min(tm, m)
    kdims = [a.shape[1] for a, _, _ in pairs]
    tk = min(tk, min(kdims))
    nk = kdims[0] // tk
    assert all(kd == nk * tk for kd in kdims) and all(row0 % tk == 0 for _, _, row0 in pairs)
    in_specs, args = [], []
    for a, b, row0 in pairs:
        in_specs += [pl.BlockSpec((tm, tk), lambda i, k: (i, k)),
                     pl.BlockSpec((tk, n), lambda i, k, blk0=row0 // tk: (blk0 + k, 0))]
        args += [a, b]
    row_spec = pl.BlockSpec((tm, n), lambda i, k: (i, 0))
    in_specs += [row_spec, pl.BlockSpec((1, n), lambda i, k: (0, 0))]
    args += [res, g.reshape(1, n)]
    out_specs, out_shape = [row_spec], [jax.ShapeDtypeStruct((m, n), h_dtype)]
    if emit_x:
        out_specs, out_shape = [row_spec] + out_specs, [jax.ShapeDtypeStruct((m, n), F32)] + out_shape
    if side_cast is not None:
        assert nk == 1 and side_cast.shape[0] % (m // tm) == 0
        slab = pl.BlockSpec((side_cast.shape[0] // (m // tm), side_cast.shape[1]), lambda i, k: (i, 0))
        in_specs.append(slab)
        args.append(side_cast)
        out_specs.append(slab)
        out_shape.append(jax.ShapeDtypeStruct(side_cast.shape, MXU_DTYPE))
    outs = pl.pallas_call(
        functools.partial(_mm_rows_body, n_pairs=len(pairs), nk=nk, emit_x=emit_x,
                          side=side_cast is not None),
        grid=(m // tm, nk),
        in_specs=in_specs,
        out_specs=out_specs,
        out_shape=out_shape,
        scratch_shapes=[pltpu.VMEM((tm, n), F32)] if nk > 1 else [],
        compiler_params=_cparams(("arbitrary", "arbitrary")),
        name=name,
    )(*args)
    return tuple(outs) if len(outs) > 1 else outs[0]


def _ssd_body(z_ref, xs_ref, bc_ref, m_ref, cwx_ref, cbx_ref, cwbc_ref, cbbc_ref, dtb_ref,
              alog_ref, dvec_ref, ng_ref, esel_ref, o_ref, xbuf, bcbuf, st_ref):
    q = SSD_CHUNK
    n = SSD_STATE
    gw = st_ref.shape[2]
    c = pl.program_id(1)

    @pl.when(c == 0)
    def _():
        xbuf[0:SUBLANES, :] = jnp.zeros((SUBLANES, xbuf.shape[1]), F32)
        bcbuf[0:SUBLANES, :] = jnp.zeros((SUBLANES, bcbuf.shape[1]), F32)
        st_ref[...] = jnp.zeros(st_ref.shape, F32)

    @pl.when(c > 0)
    def _():
        xbuf[0:SUBLANES, :] = xbuf[q:q + SUBLANES, :]
        bcbuf[0:SUBLANES, :] = bcbuf[q:q + SUBLANES, :]

    xbuf[SUBLANES:SUBLANES + q, :] = xs_ref[...]
    bcbuf[SUBLANES:SUBLANES + q, :] = bc_ref[...]

    def conv_silu(buf, w_ref, b_ref):
        full = buf[0:SUBLANES + q, :]
        acc = None
        for k in range(SSD_CONV):
            back = SSD_CONV - 1 - k
            rows = full if back == 0 else pltpu.roll(full, back, 0)
            t = rows[SUBLANES:, :] * w_ref[k:k + 1, :]
            acc = t if acc is None else acc + t
        acc = acc + b_ref[...]
        return acc * _sigmoid(acc)

    xc = conv_silu(xbuf, cwx_ref, cbx_ref)
    bcc = conv_silu(bcbuf, cwbc_ref, cbbc_ref)
    g_n = SSD_GROUPS * n

    lane = lax.broadcasted_iota(jnp.int32, (1, LANES), 1)
    n_heads = SSD_GROUPS * gw // HEAD
    dmask = (lane >= SSD_DT_LANE) & (lane < SSD_DT_LANE + n_heads)
    dt = jnp.where(dmask, _softplus(m_ref[...] + dtb_ref[...]), 0.0)
    a = jnp.where(dmask, -jnp.exp(alog_ref[...]), 0.0)
    da = dt * a
    row = lax.broadcasted_iota(jnp.int32, (q, q), 0)
    col = lax.broadcasted_iota(jnp.int32, (q, q), 1)
    tri = row >= col
    tril = jnp.where(tri, 1.0, 0.0).astype(MXU_DTYPE)
    cs = _dot_split_rhs(tril, da, 3)
    ecs = jnp.exp(cs)
    dte = jnp.exp(cs[q - 1:q, :] - cs)
    esel = esel_ref[...]
    dt_e = _dot_split_lhs(dt, esel, 2)
    ecs_e = _dot_split_lhs(ecs, esel, 2)
    dte_e = _dot_split_lhs(dte, esel, 2)
    cs_t = cs.T

    xdt = xc * dt_e
    xdt_m = _mx(xdt)
    xd_m = _mx(xdt * dte_e)
    lane_lo = lane < HEAD

    y_cols = []
    for g in range(SSD_GROUPS):
        bg = bcc[:, g * n:(g + 1) * n]
        cg = _mx(bcc[:, g_n + g * n:g_n + (g + 1) * n])
        cb = _dot_nt(cg, _mx(bg))
        gs = slice(g * gw, (g + 1) * gw)
        st = st_ref[g]
        y_off = _dot(cg, _mx(st)) * ecs_e[:, gs]
        st_ref[g] = st * ecs_e[q - 1:q, gs] + _dot(_mx(bg.T), xd_m[:, gs])
        for pr in range(gw // LANES):
            h0 = (g * gw + pr * LANES) // HEAD
            ps = slice(g * gw + pr * LANES, g * gw + (pr + 1) * LANES)
            res = []
            for hh in (h0, h0 + 1):
                li = SSD_DT_LANE + hh
                seg = cs[:, li:li + 1] - cs_t[li:li + 1, :]
                lm = jnp.where(tri, jnp.exp(jnp.where(tri, seg, 0.0)), 0.0)
                res.append(_dot(_mx(cb * lm), xdt_m[:, ps]))
            y_diag = jnp.where(lane_lo, res[0], res[1])
            y_cols.append(y_diag + y_off[:, pr * LANES:(pr + 1) * LANES])
    y = jnp.concatenate(y_cols, axis=1) + xc * dvec_ref[...]
    zz = z_ref[...]
    y = y * (zz * _sigmoid(zz))
    outs = []
    for g in range(SSD_GROUPS):
        yg = y[:, g * gw:(g + 1) * gw]
        ms = jnp.mean(yg * yg, axis=-1, keepdims=True)
        outs.append(yg * lax.rsqrt(ms + NORM_EPS))
    y = jnp.concatenate(outs, axis=1) * ng_ref[...]
    o_ref[...] = y.astype(o_ref.dtype)


def _ssd(u1, u2, bsz, seq, conv_w, conv_b, dt_bias, a_log, d_skip, norm_g):
    q = SSD_CHUNK
    nc = seq // q
    w = W_MIX
    gw = w // SSD_GROUPS
    heads = w // HEAD
    bcw = SSD_BC
    cw = conv_w[:, 0, :]
    pad = lambda v: jnp.zeros((1, LANES), F32).at[0, SSD_DT_LANE:SSD_DT_LANE + heads].set(v)
    esel = np.zeros((LANES, w), np.float32)
    for h in range(heads):
        esel[SSD_DT_LANE + h, h * HEAD:(h + 1) * HEAD] = 1.0
    rowblk = lambda cb: (lambda b, c: (b * nc + c, cb))
    const = lambda b, c: (0, 0)
    return pl.pallas_call(
        _ssd_body,
        grid=(bsz, nc),
        in_specs=[pl.BlockSpec((q, w), rowblk(0)),
                  pl.BlockSpec((q, w), rowblk(1)),
                  pl.BlockSpec((q, bcw), rowblk(2 * w // bcw)),
                  pl.BlockSpec((q, LANES), rowblk((U2_MISC + MISC_DT) // LANES)),
                  pl.BlockSpec((SSD_CONV, w), const), pl.BlockSpec((1, w), const),
                  pl.BlockSpec((SSD_CONV, bcw), const), pl.BlockSpec((1, bcw), const),
                  pl.BlockSpec((1, LANES), const), pl.BlockSpec((1, LANES), const),
                  pl.BlockSpec((1, w), const), pl.BlockSpec((1, w), const),
                  pl.BlockSpec((LANES, w), const)],
        out_specs=pl.BlockSpec((q, w), lambda b, c: (b * nc + c, 0)),
        out_shape=jax.ShapeDtypeStruct((bsz * seq, w), MXU_DTYPE),
        scratch_shapes=[pltpu.VMEM((q + SUBLANES, w), F32),
                        pltpu.VMEM((q + SUBLANES, bcw), F32),
                        pltpu.VMEM((SSD_GROUPS, SSD_STATE, gw), F32)],
        compiler_params=_cparams(("arbitrary", "arbitrary")),
        name="ssd_mixer",
    )(u1, u1, u1, u2, cw[:, :w], conv_b[:w].reshape(1, w), cw[:, w:], conv_b[w:].reshape(1, bcw),
      pad(dt_bias), pad(a_log), jnp.repeat(d_skip, HEAD).reshape(1, w), norm_g.reshape(1, w),
      jnp.asarray(esel, MXU_DTYPE))


def _unit_lower_inverse(l_mats, blk_mask, eye_f):
    mm = lambda xs, ys: [_dot(x, y) for x, y in zip(xs, ys)]
    mx = lambda xs: [_mx(x) for x in xs]
    add = lambda xs, ys: [x + y for x, y in zip(xs, ys)]
    d = [jnp.where(blk_mask, l, 0.0) for l in l_mats]
    o = mx([l - x for l, x in zip(l_mats, d)])
    d1 = mx(d)
    d2 = mx(mm(d1, d1))
    t = [eye_f + x for x in d]
    yield
    d4 = mx(mm(d2, d2))
    t = add(t, mm(d2, mx(t)))
    yield
    d8 = mx(mm(d4, d4))
    t = add(t, mm(d4, mx(t)))
    yield
    t_d = add(t, mm(d8, mx(t)))
    t_dm = mx(t_d)
    yield
    m1 = mm(t_dm, o)
    m1m = mx(m1)
    yield
    m2 = mm(m1m, m1m)
    yield
    m3 = mm(m1m, mx(m2))
    w = mx([x + y + z for x, y, z in zip(m1, m2, m3)])
    yield
    return add(t_d, mm(w, t_dm))


def _rwkv_masks():
    lane = lax.broadcasted_iota(jnp.int32, (1, LANES), 1)
    row = lax.broadcasted_iota(jnp.int32, (LANES, LANES), 0)
    col = lax.broadcasted_iota(jnp.int32, (LANES, LANES), 1)
    same = (row >= HEAD) == (col >= HEAD)
    rs = row & (HEAD - 1)
    cs = col & (HEAD - 1)
    eye = row == col
    return dict(lane_lo=lane < HEAD, strict=same & (rs > cs), incl=same & (rs >= cs),
                blk=(row >> 4) == (col >> 4), eye=eye, eye_f=jnp.where(eye, 1.0, 0.0))


def _rwkv_head_sum(x, ones_blk):
    cols = [_dot_split_lhs(x[:, t * LANES:(t + 1) * LANES], ones_blk, 2)
            for t in range(x.shape[1] // LANES)]
    return jnp.concatenate(cols, axis=1)


def _rwkv_prepare(src, buf, mu_ref, prm, masks, first):
    cl = RWKV_CHUNK
    w0_ref, w2_ref, a0_ref, a2_ref, g2_ref, kk_ref, ka_ref, rk_ref, ones_ref = prm
    if first:
        buf[0:SUBLANES, :] = jnp.zeros((SUBLANES, buf.shape[1]), F32)
    else:
        buf[0:SUBLANES, :] = buf[cl:cl + SUBLANES, :]
    cur = src[...]
    buf[SUBLANES:SUBLANES + cl, :] = cur
    prev = pltpu.roll(buf[0:SUBLANES + cl, :], 1, 0)[SUBLANES:, :]
    row = cur + (prev - cur) * mu_ref[...]
    w = W_MIX
    r, k, v = (row[:, i * w:(i + 1) * w] for i in range(3))
    misc = row[:, U2_MISC:]
    pw_pa = misc[:, 0:2 * LANES]
    ones_blk = ones_ref[...]
    yield
    w_log = -_softplus(-(w0_ref[...] + _dot(_mx(jnp.tanh(pw_pa)), w2_ref[...]))) - 0.5
    lw = -jnp.exp(w_log)
    iclr = _sigmoid(a0_ref[...] + _dot(_mx(pw_pa), a2_ref[...]))
    gate = _dot(_mx(_sigmoid(misc[:, LANES:])), g2_ref[...])
    yield

    row_c = lax.broadcasted_iota(jnp.int32, (cl, cl), 0)
    col_c = lax.broadcasted_iota(jnp.int32, (cl, cl), 1)
    tril_c = jnp.where(row_c >= col_c, 1.0, 0.0).astype(MXU_DTYPE)
    lane_lo = masks["lane_lo"]
    stack2 = lambda x: jnp.concatenate([jnp.where(lane_lo, x, 0.0), jnp.where(lane_lo, 0.0, x)], axis=0)
    out = dict(a2=[], r2=[], b2=[], k2=[], v2=[], bkp=[])
    p_ends, bonuses = [], []
    for p in range(r.shape[1] // LANES):
        ps = slice(p * LANES, (p + 1) * LANES)
        r_p, k_p, v_p, lw_p, iclr_p = r[:, ps], k[:, ps], v[:, ps], lw[:, ps], iclr[:, ps]
        kk = k_p * kk_ref[:, ps]
        kk = kk / jnp.maximum(jnp.sqrt(_dot_split_lhs(kk * kk, ones_blk, 2)), 1e-12)
        k_p = k_p * (1.0 + (iclr_p - 1.0) * ka_ref[:, ps])
        bonuses.append(_dot_split_lhs(r_p * k_p * rk_ref[:, ps], ones_blk, 2) * v_p)
        cum = _dot_split_rhs(tril_c, lw_p, 3)
        p_in = jnp.exp(cum)
        p_inv = jnp.exp(-cum)
        a_t = -(kk * jnp.exp(cum - lw_p))
        b_t = kk * iclr_p * p_inv
        k_t = k_p * p_inv
        p_end = p_in[cl - 1:cl, :]
        out["a2"].append(_mx(stack2(a_t)))
        out["r2"].append(_mx(stack2(r_p * p_in)))
        out["b2"].append(_mx(stack2(b_t)))
        out["k2"].append(_mx(stack2(k_t)))
        out["v2"].append(_mx(stack2(v_p)))
        out["bkp"].append(_mx(jnp.concatenate([stack2(b_t * p_end), stack2(k_t * p_end)], axis=0).T))
        p_ends.append(p_end)
        yield
    out.update(p_end=jnp.concatenate(p_ends, axis=1), bonus=jnp.concatenate(bonuses, axis=1), gate=gate)
    return out


def _rwkv_scan(ops_list, st_ref, ybuf, lnw_ref, lnb_ref, ones_ref, masks):
    cl = RWKV_CHUNK
    gather = lambda name: [x for ops in ops_list for x in ops[name]]
    a2, r2, b2, k2, v2, bkp_t = (gather(name) for name in _RWKV_STAGED)
    n_tiles = len(ops_list[0]["a2"])
    tiles = range(len(a2))
    cat0 = lambda *xs: jnp.concatenate(xs, axis=0)
    cat1 = lambda *xs: jnp.concatenate(xs, axis=1)
    gram = [_dot_nt(cat0(a2[t], r2[t]), cat0(b2[t], k2[t])) for t in tiles]
    yield
    a_ab = [jnp.where(masks["strict"], g[0:LANES, 0:LANES], 0.0) for g in gram]
    a_ak = [_mx(jnp.where(masks["strict"], g[0:LANES, LANES:], 0.0)) for g in gram]
    a_rb = [_mx(jnp.where(masks["incl"], g[LANES:, 0:LANES], 0.0)) for g in gram]
    a_rk = [_mx(jnp.where(masks["incl"], g[LANES:, LANES:], 0.0)) for g in gram]
    t_inv = yield from _unit_lower_inverse(a_ab, masks["blk"], masks["eye_f"])
    st = [st_ref[t] for t in tiles]
    st_m = [_mx(s) for s in st]
    y0 = [_dot(cat1(a2[t], a_ak[t]), cat0(st_m[t], v2[t])) for t in tiles]
    yield
    sa = [_mx(_dot(_mx(t_inv[t]), _mx(y0[t]))) for t in tiles]
    yield
    o2 = [_dot(cat1(r2[t], a_rb[t], a_rk[t]), cat0(st_m[t], sa[t], v2[t])) for t in tiles]
    yield
    upd = [_dot(bkp_t[t], cat0(sa[t], v2[t])) for t in tiles]
    for t in tiles:
        b, p = divmod(t, n_tiles)
        ps = slice(p * LANES, (p + 1) * LANES)
        ybuf[b, :, ps] = o2[t][0:cl, :] + o2[t][cl:, :]
        p_end = ops_list[b]["p_end"][:, ps]
        p_col = jnp.sum(jnp.where(masks["eye"], p_end, 0.0), axis=1, keepdims=True)
        st_ref[t] = st[t] * p_col + upd[t]
    yield
    ones_blk = ones_ref[...]
    inv_n = 1.0 / HEAD
    outs = []
    for b, ops in enumerate(ops_list):
        y = ybuf[b]
        mean = _rwkv_head_sum(y, ones_blk) * inv_n
        d = y - mean
        var = _rwkv_head_sum(d * d, ones_blk) * inv_n
        y = d * lax.rsqrt(var + RWKV_LN_EPS) * lnw_ref[...] + lnb_ref[...]
        outs.append((y + ops["bonus"]) * ops["gate"])
    return outs


def _run(gen):
    while True:
        try:
            next(gen)
        except StopIteration as stop:
            return stop.value


def _interleave(scan_gen, prep_gens):
    gens = [scan_gen] + list(prep_gens)
    results, done = [None] * len(gens), [False] * len(gens)
    while not all(done):
        for i, gen in enumerate(gens):
            if not done[i]:
                try:
                    next(gen)
                except StopIteration as stop:
                    results[i], done[i] = stop.value, True
    return results[0], results[1:]


_RWKV_STAGED = ("a2", "r2", "b2", "k2", "v2", "bkp")
RWKV_SEQS_PER_STEP = 2


def _rwkv_body(*refs):
    cl = RWKV_CHUNK
    first_src, odd_src, next_src, mu_ref = refs[0:4]
    prm = refs[4:12] + (refs[14],)
    lnw_ref, lnb_ref, ones_ref = refs[12], refs[13], refs[14]
    o_ref = refs[15]
    buf = refs[16]
    st_ref, ybuf = refs[17], refs[18]
    staged = dict(zip(_RWKV_STAGED, refs[19:25]))
    s_pend, s_bonus, s_gate = refs[25], refs[26], refs[27]
    masks = _rwkv_masks()
    seqs = range(o_ref.shape[0])
    n_tiles = st_ref.shape[0] // len(seqs)

    def prepare(src, b, first):
        return _rwkv_prepare(src.at[b], buf.at[b], mu_ref, prm, masks, first)

    def stash(b, ops):
        for name in _RWKV_STAGED:
            for p in range(n_tiles):
                staged[name][b * n_tiles + p] = ops[name][p]
        s_pend[b] = ops["p_end"]
        s_bonus[b] = ops["bonus"]
        s_gate[b] = ops["gate"]

    def staged_ops(b):
        ops = {name: [staged[name][b * n_tiles + p] for p in range(n_tiles)] for name in _RWKV_STAGED}
        ops.update(p_end=s_pend[b], bonus=s_bonus[b], gate=s_gate[b])
        return ops

    @pl.when(pl.program_id(1) == 0)
    def _():
        st_ref[...] = jnp.zeros(st_ref.shape, F32)
        for b in seqs:
            stash(b, _run(prepare(first_src, b, True)))

    scan = lambda ops_list: _rwkv_scan(ops_list, st_ref, ybuf, lnw_ref, lnb_ref, ones_ref, masks)
    y_even, odd = _interleave(scan([staged_ops(b) for b in seqs]), [prepare(odd_src, b, False) for b in seqs])
    for b in seqs:
        o_ref[b, 0:cl, :] = y_even[b].astype(o_ref.dtype)
    y_odd, nxt = _interleave(scan(odd), [prepare(next_src, b, False) for b in seqs])
    for b in seqs:
        o_ref[b, cl:2 * cl, :] = y_odd[b].astype(o_ref.dtype)
        stash(b, nxt[b])


def _rwkv(u2, bsz, seq, mu, w0, w2, a0, a2, g2, k_k, k_a, r_k, ln_w, ln_b):
    cl = RWKV_CHUNK
    nc = seq // cl
    nb = RWKV_SEQS_PER_STEP
    assert nc % 2 == 0 and bsz % nb == 0
    w = W_MIX
    mw = RWKV_MISC
    uw = u2.shape[1]
    n_tiles = w // LANES
    row1 = lambda vec: vec.reshape(1, -1)
    mu_row = jnp.zeros((uw,), F32).at[0:mu.shape[0]].set(mu)
    w2e = jnp.zeros((2 * LANES, w), F32).at[MISC_PW:MISC_PA].set(w2)
    a2e = jnp.zeros((2 * LANES, w), F32).at[MISC_PA:MISC_PG].set(a2)
    g2e = jnp.zeros((mw - LANES, w), F32).at[MISC_PG - LANES:MISC_DT - LANES].set(g2)
    ones_blk = np.kron(np.eye(LANES // HEAD, dtype=np.float32), np.ones((HEAD, HEAD), np.float32))

    def chunk_spec(chunk_of):
        return pl.BlockSpec((nb, cl, uw), lambda bb, j: (bb, chunk_of(j), 0))

    cvec = lambda width: pl.BlockSpec((1, width), lambda bb, j: (0, 0))
    cmat = lambda rows, width: pl.BlockSpec((rows, width), lambda bb, j: (0, 0))
    stage = lambda cols: pltpu.VMEM((nb * n_tiles, LANES, cols), MXU_DTYPE)
    out = pl.pallas_call(
        _rwkv_body,
        grid=(bsz // nb, nc // 2),
        in_specs=([chunk_spec(lambda j: 0), chunk_spec(lambda j: 2 * j + 1),
                   chunk_spec(lambda j: jnp.minimum(2 * j + 2, nc - 1)), cvec(uw)]
                  + [cvec(w), cmat(2 * LANES, w), cvec(w), cmat(2 * LANES, w), cmat(mw - LANES, w),
                     cvec(w), cvec(w), cvec(w), cvec(w), cvec(w), cmat(LANES, LANES)]),
        out_specs=pl.BlockSpec((nb, 2 * cl, w), lambda bb, j: (bb, j, 0)),
        out_shape=jax.ShapeDtypeStruct((bsz, seq, w), MXU_DTYPE),
        scratch_shapes=[pltpu.VMEM((nb, cl + SUBLANES, uw), F32),
                        pltpu.VMEM((nb * n_tiles, LANES, LANES), F32),
                        pltpu.VMEM((nb, cl, w), F32),
                        stage(LANES), stage(LANES), stage(LANES), stage(LANES), stage(LANES),
                        stage(2 * LANES),
                        pltpu.VMEM((nb, 1, w), F32), pltpu.VMEM((nb, cl, w), F32),
                        pltpu.VMEM((nb, cl, w), F32)],
        compiler_params=_cparams(("arbitrary", "arbitrary")),
        name="rwkv7_mixer",
    )(*([u2.reshape(bsz, seq, uw)] * 3), row1(mu_row), row1(w0), _mx(w2e), row1(a0), _mx(a2e), _mx(g2e), row1(k_k), row1(k_a),
      row1(r_k), row1(ln_w), row1(ln_b), jnp.asarray(ones_blk, MXU_DTYPE))
    return out.reshape(bsz * seq, w)


def _attn_body(q_ref, k_ref, v_ref, o_ref):
    d = q_ref.shape[1]
    hd = d // XATTN_HEADS
    scale = hd ** -0.5
    heads = [slice(h * hd, (h + 1) * hd) for h in range(XATTN_HEADS)]
    scores = [_dot_nt(q_ref[:, hs], k_ref[0, :, hs]) * scale for hs in heads]
    probs = []
    for s in scores:
        e = jnp.exp(s - jnp.max(s, axis=-1, keepdims=True))
        probs.append(_mx(e / jnp.sum(e, axis=-1, keepdims=True)))
    for hs, p in zip(heads, probs):
        o_ref[:, hs] = _dot(p, v_ref[0, :, hs]).astype(o_ref.dtype)


def _attention(q, k, v, bsz, seq, ts=1024):
    d = q.shape[1]
    n_mem = k.shape[1]
    ts = min(ts, seq)
    ns = seq // ts
    return pl.pallas_call(
        _attn_body,
        grid=(bsz, ns),
        in_specs=[pl.BlockSpec((ts, d), lambda b, s: (b * ns + s, 0)),
                  pl.BlockSpec((1, n_mem, d), lambda b, s: (b, 0, 0)),
                  pl.BlockSpec((1, n_mem, d), lambda b, s: (b, 0, 0))],
        out_specs=pl.BlockSpec((ts, d), lambda b, s: (b * ns + s, 0)),
        out_shape=jax.ShapeDtypeStruct((bsz * seq, d), MXU_DTYPE),
        compiler_params=_cparams(("arbitrary", "arbitrary")),
        name="mem_attention",
    )(q, k, v)


def _rwkv_side(mat, axis):
    ssd_in = U1_COLS + W_MIX // HEAD
    take = lambda lo, hi: lax.slice_in_dim(mat, lo, hi, axis=axis)
    pad_shape = list(mat.shape)
    pad_shape[axis] = U2_COLS - (mat.shape[axis] - U1_COLS)
    return jnp.concatenate([take(ssd_in, mat.shape[axis]), take(U1_COLS, ssd_in),
                            jnp.zeros(pad_shape, mat.dtype)], axis=axis)


def kernel(x, mem, norm_mix_g, w_in, ssd_conv_w, ssd_conv_b, ssd_dt_bias, ssd_a_log, ssd_d, ssd_norm_g, rwkv_mu, rwkv_w0, rwkv_w2, rwkv_a0, rwkv_a2, rwkv_g2, rwkv_k_k, rwkv_k_a, rwkv_r_k, rwkv_ln_w, rwkv_ln_b, w_out, norm_x_g, norm_mem_g, xattn_wq, xattn_wk, xattn_wv, xattn_wo, norm_ffn_g, ffn_w1, ffn_w2, final_norm_g):
    bsz, seq, d = x.shape
    n_mem = mem.shape[1]
    xr = x.reshape(bsz * seq, d)
    memr = mem.reshape(bsz * n_mem, d)
    n_layers = w_in.shape[0]
    for l in range(n_layers):
        w_in_t = w_in[l].T
        u1 = _matmul_resident(xr, w_in_t, F32, n=U1_COLS, b_is_transposed=True,
                              norm_g=norm_mix_g[l], tm=512, tn=U1_COLS // 2, name="in_proj_ssd")
        u2 = _matmul_resident(xr, _rwkv_side(w_in_t, 0), F32, b_is_transposed=True,
                              norm_g=norm_mix_g[l], tm=512, tn=U2_COLS // 2, name="in_proj_rwkv")
        y_ssd = _ssd(u1, u2, bsz, seq, ssd_conv_w[l], ssd_conv_b[l], ssd_dt_bias[l], ssd_a_log[l],
                     ssd_d[l], ssd_norm_g[l])
        y_rwkv = _rwkv(u2, bsz, seq, rwkv_mu[l], rwkv_w0[l], rwkv_w2[l], rwkv_a0[l], rwkv_a2[l],
                       rwkv_g2[l], rwkv_k_k[l], rwkv_k_a[l], rwkv_r_k[l].reshape(-1),
                       rwkv_ln_w[l], rwkv_ln_b[l])
        wo = _mx(w_out[l])
        xr, h, wo_attn = _matmul_rows([(y_ssd, wo, 0), (y_rwkv, wo, W_MIX)], xr, norm_x_g[l],
                                      MXU_DTYPE, side_cast=xattn_wo[l], name="out_proj")

        q = _matmul_resident(h, xattn_wq[l], MXU_DTYPE, tm=2048, name="q_proj")
        kx = _matmul_resident(memr, xattn_wk[l], MXU_DTYPE, norm_g=norm_mem_g[l], name="k_proj")
        vx = _matmul_resident(memr, xattn_wv[l], MXU_DTYPE, norm_g=norm_mem_g[l], name="v_proj")
        o = _attention(q, kx.reshape(bsz, n_mem, d), vx.reshape(bsz, n_mem, d), bsz, seq)
        xr, h, w2 = _matmul_rows([(o, wo_attn, 0)], xr, norm_ffn_g[l], MXU_DTYPE,
                                 side_cast=ffn_w2[l], name="o_proj")

        hid = _matmul_resident(h, ffn_w1[l], MXU_DTYPE, act="relu2", tm=2048, name="ffn_up")
        down = [(hid, w2, 0)]
        if l + 1 < n_layers:
            xr, _ = _matmul_rows(down, xr, norm_mix_g[l + 1], MXU_DTYPE, name="ffn_down")
        else:
            out = _matmul_rows(down, xr, final_norm_g, x.dtype, emit_x=False, name="ffn_down")
    return out.reshape(bsz, seq, d)
```

```python
import functools

import numpy as np
import jax
import jax.numpy as jnp
from jax import lax
from jax.experimental import pallas as pl
from jax.experimental.pallas import tpu as pltpu

F32 = jnp.float32
MXU_DTYPE = jnp.bfloat16

NORM_EPS = 1e-6
RWKV_LN_EPS = 64e-5

HEAD = 64
SSD_STATE = 128
SSD_GROUPS = 2
SSD_CHUNK = 128
SSD_CONV = 4
RWKV_CHUNK = 64
RWKV_DECAY_RANK = 96
RWKV_AAA_RANK = 96
RWKV_GATE_RANK = 256
XATTN_HEADS = 4

LANES = 128
SUBLANES = 8
VMEM_LIMIT = 56 * 1024 * 1024

W_MIX = 1024
SSD_BC = 2 * SSD_GROUPS * SSD_STATE
U1_COLS = 2 * W_MIX + SSD_BC
U2_MISC = 3 * W_MIX
RWKV_MISC = 512
U2_COLS = U2_MISC + RWKV_MISC
MISC_PW = 0
MISC_PA = MISC_PW + RWKV_DECAY_RANK
MISC_PG = MISC_PA + RWKV_AAA_RANK
MISC_DT = MISC_PG + RWKV_GATE_RANK
SSD_DT_LANE = MISC_DT % LANES


def _mx(a):
    return a.astype(MXU_DTYPE)


def _dot(a, b):
    return jnp.dot(a, b, preferred_element_type=F32)


def _dot_nt(a, b):
    return lax.dot_general(a, b, (((1,), (1,)), ((), ())), preferred_element_type=F32)


def _split(v, parts):
    out = []
    rem = v
    for _ in range(parts):
        p = rem.astype(MXU_DTYPE)
        out.append(p)
        rem = rem - p.astype(F32)
    return out


def _dot_split_rhs(a01, v, parts):
    acc = None
    for p in _split(v, parts):
        t = _dot(a01, p)
        acc = t if acc is None else acc + t
    return acc


def _dot_split_lhs(v, b01, parts):
    acc = None
    for p in _split(v, parts):
        t = _dot(p, b01)
        acc = t if acc is None else acc + t
    return acc


def _sigmoid(x):
    return 1.0 / (1.0 + jnp.exp(-x))


def _softplus(x):
    return jnp.maximum(x, 0.0) + jnp.log(1.0 + jnp.exp(-jnp.abs(x)))


def _cparams(sem):
    return pltpu.CompilerParams(dimension_semantics=sem, vmem_limit_bytes=VMEM_LIMIT)


def _rmsnorm_body(x_ref, g_ref, o_ref):
    x = x_ref[...]
    ms = jnp.mean(x * x, axis=-1, keepdims=True)
    o_ref[...] = (x * lax.rsqrt(ms + NORM_EPS) * g_ref[...]).astype(o_ref.dtype)


def _rmsnorm(x, g, out_dtype, tm=512):
    m, d = x.shape
    tm = min(tm, m)
    return pl.pallas_call(
        _rmsnorm_body,
        grid=(m // tm,),
        in_specs=[pl.BlockSpec((tm, d), lambda i: (i, 0)),
                  pl.BlockSpec((1, d), lambda i: (0, 0))],
        out_specs=pl.BlockSpec((tm, d), lambda i: (i, 0)),
        out_shape=jax.ShapeDtypeStruct((m, d), out_dtype),
        compiler_params=_cparams(("arbitrary",)),
        name="rmsnorm",
    )(x, g.reshape(1, d))


def _mm_resident_body(*refs, act, b_is_transposed, normalize):
    if normalize:
        a_ref, g_ref, b_ref, o_ref, bm_ref = refs
    else:
        a_ref, b_ref, o_ref, bm_ref = refs

    @pl.when(pl.program_id(1) == 0)
    def _():
        b = b_ref[...]
        bm_ref[...] = (b.T if b_is_transposed else b).astype(MXU_DTYPE)

    a = a_ref[...]
    if normalize:
        ms = jnp.mean(a * a, axis=-1, keepdims=True)
        a = (a * lax.rsqrt(ms + NORM_EPS) * g_ref[...]).astype(MXU_DTYPE)
    r = _dot(a, bm_ref[...])
    if act == "relu2":
        r = jnp.square(jnp.maximum(r, 0.0))
    o_ref[...] = r.astype(o_ref.dtype)


def _matmul_resident(a, b, out_dtype, act=None, n=None, b_is_transposed=False, norm_g=None,
                     tm=1024, tn=1024, name="matmul_resident"):
    m, kdim = a.shape
    n = b.shape[0 if b_is_transposed else 1] if n is None else n
    tm, tn = min(tm, m), min(tn, n)
    assert m % tm == 0 and n % tn == 0
    b_spec = (pl.BlockSpec((tn, kdim), lambda j, i: (j, 0)) if b_is_transposed
              else pl.BlockSpec((kdim, tn), lambda j, i: (0, j)))
    in_specs, args = [pl.BlockSpec((tm, kdim), lambda j, i: (i, 0))], [a]
    if norm_g is not None:
        in_specs.append(pl.BlockSpec((1, kdim), lambda j, i: (0, 0)))
        args.append(norm_g.reshape(1, kdim))
    return pl.pallas_call(
        functools.partial(_mm_resident_body, act=act, b_is_transposed=b_is_transposed,
                          normalize=norm_g is not None),
        grid=(n // tn, m // tm),
        in_specs=in_specs + [b_spec],
        out_specs=pl.BlockSpec((tm, tn), lambda j, i: (i, j)),
        out_shape=jax.ShapeDtypeStruct((m, n), out_dtype),
        scratch_shapes=[pltpu.VMEM((kdim, tn), MXU_DTYPE)],
        compiler_params=_cparams(("arbitrary", "arbitrary")),
        name=name,
    )(*args, b)


def _mm_rows_body(*refs, n_pairs, nk, emit_x, side):
    ab = refs[:2 * n_pairs]
    rest = list(refs[2 * n_pairs:])
    res_ref, g_ref = rest.pop(0), rest.pop(0)
    side_in = rest.pop(0) if side else None
    x_ref = rest.pop(0) if emit_x else None
    h_ref = rest.pop(0)
    side_out = rest.pop(0) if side else None
    acc_ref = rest.pop(0) if nk > 1 else None
    if side:
        side_out[...] = side_in[...].astype(side_out.dtype)
    part = None
    for p in range(n_pairs):
        t = _dot(ab[2 * p][...], ab[2 * p + 1][...])
        part = t if part is None else part + t

    def finish(acc):
        x = res_ref[...] + acc
        if emit_x:
            x_ref[...] = x
        ms = jnp.mean(x * x, axis=-1, keepdims=True)
        h_ref[...] = (x * lax.rsqrt(ms + NORM_EPS) * g_ref[...]).astype(h_ref.dtype)

    if nk == 1:
        finish(part)
        return

    k = pl.program_id(1)

    @pl.when(k == 0)
    def _():
        acc_ref[...] = part

    @pl.when((k > 0) & (k < nk - 1))
    def _():
        acc_ref[...] += part

    @pl.when(k == nk - 1)
    def _():
        finish(acc_ref[...] + part)


def _matmul_rows(pairs, res, g, h_dtype, emit_x=True, side_cast=None, tm=512, tk=2048,
                 name="matmul_rows"):
    m, n = res.shape
    tm = min(tm, m)
    kdims = [a.shape[1] for a, _, _ in pairs]
    tk = min(tk, min(kdims))
    nk = kdims[0] // tk
    assert all(kd == nk * tk for kd in kdims) and all(row0 % tk == 0 for _, _, row0 in pairs)
    in_specs, args = [], []
    for a, b, row0 in pairs:
        in_specs += [pl.BlockSpec((tm, tk), lambda i, k: (i, k)),
                     pl.BlockSpec((tk, n), lambda i, k, blk0=row0 // tk: (blk0 + k, 0))]
        args += [a, b]
    row_spec = pl.BlockSpec((tm, n), lambda i, k: (i, 0))
    in_specs += [row_spec, pl.BlockSpec((1, n), lambda i, k: (0, 0))]
    args += [res, g.reshape(1, n)]
    out_specs, out_shape = [row_spec], [jax.ShapeDtypeStruct((m, n), h_dtype)]
    if emit_x:
        out_specs, out_shape = [row_spec] + out_specs, [jax.ShapeDtypeStruct((m, n), F32)] + out_shape
    if side_cast is not None:
        assert nk == 1 and side_cast.shape[0] % (m // tm) == 0
        slab = pl.BlockSpec((side_cast.shape[0] // (m // tm), side_cast.shape[1]), lambda i, k: (i, 0))
        in_specs.append(slab)
        args.append(side_cast)
        out_specs.append(slab)
        out_shape.append(jax.ShapeDtypeStruct(side_cast.shape, MXU_DTYPE))
    outs = pl.pallas_call(
        functools.partial(_mm_rows_body, n_pairs=len(pairs), nk=nk, emit_x=emit_x,
                          side=side_cast is not None),
        grid=(m // tm, nk),
        in_specs=in_specs,
        out_specs=out_specs,
        out_shape=out_shape,
        scratch_shapes=[pltpu.VMEM((tm, n), F32)] if nk > 1 else [],
        compiler_params=_cparams(("arbitrary", "arbitrary")),
        name=name,
    )(*args)
    return tuple(outs) if len(outs) > 1 else outs[0]


def _ssd_body(z_ref, xs_ref, bc_ref, m_ref, cwx_ref, cbx_ref, cwbc_ref, cbbc_ref, dtb_ref,
              alog_ref, dvec_ref, ng_ref, esel_ref, o_ref, xbuf, bcbuf, st_ref):
    q = SSD_CHUNK
    n = SSD_STATE
    gw = st_ref.shape[2]
    c = pl.program_id(1)

    @pl.when(c == 0)
    def _():
        xbuf[0:SUBLANES, :] = jnp.zeros((SUBLANES, xbuf.shape[1]), F32)
        bcbuf[0:SUBLANES, :] = jnp.zeros((SUBLANES, bcbuf.shape[1]), F32)
        st_ref[...] = jnp.zeros(st_ref.shape, F32)

    @pl.when(c > 0)
    def _():
        xbuf[0:SUBLANES, :] = xbuf[q:q + SUBLANES, :]
        bcbuf[0:SUBLANES, :] = bcbuf[q:q + SUBLANES, :]

    xbuf[SUBLANES:SUBLANES + q, :] = xs_ref[...]
    bcbuf[SUBLANES:SUBLANES + q, :] = bc_ref[...]

    def conv_silu(buf, w_ref, b_ref):
        full = buf[0:SUBLANES + q, :]
        acc = None
        for k in range(SSD_CONV):
            back = SSD_CONV - 1 - k
            rows = full if back == 0 else pltpu.roll(full, back, 0)
            t = rows[SUBLANES:, :] * w_ref[k:k + 1, :]
            acc = t if acc is None else acc + t
        acc = acc + b_ref[...]
        return acc * _sigmoid(acc)

    xc = conv_silu(xbuf, cwx_ref, cbx_ref)
    bcc = conv_silu(bcbuf, cwbc_ref, cbbc_ref)
    g_n = SSD_GROUPS * n

    lane = lax.broadcasted_iota(jnp.int32, (1, LANES), 1)
    n_heads = SSD_GROUPS * gw // HEAD
    dmask = (lane >= SSD_DT_LANE) & (lane < SSD_DT_LANE + n_heads)
    dt = jnp.where(dmask, _softplus(m_ref[...] + dtb_ref[...]), 0.0)
    a = jnp.where(dmask, -jnp.exp(alog_ref[...]), 0.0)
    da = dt * a
    row = lax.broadcasted_iota(jnp.int32, (q, q), 0)
    col = lax.broadcasted_iota(jnp.int32, (q, q), 1)
    tri = row >= col
    tril = jnp.where(tri, 1.0, 0.0).astype(MXU_DTYPE)
    cs = _dot_split_rhs(tril, da, 3)
    ecs = jnp.exp(cs)
    dte = jnp.exp(cs[q - 1:q, :] - cs)
    esel = esel_ref[...]
    dt_e = _dot_split_lhs(dt, esel, 2)
    ecs_e = _dot_split_lhs(ecs, esel, 2)
    dte_e = _dot_split_lhs(dte, esel, 2)
    cs_t = cs.T

    xdt = xc * dt_e
    xdt_m = _mx(xdt)
    xd_m = _mx(xdt * dte_e)
    lane_lo = lane < HEAD

    y_cols = []
    for g in range(SSD_GROUPS):
        bg = bcc[:, g * n:(g + 1) * n]
        cg = _mx(bcc[:, g_n + g * n:g_n + (g + 1) * n])
        cb = _dot_nt(cg, _mx(bg))
        gs = slice(g * gw, (g + 1) * gw)
        st = st_ref[g]
        y_off = _dot(cg, _mx(st)) * ecs_e[:, gs]
        st_ref[g] = st * ecs_e[q - 1:q, gs] + _dot(_mx(bg.T), xd_m[:, gs])
        for pr in range(gw // LANES):
            h0 = (g * gw + pr * LANES) // HEAD
            ps = slice(g * gw + pr * LANES, g * gw + (pr + 1) * LANES)
            res = []
            for hh in (h0, h0 + 1):
                li = SSD_DT_LANE + hh
                seg = cs[:, li:li + 1] - cs_t[li:li + 1, :]
                lm = jnp.where(tri, jnp.exp(jnp.where(tri, seg, 0.0)), 0.0)
                res.append(_dot(_mx(cb * lm), xdt_m[:, ps]))
            y_diag = jnp.where(lane_lo, res[0], res[1])
            y_cols.append(y_diag + y_off[:, pr * LANES:(pr + 1) * LANES])
    y = jnp.concatenate(y_cols, axis=1) + xc * dvec_ref[...]
    zz = z_ref[...]
    y = y * (zz * _sigmoid(zz))
    outs = []
    for g in range(SSD_GROUPS):
        yg = y[:, g * gw:(g + 1) * gw]
        ms = jnp.mean(yg * yg, axis=-1, keepdims=True)
        outs.append(yg * lax.rsqrt(ms + NORM_EPS))
    y = jnp.concatenate(outs, axis=1) * ng_ref[...]
    o_ref[...] = y.astype(o_ref.dtype)


def _ssd(u1, u2, bsz, seq, conv_w, conv_b, dt_bias, a_log, d_skip, norm_g):
    q = SSD_CHUNK
    nc = seq // q
    w = W_MIX
    gw = w // SSD_GROUPS
    heads = w // HEAD
    bcw = SSD_BC
    cw = conv_w[:, 0, :]
    pad = lambda v: jnp.zeros((1, LANES), F32).at[0, SSD_DT_LANE:SSD_DT_LANE + heads].set(v)
    esel = np.zeros((LANES, w), np.float32)
    for h in range(heads):
        esel[SSD_DT_LANE + h, h * HEAD:(h + 1) * HEAD] = 1.0
    rowblk = lambda cb: (lambda b, c: (b * nc + c, cb))
    const = lambda b, c: (0, 0)
    return pl.pallas_call(
        _ssd_body,
        grid=(bsz, nc),
        in_specs=[pl.BlockSpec((q, w), rowblk(0)),
                  pl.BlockSpec((q, w), rowblk(1)),
                  pl.BlockSpec((q, bcw), rowblk(2 * w // bcw)),
                  pl.BlockSpec((q, LANES), rowblk((U2_MISC + MISC_DT) // LANES)),
                  pl.BlockSpec((SSD_CONV, w), const), pl.BlockSpec((1, w), const),
                  pl.BlockSpec((SSD_CONV, bcw), const), pl.BlockSpec((1, bcw), const),
                  pl.BlockSpec((1, LANES), const), pl.BlockSpec((1, LANES), const),
                  pl.BlockSpec((1, w), const), pl.BlockSpec((1, w), const),
                  pl.BlockSpec((LANES, w), const)],
        out_specs=pl.BlockSpec((q, w), lambda b, c: (b * nc + c, 0)),
        out_shape=jax.ShapeDtypeStruct((bsz * seq, w), MXU_DTYPE),
        scratch_shapes=[pltpu.VMEM((q + SUBLANES, w), F32),
                        pltpu.VMEM((q + SUBLANES, bcw), F32),
                        pltpu.VMEM((SSD_GROUPS, SSD_STATE, gw), F32)],
        compiler_params=_cparams(("arbitrary", "arbitrary")),
        name="ssd_mixer",
    )(u1, u1, u1, u2, cw[:, :w], conv_b[:w].reshape(1, w), cw[:, w:], conv_b[w:].reshape(1, bcw),
      pad(dt_bias), pad(a_log), jnp.repeat(d_skip, HEAD).reshape(1, w), norm_g.reshape(1, w),
      jnp.asarray(esel, MXU_DTYPE))


def _unit_lower_inverse(l_mats, blk_mask, eye_f):
    mm = lambda xs, ys: [_dot(x, y) for x, y in zip(xs, ys)]
    mx = lambda xs: [_mx(x) for x in xs]
    add = lambda xs, ys: [x + y for x, y in zip(xs, ys)]
    d = [jnp.where(blk_mask, l, 0.0) for l in l_mats]
    o = mx([l - x for l, x in zip(l_mats, d)])
    d1 = mx(d)
    d2 = mx(mm(d1, d1))
    t = [eye_f + x for x in d]
    yield
    d4 = mx(mm(d2, d2))
    t = add(t, mm(d2, mx(t)))
    yield
    d8 = mx(mm(d4, d4))
    t = add(t, mm(d4, mx(t)))
    yield
    t_d = add(t, mm(d8, mx(t)))
    t_dm = mx(t_d)
    yield
    m1 = mm(t_dm, o)
    m1m = mx(m1)
    yield
    m2 = mm(m1m, m1m)
    yield
    m3 = mm(m1m, mx(m2))
    w = mx([x + y + z for x, y, z in zip(m1, m2, m3)])
    yield
    return add(t_d, mm(w, t_dm))


def _rwkv_masks():
    lane = lax.broadcasted_iota(jnp.int32, (1, LANES), 1)
    row = lax.broadcasted_iota(jnp.int32, (LANES, LANES), 0)
    col = lax.broadcasted_iota(jnp.int32, (LANES, LANES), 1)
    same = (row >= HEAD) == (col >= HEAD)
    rs = row & (HEAD - 1)
    cs = col & (HEAD - 1)
    eye = row == col
    return dict(lane_lo=lane < HEAD, strict=same & (rs > cs), incl=same & (rs >= cs),
                blk=(row >> 4) == (col >> 4), eye=eye, eye_f=jnp.where(eye, 1.0, 0.0))


def _rwkv_head_sum(x, ones_blk):
    cols = [_dot_split_lhs(x[:, t * LANES:(t + 1) * LANES], ones_blk, 2)
            for t in range(x.shape[1] // LANES)]
    return jnp.concatenate(cols, axis=1)


def _rwkv_prepare(src, buf, mu_ref, prm, masks, first):
    cl = RWKV_CHUNK
    w0_ref, w2_ref, a0_ref, a2_ref, g2_ref, kk_ref, ka_ref, rk_ref, ones_ref = prm
    if first:
        buf[0:SUBLANES, :] = jnp.zeros((SUBLANES, buf.shape[1]), F32)
    else:
        buf[0:SUBLANES, :] = buf[cl:cl + SUBLANES, :]
    cur = src[...]
    buf[SUBLANES:SUBLANES + cl, :] = cur
    prev = pltpu.roll(buf[0:SUBLANES + cl, :], 1, 0)[SUBLANES:, :]
    row = cur + (prev - cur) * mu_ref[...]
    w = W_MIX
    r, k, v = (row[:, i * w:(i + 1) * w] for i in range(3))
    misc = row[:, U2_MISC:]
    pw_pa = misc[:, 0:2 * LANES]
    ones_blk = ones_ref[...]
    yield
    w_log = -_softplus(-(w0_ref[...] + _dot(_mx(jnp.tanh(pw_pa)), w2_ref[...]))) - 0.5
    lw = -jnp.exp(w_log)
    iclr = _sigmoid(a0_ref[...] + _dot(_mx(pw_pa), a2_ref[...]))
    gate = _dot(_mx(_sigmoid(misc[:, LANES:])), g2_ref[...])
    yield

    row_c = lax.broadcasted_iota(jnp.int32, (cl, cl), 0)
    col_c = lax.broadcasted_iota(jnp.int32, (cl, cl), 1)
    tril_c = jnp.where(row_c >= col_c, 1.0, 0.0).astype(MXU_DTYPE)
    lane_lo = masks["lane_lo"]
    stack2 = lambda x: jnp.concatenate([jnp.where(lane_lo, x, 0.0), jnp.where(lane_lo, 0.0, x)], axis=0)
    out = dict(a2=[], r2=[], b2=[], k2=[], v2=[], bkp=[])
    p_ends, bonuses = [], []
    for p in range(r.shape[1] // LANES):
        ps = slice(p * LANES, (p + 1) * LANES)
        r_p, k_p, v_p, lw_p, iclr_p = r[:, ps], k[:, ps], v[:, ps], lw[:, ps], iclr[:, ps]
        kk = k_p * kk_ref[:, ps]
        kk = kk / jnp.maximum(jnp.sqrt(_dot_split_lhs(kk * kk, ones_blk, 2)), 1e-12)
        k_p = k_p * (1.0 + (iclr_p - 1.0) * ka_ref[:, ps])
        bonuses.append(_dot_split_lhs(r_p * k_p * rk_ref[:, ps], ones_blk, 2) * v_p)
        cum = _dot_split_rhs(tril_c, lw_p, 3)
        p_in = jnp.exp(cum)
        p_inv = jnp.exp(-cum)
        a_t = -(kk * jnp.exp(cum - lw_p))
        b_t = kk * iclr_p * p_inv
        k_t = k_p * p_inv
        p_end = p_in[cl - 1:cl, :]
        out["a2"].append(_mx(stack2(a_t)))
        out["r2"].append(_mx(stack2(r_p * p_in)))
        out["b2"].append(_mx(stack2(b_t)))
        out["k2"].append(_mx(stack2(k_t)))
        out["v2"].append(_mx(stack2(v_p)))
        out["bkp"].append(_mx(jnp.concatenate([stack2(b_t * p_end), stack2(k_t * p_end)], axis=0).T))
        p_ends.append(p_end)
        yield
    out.update(p_end=jnp.concatenate(p_ends, axis=1), bonus=jnp.concatenate(bonuses, axis=1), gate=gate)
    return out


def _rwkv_scan(ops_list, st_ref, ybuf, lnw_ref, lnb_ref, ones_ref, masks):
    cl = RWKV_CHUNK
    gather = lambda name: [x for ops in ops_list for x in ops[name]]
    a2, r2, b2, k2, v2, bkp_t = (gather(name) for name in _RWKV_STAGED)
    n_tiles = len(ops_list[0]["a2"])
    tiles = range(len(a2))
    cat0 = lambda *xs: jnp.concatenate(xs, axis=0)
    cat1 = lambda *xs: jnp.concatenate(xs, axis=1)
    gram = [_dot_nt(cat0(a2[t], r2[t]), cat0(b2[t], k2[t])) for t in tiles]
    yield
    a_ab = [jnp.where(masks["strict"], g[0:LANES, 0:LANES], 0.0) for g in gram]
    a_ak = [_mx(jnp.where(masks["strict"], g[0:LANES, LANES:], 0.0)) for g in gram]
    a_rb = [_mx(jnp.where(masks["incl"], g[LANES:, 0:LANES], 0.0)) for g in gram]
    a_rk = [_mx(jnp.where(masks["incl"], g[LANES:, LANES:], 0.0)) for g in gram]
    t_inv = yield from _unit_lower_inverse(a_ab, masks["blk"], masks["eye_f"])
    st = [st_ref[t] for t in tiles]
    st_m = [_mx(s) for s in st]
    y0 = [_dot(cat1(a2[t], a_ak[t]), cat0(st_m[t], v2[t])) for t in tiles]
    yield
    sa = [_mx(_dot(_mx(t_inv[t]), _mx(y0[t]))) for t in tiles]
    yield
    o2 = [_dot(cat1(r2[t], a_rb[t], a_rk[t]), cat0(st_m[t], sa[t], v2[t])) for t in tiles]
    yield
    upd = [_dot(bkp_t[t], cat0(sa[t], v2[t])) for t in tiles]
    for t in tiles:
        b, p = divmod(t, n_tiles)
        ps = slice(p * LANES, (p + 1) * LANES)
        ybuf[b, :, ps] = o2[t][0:cl, :] + o2[t][cl:, :]
        p_end = ops_list[b]["p_end"][:, ps]
        p_col = jnp.sum(jnp.where(masks["eye"], p_end, 0.0), axis=1, keepdims=True)
        st_ref[t] = st[t] * p_col + upd[t]
    yield
    ones_blk = ones_ref[...]
    inv_n = 1.0 / HEAD
    outs = []
    for b, ops in enumerate(ops_list):
        y = ybuf[b]
        mean = _rwkv_head_sum(y, ones_blk) * inv_n
        d = y - mean
        var = _rwkv_head_sum(d * d, ones_blk) * inv_n
        y = d * lax.rsqrt(var + RWKV_LN_EPS) * lnw_ref[...] + lnb_ref[...]
        outs.append((y + ops["bonus"]) * ops["gate"])
    return outs


def _run(gen):
    while True:
        try:
            next(gen)
        except StopIteration as stop:
            return stop.value


def _interleave(scan_gen, prep_gens):
    gens = [scan_gen] + list(prep_gens)
    results, done = [None] * len(gens), [False] * len(gens)
    while not all(done):
        for i, gen in enumerate(gens):
            if not done[i]:
                try:
                    next(gen)
                except StopIteration as stop:
                    results[i], done[i] = stop.value, True
    return results[0], results[1:]


_RWKV_STAGED = ("a2", "r2", "b2", "k2", "v2", "bkp")
RWKV_SEQS_PER_STEP = 2


def _rwkv_body(*refs):
    cl = RWKV_CHUNK
    first_src, odd_src, next_src, mu_ref = refs[0:4]
    prm = refs[4:12] + (refs[14],)
    lnw_ref, lnb_ref, ones_ref = refs[12], refs[13], refs[14]
    o_ref = refs[15]
    buf = refs[16]
    st_ref, ybuf = refs[17], refs[18]
    staged = dict(zip(_RWKV_STAGED, refs[19:25]))
    s_pend, s_bonus, s_gate = refs[25], refs[26], refs[27]
    masks = _rwkv_masks()
    seqs = range(o_ref.shape[0])
    n_tiles = st_ref.shape[0] // len(seqs)

    def prepare(src, b, first):
        return _rwkv_prepare(src.at[b], buf.at[b], mu_ref, prm, masks, first)

    def stash(b, ops):
        for name in _RWKV_STAGED:
            for p in range(n_tiles):
                staged[name][b * n_tiles + p] = ops[name][p]
        s_pend[b] = ops["p_end"]
        s_bonus[b] = ops["bonus"]
        s_gate[b] = ops["gate"]

    def staged_ops(b):
        ops = {name: [staged[name][b * n_tiles + p] for p in range(n_tiles)] for name in _RWKV_STAGED}
        ops.update(p_end=s_pend[b], bonus=s_bonus[b], gate=s_gate[b])
        return ops

    @pl.when(pl.program_id(1) == 0)
    def _():
        st_ref[...] = jnp.zeros(st_ref.shape, F32)
        for b in seqs:
            stash(b, _run(prepare(first_src, b, True)))

    scan = lambda ops_list: _rwkv_scan(ops_list, st_ref, ybuf, lnw_ref, lnb_ref, ones_ref, masks)
    y_even, odd = _interleave(scan([staged_ops(b) for b in seqs]), [prepare(odd_src, b, False) for b in seqs])
    for b in seqs:
        o_ref[b, 0:cl, :] = y_even[b].astype(o_ref.dtype)
    y_odd, nxt = _interleave(scan(odd), [prepare(next_src, b, False) for b in seqs])
    for b in seqs:
        o_ref[b, cl:2 * cl, :] = y_odd[b].astype(o_ref.dtype)
        stash(b, nxt[b])


def _rwkv(u2, bsz, seq, mu, w0, w2, a0, a2, g2, k_k, k_a, r_k, ln_w, ln_b):
    cl = RWKV_CHUNK
    nc = seq // cl
    nb = RWKV_SEQS_PER_STEP
    assert nc % 2 == 0 and bsz % nb == 0
    w = W_MIX
    mw = RWKV_MISC
    uw = u2.shape[1]
    n_tiles = w // LANES
    row1 = lambda vec: vec.reshape(1, -1)
    mu_row = jnp.zeros((uw,), F32).at[0:mu.shape[0]].set(mu)
    w2e = jnp.zeros((2 * LANES, w), F32).at[MISC_PW:MISC_PA].set(w2)
    a2e = jnp.zeros((2 * LANES, w), F32).at[MISC_PA:MISC_PG].set(a2)
    g2e = jnp.zeros((mw - LANES, w), F32).at[MISC_PG - LANES:MISC_DT - LANES].set(g2)
    ones_blk = np.kron(np.eye(LANES // HEAD, dtype=np.float32), np.ones((HEAD, HEAD), np.float32))

    def chunk_spec(chunk_of):
        return pl.BlockSpec((nb, cl, uw), lambda bb, j: (bb, chunk_of(j), 0))

    cvec = lambda width: pl.BlockSpec((1, width), lambda bb, j: (0, 0))
    cmat = lambda rows, width: pl.BlockSpec((rows, width), lambda bb, j: (0, 0))
    stage = lambda cols: pltpu.VMEM((nb * n_tiles, LANES, cols), MXU_DTYPE)
    out = pl.pallas_call(
        _rwkv_body,
        grid=(bsz // nb, nc // 2),
        in_specs=([chunk_spec(lambda j: 0), chunk_spec(lambda j: 2 * j + 1),
                   chunk_spec(lambda j: jnp.minimum(2 * j + 2, nc - 1)), cvec(uw)]
                  + [cvec(w), cmat(2 * LANES, w), cvec(w), cmat(2 * LANES, w), cmat(mw - LANES, w),
                     cvec(w), cvec(w), cvec(w), cvec(w), cvec(w), cmat(LANES, LANES)]),
        out_specs=pl.BlockSpec((nb, 2 * cl, w), lambda bb, j: (bb, j, 0)),
        out_shape=jax.ShapeDtypeStruct((bsz, seq, w), MXU_DTYPE),
        scratch_shapes=[pltpu.VMEM((nb, cl + SUBLANES, uw), F32),
                        pltpu.VMEM((nb * n_tiles, LANES, LANES), F32),
                        pltpu.VMEM((nb, cl, w), F32),
                        stage(LANES), stage(LANES), stage(LANES), stage(LANES), stage(LANES),
                        stage(2 * LANES),
                        pltpu.VMEM((nb, 1, w), F32), pltpu.VMEM((nb, cl, w), F32),
                        pltpu.VMEM((nb, cl, w), F32)],
        compiler_params=_cparams(("arbitrary", "arbitrary")),
        name="rwkv7_mixer",
    )(*([u2.reshape(bsz, seq, uw)] * 3), row1(mu_row), row1(w0), _mx(w2e), row1(a0), _mx(a2e), _mx(g2e), row1(k_k), row1(k_a),
      row1(r_k), row1(ln_w), row1(ln_b), jnp.asarray(ones_blk, MXU_DTYPE))
    return out.reshape(bsz * seq, w)


def _attn_body(q_ref, k_ref, v_ref, o_ref):
    d = q_ref.shape[1]
    hd = d // XATTN_HEADS
    scale = hd ** -0.5
    heads = [slice(h * hd, (h + 1) * hd) for h in range(XATTN_HEADS)]
    scores = [_dot_nt(q_ref[:, hs], k_ref[0, :, hs]) * scale for hs in heads]
    probs = []
    for s in scores:
        e = jnp.exp(s - jnp.max(s, axis=-1, keepdims=True))
        probs.append(_mx(e / jnp.sum(e, axis=-1, keepdims=True)))
    for hs, p in zip(heads, probs):
        o_ref[:, hs] = _dot(p, v_ref[0, :, hs]).astype(o_ref.dtype)


def _attention(q, k, v, bsz, seq, ts=2048):
    d = q.shape[1]
    n_mem = k.shape[1]
    ts = min(ts, seq)
    ns = seq // ts
    return pl.pallas_call(
        _attn_body,
        grid=(bsz, ns),
        in_specs=[pl.BlockSpec((ts, d), lambda b, s: (b * ns + s, 0)),
                  pl.BlockSpec((1, n_mem, d), lambda b, s: (b, 0, 0)),
                  pl.BlockSpec((1, n_mem, d), lambda b, s: (b, 0, 0))],
        out_specs=pl.BlockSpec((ts, d), lambda b, s: (b * ns + s, 0)),
        out_shape=jax.ShapeDtypeStruct((bsz * seq, d), MXU_DTYPE),
        compiler_params=_cparams(("arbitrary", "arbitrary")),
        name="mem_attention",
    )(q, k, v)


def _rwkv_side(mat, axis):
    ssd_in = U1_COLS + W_MIX // HEAD
    take = lambda lo, hi: lax.slice_in_dim(mat, lo, hi, axis=axis)
    pad_shape = list(mat.shape)
    pad_shape[axis] = U2_COLS - (mat.shape[axis] - U1_COLS)
    return jnp.concatenate([take(ssd_in, mat.shape[axis]), take(U1_COLS, ssd_in),
                            jnp.zeros(pad_shape, mat.dtype)], axis=axis)


def kernel(x, mem, norm_mix_g, w_in, ssd_conv_w, ssd_conv_b, ssd_dt_bias, ssd_a_log, ssd_d, ssd_norm_g, rwkv_mu, rwkv_w0, rwkv_w2, rwkv_a0, rwkv_a2, rwkv_g2, rwkv_k_k, rwkv_k_a, rwkv_r_k, rwkv_ln_w, rwkv_ln_b, w_out, norm_x_g, norm_mem_g, xattn_wq, xattn_wk, xattn_wv, xattn_wo, norm_ffn_g, ffn_w1, ffn_w2, final_norm_g):
    bsz, seq, d = x.shape
    n_mem = mem.shape[1]
    xr = x.reshape(bsz * seq, d)
    memr = mem.reshape(bsz * n_mem, d)
    n_layers = w_in.shape[0]
    for l in range(n_layers):
        w_in_t = w_in[l].T
        u1 = _matmul_resident(xr, w_in_t, F32, n=U1_COLS, b_is_transposed=True,
                              norm_g=norm_mix_g[l], tm=512, tn=U1_COLS // 2, name="in_proj_ssd")
        u2 = _matmul_resident(xr, _rwkv_side(w_in_t, 0), F32, b_is_transposed=True,
                              norm_g=norm_mix_g[l], tm=512, tn=U2_COLS // 2, name="in_proj_rwkv")
        y_ssd = _ssd(u1, u2, bsz, seq, ssd_conv_w[l], ssd_conv_b[l], ssd_dt_bias[l], ssd_a_log[l],
                     ssd_d[l], ssd_norm_g[l])
        y_rwkv = _rwkv(u2, bsz, seq, rwkv_mu[l], rwkv_w0[l], rwkv_w2[l], rwkv_a0[l], rwkv_a2[l],
                       rwkv_g2[l], rwkv_k_k[l], rwkv_k_a[l], rwkv_r_k[l].reshape(-1),
                       rwkv_ln_w[l], rwkv_ln_b[l])
        wo = _mx(w_out[l])
        xr, h, wo_attn = _matmul_rows([(y_ssd, wo, 0), (y_rwkv, wo, W_MIX)], xr, norm_x_g[l],
                                      MXU_DTYPE, side_cast=xattn_wo[l], name="out_proj")

        m = _rmsnorm(memr, norm_mem_g[l], MXU_DTYPE)
        q = _matmul_resident(h, xattn_wq[l], MXU_DTYPE, tm=2048, name="q_proj")
        kx = _matmul_resident(m, xattn_wk[l], MXU_DTYPE, name="k_proj")
        vx = _matmul_resident(m, xattn_wv[l], MXU_DTYPE, name="v_proj")
        o = _attention(q, kx.reshape(bsz, n_mem, d), vx.reshape(bsz, n_mem, d), bsz, seq)
        xr, h, w2 = _matmul_rows([(o, wo_attn, 0)], xr, norm_ffn_g[l], MXU_DTYPE,
                                 side_cast=ffn_w2[l], name="o_proj")

        hid = _matmul_resident(h, ffn_w1[l], MXU_DTYPE, act="relu2", tm=2048, name="ffn_up")
        down = [(hid, w2, 0)]
        if l + 1 < n_layers:
            xr, _ = _matmul_rows(down, xr, norm_mix_g[l + 1], MXU_DTYPE, name="ffn_down")
        else:
            out = _matmul_rows(down, xr, final_norm_g, x.dtype, emit_x=False, name="ffn_down")
    return out.reshape(bsz, seq, d)
```

```python
import functools

import numpy as np
import jax
import jax.numpy as jnp
from jax import lax
from jax.experimental import pallas as pl
from jax.experimental.pallas import tpu as pltpu

F32 = jnp.float32
MXU_DTYPE = jnp.bfloat16

NORM_EPS = 1e-6
RWKV_LN_EPS = 64e-5

HEAD = 64
SSD_STATE = 128
SSD_GROUPS = 2
SSD_CHUNK = 128
SSD_CONV = 4
RWKV_CHUNK = 64
RWKV_DECAY_RANK = 96
RWKV_AAA_RANK = 96
RWKV_GATE_RANK = 256
XATTN_HEADS = 4

LANES = 128
SUBLANES = 8
VMEM_LIMIT = 56 * 1024 * 1024

W_MIX = 1024
SSD_BC = 2 * SSD_GROUPS * SSD_STATE
U1_COLS = 2 * W_MIX + SSD_BC
U2_MISC = 3 * W_MIX
RWKV_MISC = 512
U2_COLS = U2_MISC + RWKV_MISC
MISC_PW = 0
MISC_PA = MISC_PW + RWKV_DECAY_RANK
MISC_PG = MISC_PA + RWKV_AAA_RANK
MISC_DT = MISC_PG + RWKV_GATE_RANK
SSD_DT_LANE = MISC_DT % LANES


def _mx(a):
    return a.astype(MXU_DTYPE)


def _dot(a, b):
    return jnp.dot(a, b, preferred_element_type=F32)


def _dot_nt(a, b):
    return lax.dot_general(a, b, (((1,), (1,)), ((), ())), preferred_element_type=F32)


def _split(v, parts):
    out = []
    rem = v
    for _ in range(parts):
        p = rem.astype(MXU_DTYPE)
        out.append(p)
        rem = rem - p.astype(F32)
    return out


def _dot_split_rhs(a01, v, parts):
    acc = None
    for p in _split(v, parts):
        t = _dot(a01, p)
        acc = t if acc is None else acc + t
    return acc


def _dot_split_lhs(v, b01, parts):
    acc = None
    for p in _split(v, parts):
        t = _dot(p, b01)
        acc = t if acc is None else acc + t
    return acc


def _sigmoid(x):
    return 1.0 / (1.0 + jnp.exp(-x))


def _softplus(x):
    return jnp.maximum(x, 0.0) + jnp.log(1.0 + jnp.exp(-jnp.abs(x)))


def _cparams(sem):
    return pltpu.CompilerParams(dimension_semantics=sem, vmem_limit_bytes=VMEM_LIMIT)


def _rmsnorm_body(x_ref, g_ref, o_ref):
    x = x_ref[...]
    ms = jnp.mean(x * x, axis=-1, keepdims=True)
    o_ref[...] = (x * lax.rsqrt(ms + NORM_EPS) * g_ref[...]).astype(o_ref.dtype)


def _rmsnorm(x, g, out_dtype, tm=512):
    m, d = x.shape
    tm = min(tm, m)
    return pl.pallas_call(
        _rmsnorm_body,
        grid=(m // tm,),
        in_specs=[pl.BlockSpec((tm, d), lambda i: (i, 0)),
                  pl.BlockSpec((1, d), lambda i: (0, 0))],
        out_specs=pl.BlockSpec((tm, d), lambda i: (i, 0)),
        out_shape=jax.ShapeDtypeStruct((m, d), out_dtype),
        compiler_params=_cparams(("arbitrary",)),
        name="rmsnorm",
    )(x, g.reshape(1, d))


def _mm_resident_body(*refs, act, b_is_transposed, normalize):
    if normalize:
        a_ref, g_ref, b_ref, o_ref, bm_ref = refs
    else:
        a_ref, b_ref, o_ref, bm_ref = refs

    @pl.when(pl.program_id(1) == 0)
    def _():
        b = b_ref[...]
        bm_ref[...] = (b.T if b_is_transposed else b).astype(MXU_DTYPE)

    a = a_ref[...]
    if normalize:
        ms = jnp.mean(a * a, axis=-1, keepdims=True)
        a = (a * lax.rsqrt(ms + NORM_EPS) * g_ref[...]).astype(MXU_DTYPE)
    r = _dot(a, bm_ref[...])
    if act == "relu2":
        r = jnp.square(jnp.maximum(r, 0.0))
    o_ref[...] = r.astype(o_ref.dtype)


def _matmul_resident(a, b, out_dtype, act=None, n=None, b_is_transposed=False, norm_g=None,
                     tm=1024, tn=1024, name="matmul_resident"):
    m, kdim = a.shape
    n = b.shape[0 if b_is_transposed else 1] if n is None else n
    tm, tn = min(tm, m), min(tn, n)
    assert m % tm == 0 and n % tn == 0
    b_spec = (pl.BlockSpec((tn, kdim), lambda j, i: (j, 0)) if b_is_transposed
              else pl.BlockSpec((kdim, tn), lambda j, i: (0, j)))
    in_specs, args = [pl.BlockSpec((tm, kdim), lambda j, i: (i, 0))], [a]
    if norm_g is not None:
        in_specs.append(pl.BlockSpec((1, kdim), lambda j, i: (0, 0)))
        args.append(norm_g.reshape(1, kdim))
    return pl.pallas_call(
        functools.partial(_mm_resident_body, act=act, b_is_transposed=b_is_transposed,
                          normalize=norm_g is not None),
        grid=(n // tn, m // tm),
        in_specs=in_specs + [b_spec],
        out_specs=pl.BlockSpec((tm, tn), lambda j, i: (i, j)),
        out_shape=jax.ShapeDtypeStruct((m, n), out_dtype),
        scratch_shapes=[pltpu.VMEM((kdim, tn), MXU_DTYPE)],
        compiler_params=_cparams(("arbitrary", "arbitrary")),
        name=name,
    )(*args, b)


def _mm_rows_body(*refs, n_pairs, nk, emit_x, side):
    ab = refs[:2 * n_pairs]
    rest = list(refs[2 * n_pairs:])
    res_ref, g_ref = rest.pop(0), rest.pop(0)
    side_in = rest.pop(0) if side else None
    x_ref = rest.pop(0) if emit_x else None
    h_ref = rest.pop(0)
    side_out = rest.pop(0) if side else None
    acc_ref = rest.pop(0) if nk > 1 else None
    if side:
        side_out[...] = side_in[...].astype(side_out.dtype)
    part = None
    for p in range(n_pairs):
        t = _dot(ab[2 * p][...], ab[2 * p + 1][...])
        part = t if part is None else part + t

    def finish(acc):
        x = res_ref[...] + acc
        if emit_x:
            x_ref[...] = x
        ms = jnp.mean(x * x, axis=-1, keepdims=True)
        h_ref[...] = (x * lax.rsqrt(ms + NORM_EPS) * g_ref[...]).astype(h_ref.dtype)

    if nk == 1:
        finish(part)
        return

    k = pl.program_id(1)

    @pl.when(k == 0)
    def _():
        acc_ref[...] = part

    @pl.when((k > 0) & (k < nk - 1))
    def _():
        acc_ref[...] += part

    @pl.when(k == nk - 1)
    def _():
        finish(acc_ref[...] + part)


def _matmul_rows(pairs, res, g, h_dtype, emit_x=True, side_cast=None, tm=512, tk=2048,
                 name="matmul_rows"):
    m, n = res.shape
    tm = min(tm, m)
    kdims = [a.shape[1] for a, _, _ in pairs]
    tk = min(tk, min(kdims))
    nk = kdims[0] // tk
    assert all(kd == nk * tk for kd in kdims) and all(row0 % tk == 0 for _, _, row0 in pairs)
    in_specs, args = [], []
    for a, b, row0 in pairs:
        in_specs += [pl.BlockSpec((tm, tk), lambda i, k: (i, k)),
                     pl.BlockSpec((tk, n), lambda i, k, blk0=row0 // tk: (blk0 + k, 0))]
        args += [a, b]
    row_spec = pl.BlockSpec((tm, n), lambda i, k: (i, 0))
    in_specs += [row_spec, pl.BlockSpec((1, n), lambda i, k: (0, 0))]
    args += [res, g.reshape(1, n)]
    out_specs, out_shape = [row_spec], [jax.ShapeDtypeStruct((m, n), h_dtype)]
    if emit_x:
        out_specs, out_shape = [row_spec] + out_specs, [jax.ShapeDtypeStruct((m, n), F32)] + out_shape
    if side_cast is not None:
        assert nk == 1 and side_cast.shape[0] % (m // tm) == 0
        slab = pl.BlockSpec((side_cast.shape[0] // (m // tm), side_cast.shape[1]), lambda i, k: (i, 0))
        in_specs.append(slab)
        args.append(side_cast)
        out_specs.append(slab)
        out_shape.append(jax.ShapeDtypeStruct(side_cast.shape, MXU_DTYPE))
    outs = pl.pallas_call(
        functools.partial(_mm_rows_body, n_pairs=len(pairs), nk=nk, emit_x=emit_x,
                          side=side_cast is not None),
        grid=(m // tm, nk),
        in_specs=in_specs,
        out_specs=out_specs,
        out_shape=out_shape,
        scratch_shapes=[pltpu.VMEM((tm, n), F32)] if nk > 1 else [],
        compiler_params=_cparams(("arbitrary", "arbitrary")),
        name=name,
    )(*args)
    return tuple(outs) if len(outs) > 1 else outs[0]


def _ssd_body(z_ref, xs_ref, bc_ref, m_ref, cwx_ref, cbx_ref, cwbc_ref, cbbc_ref, dtb_ref,
              alog_ref, dvec_ref, ng_ref, esel_ref, o_ref, xbuf, bcbuf, st_ref):
    q = SSD_CHUNK
    n = SSD_STATE
    gw = st_ref.shape[2]
    c = pl.program_id(1)

    @pl.when(c == 0)
    def _():
        xbuf[0:SUBLANES, :] = jnp.zeros((SUBLANES, xbuf.shape[1]), F32)
        bcbuf[0:SUBLANES, :] = jnp.zeros((SUBLANES, bcbuf.shape[1]), F32)
        st_ref[...] = jnp.zeros(st_ref.shape, F32)

    @pl.when(c > 0)
    def _():
        xbuf[0:SUBLANES, :] = xbuf[q:q + SUBLANES, :]
        bcbuf[0:SUBLANES, :] = bcbuf[q:q + SUBLANES, :]

    xbuf[SUBLANES:SUBLANES + q, :] = xs_ref[...]
    bcbuf[SUBLANES:SUBLANES + q, :] = bc_ref[...]

    def conv_silu(buf, w_ref, b_ref):
        full = buf[0:SUBLANES + q, :]
        acc = None
        for k in range(SSD_CONV):
            back = SSD_CONV - 1 - k
            rows = full if back == 0 else pltpu.roll(full, back, 0)
            t = rows[SUBLANES:, :] * w_ref[k:k + 1, :]
            acc = t if acc is None else acc + t
        acc = acc + b_ref[...]
        return acc * _sigmoid(acc)

    xc = conv_silu(xbuf, cwx_ref, cbx_ref)
    bcc = conv_silu(bcbuf, cwbc_ref, cbbc_ref)
    g_n = SSD_GROUPS * n

    lane = lax.broadcasted_iota(jnp.int32, (1, LANES), 1)
    n_heads = SSD_GROUPS * gw // HEAD
    dmask = (lane >= SSD_DT_LANE) & (lane < SSD_DT_LANE + n_heads)
    dt = jnp.where(dmask, _softplus(m_ref[...] + dtb_ref[...]), 0.0)
    a = jnp.where(dmask, -jnp.exp(alog_ref[...]), 0.0)
    da = dt * a
    row = lax.broadcasted_iota(jnp.int32, (q, q), 0)
    col = lax.broadcasted_iota(jnp.int32, (q, q), 1)
    tri = row >= col
    tril = jnp.where(tri, 1.0, 0.0).astype(MXU_DTYPE)
    cs = _dot_split_rhs(tril, da, 3)
    ecs = jnp.exp(cs)
    dte = jnp.exp(cs[q - 1:q, :] - cs)
    esel = esel_ref[...]
    dt_e = _dot_split_lhs(dt, esel, 2)
    ecs_e = _dot_split_lhs(ecs, esel, 2)
    dte_e = _dot_split_lhs(dte, esel, 2)
    cs_t = cs.T

    xdt = xc * dt_e
    xdt_m = _mx(xdt)
    xd_m = _mx(xdt * dte_e)
    lane_lo = lane < HEAD

    y_cols = []
    for g in range(SSD_GROUPS):
        bg = bcc[:, g * n:(g + 1) * n]
        cg = _mx(bcc[:, g_n + g * n:g_n + (g + 1) * n])
        cb = jnp.where(tri, _dot_nt(cg, _mx(bg)), 0.0)
        gs = slice(g * gw, (g + 1) * gw)
        st = st_ref[g]
        y_off = _dot(cg, _mx(st)) * ecs_e[:, gs]
        st_ref[g] = st * ecs_e[q - 1:q, gs] + _dot(_mx(bg.T), xd_m[:, gs])
        for pr in range(gw // LANES):
            h0 = (g * gw + pr * LANES) // HEAD
            ps = slice(g * gw + pr * LANES, g * gw + (pr + 1) * LANES)
            res = []
            for hh in (h0, h0 + 1):
                li = SSD_DT_LANE + hh
                seg = cs[:, li:li + 1] - cs_t[li:li + 1, :]
                lm = jnp.exp(jnp.where(tri, seg, 0.0))
                res.append(_dot(_mx(cb * lm), xdt_m[:, ps]))
            y_diag = jnp.where(lane_lo, res[0], res[1])
            y_cols.append(y_diag + y_off[:, pr * LANES:(pr + 1) * LANES])
    y = jnp.concatenate(y_cols, axis=1) + xc * dvec_ref[...]
    zz = z_ref[...]
    y = y * (zz * _sigmoid(zz))
    outs = []
    for g in range(SSD_GROUPS):
        yg = y[:, g * gw:(g + 1) * gw]
        ms = jnp.mean(yg * yg, axis=-1, keepdims=True)
        outs.append(yg * lax.rsqrt(ms + NORM_EPS))
    y = jnp.concatenate(outs, axis=1) * ng_ref[...]
    o_ref[...] = y.astype(o_ref.dtype)


def _ssd(u1, u2, bsz, seq, conv_w, conv_b, dt_bias, a_log, d_skip, norm_g):
    q = SSD_CHUNK
    nc = seq // q
    w = W_MIX
    gw = w // SSD_GROUPS
    heads = w // HEAD
    bcw = SSD_BC
    cw = conv_w[:, 0, :]
    pad = lambda v: jnp.zeros((1, LANES), F32).at[0, SSD_DT_LANE:SSD_DT_LANE + heads].set(v)
    esel = np.zeros((LANES, w), np.float32)
    for h in range(heads):
        esel[SSD_DT_LANE + h, h * HEAD:(h + 1) * HEAD] = 1.0
    rowblk = lambda cb: (lambda b, c: (b * nc + c, cb))
    const = lambda b, c: (0, 0)
    return pl.pallas_call(
        _ssd_body,
        grid=(bsz, nc),
        in_specs=[pl.BlockSpec((q, w), rowblk(0)),
                  pl.BlockSpec((q, w), rowblk(1)),
                  pl.BlockSpec((q, bcw), rowblk(2 * w // bcw)),
                  pl.BlockSpec((q, LANES), rowblk((U2_MISC + MISC_DT) // LANES)),
                  pl.BlockSpec((SSD_CONV, w), const), pl.BlockSpec((1, w), const),
                  pl.BlockSpec((SSD_CONV, bcw), const), pl.BlockSpec((1, bcw), const),
                  pl.BlockSpec((1, LANES), const), pl.BlockSpec((1, LANES), const),
                  pl.BlockSpec((1, w), const), pl.BlockSpec((1, w), const),
                  pl.BlockSpec((LANES, w), const)],
        out_specs=pl.BlockSpec((q, w), lambda b, c: (b * nc + c, 0)),
        out_shape=jax.ShapeDtypeStruct((bsz * seq, w), MXU_DTYPE),
        scratch_shapes=[pltpu.VMEM((q + SUBLANES, w), F32),
                        pltpu.VMEM((q + SUBLANES, bcw), F32),
                        pltpu.VMEM((SSD_GROUPS, SSD_STATE, gw), F32)],
        compiler_params=_cparams(("arbitrary", "arbitrary")),
        name="ssd_mixer",
    )(u1, u1, u1, u2, cw[:, :w], conv_b[:w].reshape(1, w), cw[:, w:], conv_b[w:].reshape(1, bcw),
      pad(dt_bias), pad(a_log), jnp.repeat(d_skip, HEAD).reshape(1, w), norm_g.reshape(1, w),
      jnp.asarray(esel, MXU_DTYPE))


def _unit_lower_inverse(l_mats, blk_mask, eye_f):
    mm = lambda xs, ys: [_dot(x, y) for x, y in zip(xs, ys)]
    mx = lambda xs: [_mx(x) for x in xs]
    add = lambda xs, ys: [x + y for x, y in zip(xs, ys)]
    d = [jnp.where(blk_mask, l, 0.0) for l in l_mats]
    o = mx([l - x for l, x in zip(l_mats, d)])
    d1 = mx(d)
    d2 = mx(mm(d1, d1))
    t = [eye_f + x for x in d]
    yield
    d4 = mx(mm(d2, d2))
    t = add(t, mm(d2, mx(t)))
    yield
    d8 = mx(mm(d4, d4))
    t = add(t, mm(d4, mx(t)))
    yield
    t_d = add(t, mm(d8, mx(t)))
    t_dm = mx(t_d)
    yield
    m1 = mm(t_dm, o)
    m1m = mx(m1)
    yield
    m2 = mm(m1m, m1m)
    yield
    m3 = mm(m1m, mx(m2))
    w = mx([x + y + z for x, y, z in zip(m1, m2, m3)])
    yield
    return add(t_d, mm(w, t_dm))


def _rwkv_masks():
    lane = lax.broadcasted_iota(jnp.int32, (1, LANES), 1)
    row = lax.broadcasted_iota(jnp.int32, (LANES, LANES), 0)
    col = lax.broadcasted_iota(jnp.int32, (LANES, LANES), 1)
    same = (row >= HEAD) == (col >= HEAD)
    rs = row & (HEAD - 1)
    cs = col & (HEAD - 1)
    eye = row == col
    return dict(lane_lo=lane < HEAD, strict=same & (rs > cs), incl=same & (rs >= cs),
                blk=(row >> 4) == (col >> 4), eye=eye, eye_f=jnp.where(eye, 1.0, 0.0))


def _rwkv_head_sum(x, ones_blk):
    cols = [_dot_split_lhs(x[:, t * LANES:(t + 1) * LANES], ones_blk, 2)
            for t in range(x.shape[1] // LANES)]
    return jnp.concatenate(cols, axis=1)


def _rwkv_prepare(src, buf, mu_ref, prm, masks, first):
    cl = RWKV_CHUNK
    w0_ref, w2_ref, a0_ref, a2_ref, g2_ref, kk_ref, ka_ref, rk_ref, ones_ref = prm
    if first:
        buf[0:SUBLANES, :] = jnp.zeros((SUBLANES, buf.shape[1]), F32)
    else:
        buf[0:SUBLANES, :] = buf[cl:cl + SUBLANES, :]
    cur = src[...]
    buf[SUBLANES:SUBLANES + cl, :] = cur
    prev = pltpu.roll(buf[0:SUBLANES + cl, :], 1, 0)[SUBLANES:, :]
    row = cur + (prev - cur) * mu_ref[...]
    w = W_MIX
    r, k, v = (row[:, i * w:(i + 1) * w] for i in range(3))
    misc = row[:, U2_MISC:]
    pw_pa = misc[:, 0:2 * LANES]
    ones_blk = ones_ref[...]
    yield
    w_log = -_softplus(-(w0_ref[...] + _dot(_mx(jnp.tanh(pw_pa)), w2_ref[...]))) - 0.5
    lw = -jnp.exp(w_log)
    iclr = _sigmoid(a0_ref[...] + _dot(_mx(pw_pa), a2_ref[...]))
    gate = _dot(_mx(_sigmoid(misc[:, LANES:])), g2_ref[...])
    yield

    row_c = lax.broadcasted_iota(jnp.int32, (cl, cl), 0)
    col_c = lax.broadcasted_iota(jnp.int32, (cl, cl), 1)
    tril_c = jnp.where(row_c >= col_c, 1.0, 0.0).astype(MXU_DTYPE)
    lane_lo = masks["lane_lo"]
    stack2 = lambda x: jnp.concatenate([jnp.where(lane_lo, x, 0.0), jnp.where(lane_lo, 0.0, x)], axis=0)
    out = dict(a2=[], r2=[], b2=[], k2=[], v2=[], bkp=[])
    p_ends, bonuses = [], []
    for p in range(r.shape[1] // LANES):
        ps = slice(p * LANES, (p + 1) * LANES)
        r_p, k_p, v_p, lw_p, iclr_p = r[:, ps], k[:, ps], v[:, ps], lw[:, ps], iclr[:, ps]
        kk = k_p * kk_ref[:, ps]
        kk = kk / jnp.maximum(jnp.sqrt(_dot_split_lhs(kk * kk, ones_blk, 2)), 1e-12)
        k_p = k_p * (1.0 + (iclr_p - 1.0) * ka_ref[:, ps])
        bonuses.append(_dot_split_lhs(r_p * k_p * rk_ref[:, ps], ones_blk, 2) * v_p)
        cum = _dot_split_rhs(tril_c, lw_p, 3)
        p_in = jnp.exp(cum)
        p_inv = jnp.exp(-cum)
        a_t = -(kk * jnp.exp(cum - lw_p))
        b_t = kk * iclr_p * p_inv
        k_t = k_p * p_inv
        p_end = p_in[cl - 1:cl, :]
        out["a2"].append(_mx(stack2(a_t)))
        out["r2"].append(_mx(stack2(r_p * p_in)))
        out["b2"].append(_mx(stack2(b_t)))
        out["k2"].append(_mx(stack2(k_t)))
        out["v2"].append(_mx(stack2(v_p)))
        out["bkp"].append(_mx(jnp.concatenate([stack2(b_t * p_end), stack2(k_t * p_end)], axis=0).T))
        p_ends.append(p_end)
        yield
    out.update(p_end=jnp.concatenate(p_ends, axis=1), bonus=jnp.concatenate(bonuses, axis=1), gate=gate)
    return out


def _rwkv_scan(ops_list, st_ref, ybuf, lnw_ref, lnb_ref, ones_ref, masks):
    cl = RWKV_CHUNK
    gather = lambda name: [x for ops in ops_list for x in ops[name]]
    a2, r2, b2, k2, v2, bkp_t = (gather(name) for name in _RWKV_STAGED)
    n_tiles = len(ops_list[0]["a2"])
    tiles = range(len(a2))
    cat0 = lambda *xs: jnp.concatenate(xs, axis=0)
    cat1 = lambda *xs: jnp.concatenate(xs, axis=1)
    gram = [_dot_nt(cat0(a2[t], r2[t]), cat0(b2[t], k2[t])) for t in tiles]
    yield
    a_ab = [jnp.where(masks["strict"], g[0:LANES, 0:LANES], 0.0) for g in gram]
    a_ak = [_mx(jnp.where(masks["strict"], g[0:LANES, LANES:], 0.0)) for g in gram]
    a_rb = [_mx(jnp.where(masks["incl"], g[LANES:, 0:LANES], 0.0)) for g in gram]
    a_rk = [_mx(jnp.where(masks["incl"], g[LANES:, LANES:], 0.0)) for g in gram]
    t_inv = yield from _unit_lower_inverse(a_ab, masks["blk"], masks["eye_f"])
    st = [st_ref[t] for t in tiles]
    st_m = [_mx(s) for s in st]
    y0 = [_dot(cat1(a2[t], a_ak[t]), cat0(st_m[t], v2[t])) for t in tiles]
    yield
    sa = [_mx(_dot(_mx(t_inv[t]), _mx(y0[t]))) for t in tiles]
    yield
    o2 = [_dot(cat1(r2[t], a_rb[t], a_rk[t]), cat0(st_m[t], sa[t], v2[t])) for t in tiles]
    yield
    upd = [_dot(bkp_t[t], cat0(sa[t], v2[t])) for t in tiles]
    for t in tiles:
        b, p = divmod(t, n_tiles)
        ps = slice(p * LANES, (p + 1) * LANES)
        ybuf[b, :, ps] = o2[t][0:cl, :] + o2[t][cl:, :]
        p_end = ops_list[b]["p_end"][:, ps]
        p_col = jnp.sum(jnp.where(masks["eye"], p_end, 0.0), axis=1, keepdims=True)
        st_ref[t] = st[t] * p_col + upd[t]
    yield
    ones_blk = ones_ref[...]
    inv_n = 1.0 / HEAD
    outs = []
    for b, ops in enumerate(ops_list):
        y = ybuf[b]
        mean = _rwkv_head_sum(y, ones_blk) * inv_n
        d = y - mean
        var = _rwkv_head_sum(d * d, ones_blk) * inv_n
        y = d * lax.rsqrt(var + RWKV_LN_EPS) * lnw_ref[...] + lnb_ref[...]
        outs.append((y + ops["bonus"]) * ops["gate"])
    return outs


def _run(gen):
    while True:
        try:
            next(gen)
        except StopIteration as stop:
            return stop.value


def _interleave(scan_gen, prep_gens):
    gens = [scan_gen] + list(prep_gens)
    results, done = [None] * len(gens), [False] * len(gens)
    while not all(done):
        for i, gen in enumerate(gens):
            if not done[i]:
                try:
                    next(gen)
                except StopIteration as stop:
                    results[i], done[i] = stop.value, True
    return results[0], results[1:]


_RWKV_STAGED = ("a2", "r2", "b2", "k2", "v2", "bkp")
RWKV_SEQS_PER_STEP = 2


def _rwkv_body(*refs):
    cl = RWKV_CHUNK
    first_src, odd_src, next_src, mu_ref = refs[0:4]
    prm = refs[4:12] + (refs[14],)
    lnw_ref, lnb_ref, ones_ref = refs[12], refs[13], refs[14]
    o_ref = refs[15]
    buf = refs[16]
    st_ref, ybuf = refs[17], refs[18]
    staged = dict(zip(_RWKV_STAGED, refs[19:25]))
    s_pend, s_bonus, s_gate = refs[25], refs[26], refs[27]
    masks = _rwkv_masks()
    seqs = range(o_ref.shape[0])
    n_tiles = st_ref.shape[0] // len(seqs)

    def prepare(src, b, first):
        return _rwkv_prepare(src.at[b], buf.at[b], mu_ref, prm, masks, first)

    def stash(b, ops):
        for name in _RWKV_STAGED:
            for p in range(n_tiles):
                staged[name][b * n_tiles + p] = ops[name][p]
        s_pend[b] = ops["p_end"]
        s_bonus[b] = ops["bonus"]
        s_gate[b] = ops["gate"]

    def staged_ops(b):
        ops = {name: [staged[name][b * n_tiles + p] for p in range(n_tiles)] for name in _RWKV_STAGED}
        ops.update(p_end=s_pend[b], bonus=s_bonus[b], gate=s_gate[b])
        return ops

    @pl.when(pl.program_id(1) == 0)
    def _():
        st_ref[...] = jnp.zeros(st_ref.shape, F32)
        for b in seqs:
            stash(b, _run(prepare(first_src, b, True)))

    scan = lambda ops_list: _rwkv_scan(ops_list, st_ref, ybuf, lnw_ref, lnb_ref, ones_ref, masks)
    y_even, odd = _interleave(scan([staged_ops(b) for b in seqs]), [prepare(odd_src, b, False) for b in seqs])
    for b in seqs:
        o_ref[b, 0:cl, :] = y_even[b].astype(o_ref.dtype)
    y_odd, nxt = _interleave(scan(odd), [prepare(next_src, b, False) for b in seqs])
    for b in seqs:
        o_ref[b, cl:2 * cl, :] = y_odd[b].astype(o_ref.dtype)
        stash(b, nxt[b])


def _rwkv(u2, bsz, seq, mu, w0, w2, a0, a2, g2, k_k, k_a, r_k, ln_w, ln_b):
    cl = RWKV_CHUNK
    nc = seq // cl
    nb = RWKV_SEQS_PER_STEP
    assert nc % 2 == 0 and bsz % nb == 0
    w = W_MIX
    mw = RWKV_MISC
    uw = u2.shape[1]
    n_tiles = w // LANES
    row1 = lambda vec: vec.reshape(1, -1)
    mu_row = jnp.zeros((uw,), F32).at[0:mu.shape[0]].set(mu)
    w2e = jnp.zeros((2 * LANES, w), F32).at[MISC_PW:MISC_PA].set(w2)
    a2e = jnp.zeros((2 * LANES, w), F32).at[MISC_PA:MISC_PG].set(a2)
    g2e = jnp.zeros((mw - LANES, w), F32).at[MISC_PG - LANES:MISC_DT - LANES].set(g2)
    ones_blk = np.kron(np.eye(LANES // HEAD, dtype=np.float32), np.ones((HEAD, HEAD), np.float32))

    def chunk_spec(chunk_of):
        return pl.BlockSpec((nb, cl, uw), lambda bb, j: (bb, chunk_of(j), 0))

    cvec = lambda width: pl.BlockSpec((1, width), lambda bb, j: (0, 0))
    cmat = lambda rows, width: pl.BlockSpec((rows, width), lambda bb, j: (0, 0))
    stage = lambda cols: pltpu.VMEM((nb * n_tiles, LANES, cols), MXU_DTYPE)
    out = pl.pallas_call(
        _rwkv_body,
        grid=(bsz // nb, nc // 2),
        in_specs=([chunk_spec(lambda j: 0), chunk_spec(lambda j: 2 * j + 1),
                   chunk_spec(lambda j: jnp.minimum(2 * j + 2, nc - 1)), cvec(uw)]
                  + [cvec(w), cmat(2 * LANES, w), cvec(w), cmat(2 * LANES, w), cmat(mw - LANES, w),
                     cvec(w), cvec(w), cvec(w), cvec(w), cvec(w), cmat(LANES, LANES)]),
        out_specs=pl.BlockSpec((nb, 2 * cl, w), lambda bb, j: (bb, j, 0)),
        out_shape=jax.ShapeDtypeStruct((bsz, seq, w), MXU_DTYPE),
        scratch_shapes=[pltpu.VMEM((nb, cl + SUBLANES, uw), F32),
                        pltpu.VMEM((nb * n_tiles, LANES, LANES), F32),
                        pltpu.VMEM((nb, cl, w), F32),
                        stage(LANES), stage(LANES), stage(LANES), stage(LANES), stage(LANES),
                        stage(2 * LANES),
                        pltpu.VMEM((nb, 1, w), F32), pltpu.VMEM((nb, cl, w), F32),
                        pltpu.VMEM((nb, cl, w), F32)],
        compiler_params=_cparams(("arbitrary", "arbitrary")),
        name="rwkv7_mixer",
    )(*([u2.reshape(bsz, seq, uw)] * 3), row1(mu_row), row1(w0), _mx(w2e), row1(a0), _mx(a2e), _mx(g2e), row1(k_k), row1(k_a),
      row1(r_k), row1(ln_w), row1(ln_b), jnp.asarray(ones_blk, MXU_DTYPE))
    return out.reshape(bsz * seq, w)


def _attn_body(q_ref, k_ref, v_ref, o_ref):
    d = q_ref.shape[1]
    hd = d // XATTN_HEADS
    scale = hd ** -0.5
    heads = [slice(h * hd, (h + 1) * hd) for h in range(XATTN_HEADS)]
    scores = [_dot_nt(q_ref[:, hs], k_ref[0, :, hs]) * scale for hs in heads]
    probs = []
    for s in scores:
        e = jnp.exp(s - jnp.max(s, axis=-1, keepdims=True))
        probs.append(_mx(e / jnp.sum(e, axis=-1, keepdims=True)))
    for hs, p in zip(heads, probs):
        o_ref[:, hs] = _dot(p, v_ref[0, :, hs]).astype(o_ref.dtype)


def _attention(q, k, v, bsz, seq, ts=2048):
    d = q.shape[1]
    n_mem = k.shape[1]
    ts = min(ts, seq)
    ns = seq // ts
    return pl.pallas_call(
        _attn_body,
        grid=(bsz, ns),
        in_specs=[pl.BlockSpec((ts, d), lambda b, s: (b * ns + s, 0)),
                  pl.BlockSpec((1, n_mem, d), lambda b, s: (b, 0, 0)),
                  pl.BlockSpec((1, n_mem, d), lambda b, s: (b, 0, 0))],
        out_specs=pl.BlockSpec((ts, d), lambda b, s: (b * ns + s, 0)),
        out_shape=jax.ShapeDtypeStruct((bsz * seq, d), MXU_DTYPE),
        compiler_params=_cparams(("arbitrary", "arbitrary")),
        name="mem_attention",
    )(q, k, v)


def _rwkv_side(mat, axis):
    ssd_in = U1_COLS + W_MIX // HEAD
    take = lambda lo, hi: lax.slice_in_dim(mat, lo, hi, axis=axis)
    pad_shape = list(mat.shape)
    pad_shape[axis] = U2_COLS - (mat.shape[axis] - U1_COLS)
    return jnp.concatenate([take(ssd_in, mat.shape[axis]), take(U1_COLS, ssd_in),
                            jnp.zeros(pad_shape, mat.dtype)], axis=axis)


def kernel(x, mem, norm_mix_g, w_in, ssd_conv_w, ssd_conv_b, ssd_dt_bias, ssd_a_log, ssd_d, ssd_norm_g, rwkv_mu, rwkv_w0, rwkv_w2, rwkv_a0, rwkv_a2, rwkv_g2, rwkv_k_k, rwkv_k_a, rwkv_r_k, rwkv_ln_w, rwkv_ln_b, w_out, norm_x_g, norm_mem_g, xattn_wq, xattn_wk, xattn_wv, xattn_wo, norm_ffn_g, ffn_w1, ffn_w2, final_norm_g):
    bsz, seq, d = x.shape
    n_mem = mem.shape[1]
    xr = x.reshape(bsz * seq, d)
    memr = mem.reshape(bsz * n_mem, d)
    n_layers = w_in.shape[0]
    for l in range(n_layers):
        w_in_t = w_in[l].T
        u1 = _matmul_resident(xr, w_in_t, F32, n=U1_COLS, b_is_transposed=True,
                              norm_g=norm_mix_g[l], tm=512, tn=U1_COLS // 2, name="in_proj_ssd")
        u2 = _matmul_resident(xr, _rwkv_side(w_in_t, 0), F32, b_is_transposed=True,
                              norm_g=norm_mix_g[l], tm=512, tn=U2_COLS // 2, name="in_proj_rwkv")
        y_ssd = _ssd(u1, u2, bsz, seq, ssd_conv_w[l], ssd_conv_b[l], ssd_dt_bias[l], ssd_a_log[l],
                     ssd_d[l], ssd_norm_g[l])
        y_rwkv = _rwkv(u2, bsz, seq, rwkv_mu[l], rwkv_w0[l], rwkv_w2[l], rwkv_a0[l], rwkv_a2[l],
                       rwkv_g2[l], rwkv_k_k[l], rwkv_k_a[l], rwkv_r_k[l].reshape(-1),
                       rwkv_ln_w[l], rwkv_ln_b[l])
        wo = _mx(w_out[l])
        xr, h, wo_attn = _matmul_rows([(y_ssd, wo, 0), (y_rwkv, wo, W_MIX)], xr, norm_x_g[l],
                                      MXU_DTYPE, side_cast=xattn_wo[l], name="out_proj")

        m = _rmsnorm(memr, norm_mem_g[l], MXU_DTYPE)
        q = _matmul_resident(h, xattn_wq[l], MXU_DTYPE, tm=2048, name="q_proj")
        kx = _matmul_resident(m, xattn_wk[l], MXU_DTYPE, name="k_proj")
        vx = _matmul_resident(m, xattn_wv[l], MXU_DTYPE, name="v_proj")
        o = _attention(q, kx.reshape(bsz, n_mem, d), vx.reshape(bsz, n_mem, d), bsz, seq)
        xr, h, w2 = _matmul_rows([(o, wo_attn, 0)], xr, norm_ffn_g[l], MXU_DTYPE,
                                 side_cast=ffn_w2[l], name="o_proj")

        hid = _matmul_resident(h, ffn_w1[l], MXU_DTYPE, act="relu2", tm=2048, name="ffn_up")
        down = [(hid, w2, 0)]
        if l + 1 < n_layers:
            xr, _ = _matmul_rows(down, xr, norm_mix_g[l + 1], MXU_DTYPE, name="ffn_down")
        else:
            out = _matmul_rows(down, xr, final_norm_g, x.dtype, emit_x=False, name="ffn_down")
    return out.reshape(bsz, seq, d)
```
